```python
import jax, jax.numpy as jnp
from jax import lax
import numpy as np

D_MODEL = 1024
BATCH = 8
SEQ = 2048
DEPTH = 4
DEC_BATCH = 128
DEC_SEQ = 8
PAST_LEN = 16384
PAGE_SIZE = 128

N_EVEN = (DEPTH + 1) // 2
N_ODD = DEPTH // 2
BRANCH = D_MODEL // 2
W_A = BRANCH
N_HEADS_A = 8
CONV_A = 3
W_B = BRANCH
POOL_WINDOWS = (2, 4, 8, 16)
N_POOL_GROUPS = len(POOL_WINDOWS)
POOL_GROUP = W_B // N_POOL_GROUPS
POOL_HIST = max(POOL_WINDOWS) - 1
W_C = BRANCH
CHUNK = 128
N_SGU_GROUPS = 4
SGU_GROUP = W_C // N_SGU_GROUPS
W_D = BRANCH
N_HEADS_D = 8
CONV_D = 31
IN_EVEN = 4 * W_A + 2 * W_B
IN_ODD = 3 * W_C + 3 * W_D
OUT_EVEN = W_A + W_B
OUT_ODD = W_C + W_D
EPS = 1e-6

kernel_name = "hybrid_conv_pool_sgu_conformer_decode_step"


def rms_norm(x, g):
    xf = x.astype(jnp.float32)
    y = xf * lax.rsqrt(jnp.mean(xf * xf, axis=-1, keepdims=True) + EPS)
    return (y * g.astype(jnp.float32)).astype(x.dtype)


def layer_norm(x, g, b):
    xf = x.astype(jnp.float32)
    mu = jnp.mean(xf, axis=-1, keepdims=True)
    xc = xf - mu
    y = xc * lax.rsqrt(jnp.mean(xc * xc, axis=-1, keepdims=True) + EPS)
    return (y * g.astype(jnp.float32) + b.astype(jnp.float32)).astype(x.dtype)


def causal_depthwise_conv(x, hist, w):
    k = w.shape[0]
    xe = jnp.concatenate([hist.astype(x.dtype), x], axis=1)
    y = lax.conv_general_dilated(
        xe, w[:, None, :].astype(x.dtype), window_strides=(1,), padding="VALID",
        dimension_numbers=("NWC", "WIO", "NWC"), feature_group_count=x.shape[-1])
    return y, xe[:, xe.shape[1] - (k - 1):]


def short_conv_mixer(zb, zc, zh, zg, hist, conv_w):
    s = zc * zh
    y, new_hist = causal_depthwise_conv(s, hist, conv_w)
    return zb * y * jax.nn.silu(zg), new_hist


def multiscale_pool_mixer(p, zg, hist, pool_w, pool_scale, start_pos):
    bsz, t = p.shape[0], p.shape[1]
    pe = jnp.concatenate([hist.astype(p.dtype), p], axis=1)
    pf = pe.astype(jnp.float32)
    cs = jnp.concatenate([jnp.zeros((bsz, 1, W_B), jnp.float32), jnp.cumsum(pf, axis=1)], axis=1)
    pos = start_pos + jnp.arange(t, dtype=jnp.int32)
    end = POOL_HIST + 1
    means = []
    for gi, w in enumerate(POOL_WINDOWS):
        sl = slice(gi * POOL_GROUP, (gi + 1) * POOL_GROUP)
        win_sum = cs[:, end:end + t, sl] - cs[:, end - w:end - w + t, sl]
        cnt = jnp.minimum(pos + 1, w).astype(jnp.float32)[None, :, None]
        means.append(win_sum / cnt)
    d = jnp.concatenate(means, axis=-1) - pf[:, POOL_HIST:]
    d = d.reshape(bsz, t, N_POOL_GROUPS, POOL_GROUP)
    mixed = jnp.einsum("btgc,gcd->btgd", d, pool_w.astype(jnp.float32)).reshape(bsz, t, W_B)
    mixed = mixed * pool_scale.astype(jnp.float32)
    return mixed.astype(p.dtype) * jax.nn.silu(zg), pe[:, pe.shape[1] - POOL_HIST:]


def chunk_sgu_mixer(u, v, zg, ln_g, ln_b, w_s, b_s):
    bsz, t = u.shape[0], u.shape[1]
    lch = min(t, CHUNK)
    n = t // lch
    vn = layer_norm(v, ln_g, ln_b)
    vr = vn.reshape(bsz, n, lch, N_SGU_GROUPS, SGU_GROUP)
    mask = jnp.tril(jnp.ones((lch, lch), dtype=bool))
    wm = jnp.where(mask[None], w_s[:, :lch, :lch], 0.0).astype(v.dtype)
    mixed = jnp.einsum("gij,bnjgc->bnigc", wm, vr) + b_s[:, :lch].T.astype(v.dtype)[None, None, :, :, None]
    return u * mixed.reshape(bsz, t, W_C) * jax.nn.silu(zg), vn


def conformer_conv_mixer(za, zb, zg, hist, dw_w, dw_b, ln_g, ln_b):
    glu = za * jax.nn.sigmoid(zb)
    y, new_hist = causal_depthwise_conv(glu, hist, dw_w)
    y = layer_norm(y + dw_b.astype(y.dtype), ln_g, ln_b)
    return jax.nn.silu(y) * jax.nn.silu(zg), new_hist


def even_layer(x, hist_a, hist_b, start_pos, pre_g, post_g, w_in, w_out, conv_w, pool_w, pool_scale):
    h = rms_norm(x, pre_g)
    z = jnp.einsum("btd,de->bte", h, w_in)
    zb, zc, zh, ga, p, gb = jnp.split(z, 6, axis=-1)
    ya, new_a = short_conv_mixer(zb, zc, zh, ga, hist_a, conv_w)
    yb, new_b = multiscale_pool_mixer(p, gb, hist_b, pool_w, pool_scale, start_pos)
    o = jnp.einsum("bte,ed->btd", jnp.concatenate([ya, yb], axis=-1), w_out)
    return x + rms_norm(o, post_g), new_a, new_b


def odd_layer(x, hist_d, pre_g, post_g, w_in, w_out, ln_g, ln_b, w_s, b_s, dw_w, dw_b, cln_g, cln_b):
    h = rms_norm(x, pre_g)
    z = jnp.einsum("btd,de->bte", h, w_in)
    u, v, gc, za, zb, gd = jnp.split(z, 6, axis=-1)
    yc, vn = chunk_sgu_mixer(u, v, gc, ln_g, ln_b, w_s, b_s)
    yd, new_d = conformer_conv_mixer(za, zb, gd, hist_d, dw_w, dw_b, cln_g, cln_b)
    o = jnp.einsum("bte,ed->btd", jnp.concatenate([yc, yd], axis=-1), w_out)
    return x + rms_norm(o, post_g), new_d, vn


def setup_inputs(seed: int = 0) -> dict:
    key = jax.random.key(seed)
    ks = jax.random.split(key, 24)
    f32 = jnp.float32
    nrm = lambda k, shape, s: jax.random.normal(k, shape, f32) * s
    return {
        "x_prompt": nrm(ks[0], (BATCH, SEQ, D_MODEL), 1.0),
        "x_sample": nrm(ks[1], (DEC_BATCH, DEC_SEQ, D_MODEL), 1.0),
        "state_conv_a": nrm(ks[2], (N_EVEN, DEC_BATCH, CONV_A - 1, W_A), 1.0),
        "state_pool_b": nrm(ks[3], (N_EVEN, DEC_BATCH, POOL_HIST, W_B), 1.0),
        "state_conv_d": nrm(ks[4], (N_ODD, DEC_BATCH, CONV_D - 1, W_D), 1.0),
        "norm_pre": 1.0 + nrm(ks[5], (DEPTH, D_MODEL), 0.05),
        "norm_post": 1.0 + nrm(ks[6], (DEPTH, D_MODEL), 0.05),
        "w_in_even": nrm(ks[7], (N_EVEN, D_MODEL, IN_EVEN), D_MODEL ** -0.5),
        "w_out_even": nrm(ks[8], (N_EVEN, OUT_EVEN, D_MODEL), OUT_EVEN ** -0.5),
        "conv_a_w": nrm(ks[9], (N_EVEN, CONV_A, W_A), CONV_A ** -0.5),
        "pool_w": nrm(ks[10], (N_EVEN, N_POOL_GROUPS, POOL_GROUP, POOL_GROUP), POOL_GROUP ** -0.5),
        "pool_scale": 1.0 + nrm(ks[11], (N_EVEN, W_B), 0.05),
        "w_in_odd": nrm(ks[12], (N_ODD, D_MODEL, IN_ODD), D_MODEL ** -0.5),
        "w_out_odd": nrm(ks[13], (N_ODD, OUT_ODD, D_MODEL), OUT_ODD ** -0.5),
        "sgu_ln_g": 1.0 + nrm(ks[14], (N_ODD, W_C), 0.05),
        "sgu_ln_b": nrm(ks[15], (N_ODD, W_C), 0.02),
        "sgu_w": nrm(ks[16], (N_ODD, N_SGU_GROUPS, CHUNK, CHUNK), 0.02),
        "sgu_b": 1.0 + nrm(ks[17], (N_ODD, N_SGU_GROUPS, CHUNK), 0.02),
        "conf_dw_w": nrm(ks[18], (N_ODD, CONV_D, W_D), CONV_D ** -0.5),
        "conf_dw_b": nrm(ks[19], (N_ODD, W_D), 0.02),
        "conf_ln_g": 1.0 + nrm(ks[20], (N_ODD, W_D), 0.05),
        "conf_ln_b": nrm(ks[21], (N_ODD, W_D), 0.02),
    }


def reference(x_prompt, x_sample, state_conv_a, state_pool_b, state_conv_d,
              norm_pre, norm_post, w_in_even, w_out_even, conv_a_w, pool_w, pool_scale,
              w_in_odd, w_out_odd, sgu_ln_g, sgu_ln_b, sgu_w, sgu_b,
              conf_dw_w, conf_dw_b, conf_ln_g, conf_ln_b):
    xp, xs = x_prompt, x_sample
    bp = xp.shape[0]
    zero_a = jnp.zeros((bp, CONV_A - 1, W_A), xp.dtype)
    zero_b = jnp.zeros((bp, POOL_HIST, W_B), xp.dtype)
    zero_d = jnp.zeros((bp, CONV_D - 1, W_D), xp.dtype)
    ca_p, ca_s, pb_p, pb_s, cd_p, cd_s, v_s = [], [], [], [], [], [], []
    for l in range(DEPTH):
        i = l // 2
        if l % 2 == 0:
            args = (norm_pre[l], norm_post[l], w_in_even[i], w_out_even[i], conv_a_w[i], pool_w[i], pool_scale[i])
            xp, a_p, b_p = even_layer(xp, zero_a, zero_b, 0, *args)
            xs, a_s, b_s = even_layer(xs, state_conv_a[i], state_pool_b[i], PAST_LEN, *args)
            ca_p.append(a_p); ca_s.append(a_s); pb_p.append(b_p); pb_s.append(b_s)
        else:
            args = (norm_pre[l], norm_post[l], w_in_odd[i], w_out_odd[i], sgu_ln_g[i], sgu_ln_b[i],
                    sgu_w[i], sgu_b[i], conf_dw_w[i], conf_dw_b[i], conf_ln_g[i], conf_ln_b[i])
            xp, d_p, _ = odd_layer(xp, zero_d, *args)
            xs, d_s, vn_s = odd_layer(xs, state_conv_d[i], *args)
            cd_p.append(d_p); cd_s.append(d_s); v_s.append(vn_s)
    new_conv_a_prompt = jnp.stack(ca_p)
    new_conv_a_sample = jnp.stack(ca_s)
    new_pool_b_prompt = jnp.stack(pb_p)
    new_pool_b_sample = jnp.stack(pb_s)
    new_conv_d_prompt = jnp.stack(cd_p)
    new_conv_d_sample = jnp.stack(cd_s)
    new_chunk_v_sample = jnp.stack(v_s)
    return (xp, xs, new_conv_a_prompt, new_conv_a_sample, new_pool_b_prompt, new_pool_b_sample,
            new_conv_d_prompt, new_conv_d_sample, new_chunk_v_sample)
```

```python
import functools

import jax
import jax.numpy as jnp
from jax import lax
from jax.experimental import pallas as pl
from jax.experimental.pallas import tpu as pltpu

D_MODEL = 1024
BRANCH = 512
LANES = 128
N_SLABS = BRANCH // LANES
IN_WIDTH = 6 * BRANCH
CONV_A = 3
POOL_WINDOWS = (2, 4, 8, 16)
POOL_HIST = max(POOL_WINDOWS) - 1
CHUNK = 128
CONV_D = 31
PAST_LEN = 16384
EPS = 1e-6

PAD_A = 8
PAD_B = 16
PAD_D = 32

VMEM_LIMIT_BYTES = 56 * 1024 * 1024


def _rms_norm(x, g):
    ms = jnp.mean(x * x, axis=-1, keepdims=True)
    return x * lax.rsqrt(ms + EPS) * g


def _layer_norm(x, g, b):
    mu = jnp.mean(x, axis=-1, keepdims=True)
    xc = x - mu
    var = jnp.mean(xc * xc, axis=-1, keepdims=True)
    return xc * lax.rsqrt(var + EPS) * g + b


def _sigmoid(x):
    return 0.5 * jnp.tanh(0.5 * x) + 0.5


def _silu(x):
    return x * _sigmoid(x)


def _slab(c):
    return slice(c * LANES, (c + 1) * LANES)


def _carry_history(ext, hist_ref, pad, n_hist, tt, n_t):
    j = pl.program_id(1)

    @pl.when(j == 0)
    def _():
        for c in range(N_SLABS):
            ext[:, c, pad - n_hist:pad, :] = hist_ref[:, :, _slab(c)]

    if n_t > 1:
        @pl.when(j > 0)
        def _():
            for c in range(N_SLABS):
                ext[:, c, pad - n_hist:pad, :] = ext[:, c, pad + tt - n_hist:pad + tt, :]


def _emit_history(ext, new_ref, pad, n_hist, tt, n_t):
    @pl.when(pl.program_id(1) == n_t - 1)
    def _():
        for c in range(N_SLABS):
            new_ref[:, :, _slab(c)] = ext[:, c, pad + tt - n_hist:pad + tt, :]


def _store_slabs(ext, val, pad, bb, tt):
    for c in range(N_SLABS):
        ext[:, c, pad:pad + tt, :] = val[:, _slab(c)].reshape(bb, tt, LANES)


def _causal_conv(ext, w_ref, pad, n_taps, bb, tt):
    m = bb * tt
    outs = []
    for c in range(N_SLABS):
        acc = None
        for k in range(n_taps):
            lo = pad - (n_taps - 1) + k
            term = ext[:, c, lo:lo + tt, :] * w_ref[k:k + 1, _slab(c)]
            acc = term if acc is None else acc + term
        outs.append(acc.reshape(m, LANES))
    return jnp.concatenate(outs, axis=-1)


def _project_in(x_ref, pre_ref, win_ref, m):
    x = x_ref[...].reshape(m, D_MODEL)
    h = _rms_norm(x, pre_ref[...]).astype(jnp.bfloat16)
    return jnp.dot(h, win_ref[...], preferred_element_type=jnp.float32)


def _project_out(x_ref, y_ref, post_ref, wout_ref, y1, y2, bb, tt):
    m = bb * tt
    o = jnp.dot(y1.astype(jnp.bfloat16), wout_ref[:BRANCH, :], preferred_element_type=jnp.float32)
    o = o + jnp.dot(y2.astype(jnp.bfloat16), wout_ref[BRANCH:, :], preferred_element_type=jnp.float32)
    out = x_ref[...].reshape(m, D_MODEL) + _rms_norm(o, post_ref[...])
    y_ref[...] = out.reshape(bb, tt, D_MODEL)


def _even_kernel(x_ref, ha_ref, hb_ref, pre_ref, post_ref, win_ref, wout_ref, cw_ref, pw_ref, ps_ref,
                 y_ref, na_ref, nb_ref, ext_a, ext_b, *, bb, tt, n_t, start_pos):
    m = bb * tt
    _carry_history(ext_a, ha_ref, PAD_A, CONV_A - 1, tt, n_t)
    _carry_history(ext_b, hb_ref, PAD_B, POOL_HIST, tt, n_t)

    z = _project_in(x_ref, pre_ref, win_ref, m)
    zb, zc, zh, ga, p, gb = (z[:, i * BRANCH:(i + 1) * BRANCH] for i in range(6))

    _store_slabs(ext_a, zc * zh, PAD_A, bb, tt)
    ya = zb * _causal_conv(ext_a, cw_ref, PAD_A, CONV_A, bb, tt) * _silu(ga)

    _store_slabs(ext_b, p, PAD_B, bb, tt)
    pos = start_pos + pl.program_id(1) * tt + lax.broadcasted_iota(jnp.int32, (bb, tt, LANES), 1)
    mixed = []
    for g, w in enumerate(POOL_WINDOWS):
        acc = ext_b[:, g, PAD_B:PAD_B + tt, :]
        for i in range(1, w):
            acc = acc + ext_b[:, g, PAD_B - i:PAD_B - i + tt, :]
        cnt = jnp.minimum(pos + 1, w).astype(jnp.float32)
        d = (acc / cnt).reshape(m, LANES) - p[:, _slab(g)]
        mixed.append(jnp.dot(d.astype(jnp.bfloat16), pw_ref[g], preferred_element_type=jnp.float32))
    yb = jnp.concatenate(mixed, axis=-1) * ps_ref[...] * _silu(gb)

    _project_out(x_ref, y_ref, post_ref, wout_ref, ya, yb, bb, tt)
    _emit_history(ext_a, na_ref, PAD_A, CONV_A - 1, tt, n_t)
    _emit_history(ext_b, nb_ref, PAD_B, POOL_HIST, tt, n_t)


def _odd_kernel(x_ref, hd_ref, pre_ref, post_ref, win_ref, wout_ref, lng_ref, lnb_ref, ws_ref, bs_ref,
                dw_ref, db_ref, cg_ref, cb_ref, y_ref, nd_ref, *rest, bb, tt, n_t, mix_len, emit_vn):
    if emit_vn:
        vn_ref, ext_d = rest
    else:
        (ext_d,) = rest
    m = bb * tt
    _carry_history(ext_d, hd_ref, PAD_D, CONV_D - 1, tt, n_t)

    z = _project_in(x_ref, pre_ref, win_ref, m)
    u, v, gc, za, zb, gd = (z[:, i * BRANCH:(i + 1) * BRANCH] for i in range(6))

    vn = _layer_norm(v, lng_ref[...], lnb_ref[...])
    if emit_vn:
        vn_ref[...] = vn.reshape(bb, tt, BRANCH)
    vnb = vn.astype(jnp.bfloat16)
    row = lax.broadcasted_iota(jnp.int32, (CHUNK, CHUNK), 0)
    col = lax.broadcasted_iota(jnp.int32, (CHUNK, CHUNK), 1)
    keep = (row // mix_len == col // mix_len) & (col <= row)
    wm = [jnp.where(keep, ws_ref[g], 0.0).astype(jnp.bfloat16) for g in range(N_SLABS)]
    rows = []
    for ch in range(m // CHUNK):
        blk = vnb[ch * CHUNK:(ch + 1) * CHUNK, :]
        rows.append(jnp.concatenate(
            [jnp.dot(wm[g], blk[:, _slab(g)], preferred_element_type=jnp.float32) + bs_ref[:, g:g + 1]
             for g in range(N_SLABS)], axis=-1))
    yc = u * jnp.concatenate(rows, axis=0) * _silu(gc)

    _store_slabs(ext_d, za * _sigmoid(zb), PAD_D, bb, tt)
    y = _causal_conv(ext_d, dw_ref, PAD_D, CONV_D, bb, tt) + db_ref[...]
    yd = _silu(_layer_norm(y, cg_ref[...], cb_ref[...])) * _silu(gd)

    _project_out(x_ref, y_ref, post_ref, wout_ref, yc, yd, bb, tt)
    _emit_history(ext_d, nd_ref, PAD_D, CONV_D - 1, tt, n_t)


def _const_spec(shape):
    return pl.BlockSpec(shape, lambda b, j: (0,) * len(shape))


def _compiler_params():
    return pltpu.CompilerParams(dimension_semantics=("arbitrary", "arbitrary"),
                                vmem_limit_bytes=VMEM_LIMIT_BYTES)


def _even_layer(x, hist_a, hist_b, pre_g, post_g, w_in, w_out, conv_w, pool_w, pool_scale, *, bb, tt, start_pos):
    bsz, t, _ = x.shape
    n_t = t // tt
    hist_spec = lambda n: pl.BlockSpec((bb, n, BRANCH), lambda b, j: (b, 0, 0))
    x_spec = pl.BlockSpec((bb, tt, D_MODEL), lambda b, j: (b, j, 0))
    return pl.pallas_call(
        functools.partial(_even_kernel, bb=bb, tt=tt, n_t=n_t, start_pos=start_pos),
        grid=(bsz // bb, n_t),
        in_specs=[x_spec, hist_spec(CONV_A - 1), hist_spec(POOL_HIST),
                  _const_spec((1, D_MODEL)), _const_spec((1, D_MODEL)),
                  _const_spec((D_MODEL, IN_WIDTH)), _const_spec((2 * BRANCH, D_MODEL)),
                  _const_spec((CONV_A, BRANCH)), _const_spec((N_SLABS, LANES, LANES)),
                  _const_spec((1, BRANCH))],
        out_specs=[x_spec, hist_spec(CONV_A - 1), hist_spec(POOL_HIST)],
        out_shape=[jax.ShapeDtypeStruct(x.shape, x.dtype),
                   jax.ShapeDtypeStruct((bsz, CONV_A - 1, BRANCH), x.dtype),
                   jax.ShapeDtypeStruct((bsz, POOL_HIST, BRANCH), x.dtype)],
        scratch_shapes=[pltpu.VMEM((bb, N_SLABS, PAD_A + tt, LANES), jnp.float32),
                        pltpu.VMEM((bb, N_SLABS, PAD_B + tt, LANES), jnp.float32)],
        compiler_params=_compiler_params(),
        name="even_layer",
    )(x, hist_a, hist_b, pre_g, post_g, w_in, w_out, conv_w, pool_w, pool_scale)


def _odd_layer(x, hist_d, pre_g, post_g, w_in, w_out, ln_g, ln_b, w_s, b_s, dw_w, dw_b, cln_g, cln_b,
               *, bb, tt, mix_len, emit_vn):
    bsz, t, _ = x.shape
    n_t = t // tt
    x_spec = pl.BlockSpec((bb, tt, D_MODEL), lambda b, j: (b, j, 0))
    hist_spec = pl.BlockSpec((bb, CONV_D - 1, BRANCH), lambda b, j: (b, 0, 0))
    row = lambda n: _const_spec((1, n))
    out_specs = [x_spec, hist_spec]
    out_shape = [jax.ShapeDtypeStruct(x.shape, x.dtype),
                 jax.ShapeDtypeStruct((bsz, CONV_D - 1, BRANCH), x.dtype)]
    if emit_vn:
        out_specs.append(pl.BlockSpec((bb, tt, BRANCH), lambda b, j: (b, j, 0)))
        out_shape.append(jax.ShapeDtypeStruct((bsz, t, BRANCH), x.dtype))
    return pl.pallas_call(
        functools.partial(_odd_kernel, bb=bb, tt=tt, n_t=n_t, mix_len=mix_len, emit_vn=emit_vn),
        grid=(bsz // bb, n_t),
        in_specs=[x_spec, hist_spec, row(D_MODEL), row(D_MODEL),
                  _const_spec((D_MODEL, IN_WIDTH)), _const_spec((2 * BRANCH, D_MODEL)),
                  row(BRANCH), row(BRANCH), _const_spec((N_SLABS, CHUNK, CHUNK)), _const_spec((CHUNK, N_SLABS)),
                  _const_spec((CONV_D, BRANCH)), row(BRANCH), row(BRANCH), row(BRANCH)],
        out_specs=out_specs,
        out_shape=out_shape,
        scratch_shapes=[pltpu.VMEM((bb, N_SLABS, PAD_D + tt, LANES), jnp.float32)],
        compiler_params=_compiler_params(),
        name="odd_layer",
    )(x, hist_d, pre_g, post_g, w_in, w_out, ln_g, ln_b, w_s, b_s, dw_w, dw_b, cln_g, cln_b)


PROMPT_TILE = 512
SAMPLE_BATCH = 64


def kernel(x_prompt, x_sample, state_conv_a, state_pool_b, state_conv_d, norm_pre, norm_post, w_in_even, w_out_even, conv_a_w, pool_w, pool_scale, w_in_odd, w_out_odd, sgu_ln_g, sgu_ln_b, sgu_w, sgu_b, conf_dw_w, conf_dw_b, conf_ln_g, conf_ln_b):
    xp, xs = x_prompt, x_sample
    bp = xp.shape[0]
    dec_seq = xs.shape[1]
    depth = norm_pre.shape[0]
    bf16 = jnp.bfloat16
    row = lambda a: a.reshape(1, -1)
    zero_a = jnp.zeros((bp, CONV_A - 1, BRANCH), xp.dtype)
    zero_b = jnp.zeros((bp, POOL_HIST, BRANCH), xp.dtype)
    zero_d = jnp.zeros((bp, CONV_D - 1, BRANCH), xp.dtype)
    ca_p, ca_s, pb_p, pb_s, cd_p, cd_s, v_s = [], [], [], [], [], [], []
    for l in range(depth):
        i = l // 2
        if l % 2 == 0:
            args = (row(norm_pre[l]), row(norm_post[l]), w_in_even[i].astype(bf16), w_out_even[i].astype(bf16),
                    conv_a_w[i], pool_w[i].astype(bf16), row(pool_scale[i]))
            xp, a_p, b_p = _even_layer(xp, zero_a, zero_b, *args, bb=1, tt=PROMPT_TILE, start_pos=0)
            xs, a_s, b_s = _even_layer(xs, state_conv_a[i], state_pool_b[i], *args,
                                       bb=SAMPLE_BATCH, tt=dec_seq, start_pos=PAST_LEN)
            ca_p.append(a_p); ca_s.append(a_s); pb_p.append(b_p); pb_s.append(b_s)
        else:
            reps = CHUNK // dec_seq
            head = (row(norm_pre[l]), row(norm_post[l]), w_in_odd[i].astype(bf16), w_out_odd[i].astype(bf16),
                    row(sgu_ln_g[i]), row(sgu_ln_b[i]))
            tail = (conf_dw_w[i], row(conf_dw_b[i]), row(conf_ln_g[i]), row(conf_ln_b[i]))
            ws_s = jnp.tile(sgu_w[i][:, :dec_seq, :dec_seq], (1, reps, reps))
            bs_s = jnp.tile(sgu_b[i][:, :dec_seq], (1, reps))
            xp, d_p = _odd_layer(xp, zero_d, *head, sgu_w[i], sgu_b[i].T, *tail,
                                 bb=1, tt=PROMPT_TILE, mix_len=CHUNK, emit_vn=False)
            xs, d_s, vn_s = _odd_layer(xs, state_conv_d[i], *head, ws_s, bs_s.T, *tail,
                                       bb=SAMPLE_BATCH, tt=dec_seq, mix_len=dec_seq, emit_vn=True)
            cd_p.append(d_p); cd_s.append(d_s); v_s.append(vn_s)
    return (xp, xs, jnp.stack(ca_p), jnp.stack(ca_s), jnp.stack(pb_p), jnp.stack(pb_s),
            jnp.stack(cd_p), jnp.stack(cd_s), jnp.stack(v_s))
```

```python
import functools

import jax
import jax.numpy as jnp
from jax import lax
from jax.experimental import pallas as pl
from jax.experimental.pallas import tpu as pltpu

D_MODEL = 1024
BRANCH = 512
LANES = 128
N_SLABS = BRANCH // LANES
IN_WIDTH = 6 * BRANCH
CONV_A = 3
POOL_WINDOWS = (2, 4, 8, 16)
POOL_HIST = max(POOL_WINDOWS) - 1
CHUNK = 128
CONV_D = 31
PAST_LEN = 16384
EPS = 1e-6

PAD_A = 8
PAD_B = 16
PAD_D = 32

VMEM_LIMIT_BYTES = 56 * 1024 * 1024

PROMPT_TILE = 512
SAMPLE_BATCH = 64


def _rms_norm(x, g):
    ms = jnp.mean(x * x, axis=-1, keepdims=True)
    return x * lax.rsqrt(ms + EPS) * g


def _layer_norm(x, g, b):
    mu = jnp.mean(x, axis=-1, keepdims=True)
    xc = x - mu
    var = jnp.mean(xc * xc, axis=-1, keepdims=True)
    return xc * lax.rsqrt(var + EPS) * g + b


def _sigmoid(x):
    return 0.5 * jnp.tanh(0.5 * x) + 0.5


def _silu(x):
    return x * _sigmoid(x)


def _slab(c):
    return slice(c * LANES, (c + 1) * LANES)


def _carry_history(ext, hist_ref, pad, n_hist, tt, n_t):
    j = pl.program_id(1)

    @pl.when(j == 0)
    def _():
        for c in range(N_SLABS):
            if hist_ref is None:
                ext[:, c, pad - n_hist:pad, :] = jnp.zeros((ext.shape[0], n_hist, LANES), ext.dtype)
            else:
                ext[:, c, pad - n_hist:pad, :] = hist_ref[:, :, _slab(c)]

    if n_t > 1:
        @pl.when(j > 0)
        def _():
            for c in range(N_SLABS):
                ext[:, c, pad - n_hist:pad, :] = ext[:, c, pad + tt - n_hist:pad + tt, :]


def _emit_history(ext, new_ref, pad, n_hist, tt, n_t):
    @pl.when(pl.program_id(1) == n_t - 1)
    def _():
        for c in range(N_SLABS):
            new_ref[:, :, _slab(c)] = ext[:, c, pad + tt - n_hist:pad + tt, :]


def _store_slabs(ext, val, pad, bb, tt):
    for c in range(N_SLABS):
        ext[:, c, pad:pad + tt, :] = val[:, _slab(c)].reshape(bb, tt, LANES)


def _causal_conv(ext, w_ref, pad, n_taps, bb, tt):
    m = bb * tt
    outs = []
    for c in range(N_SLABS):
        acc = None
        for k in range(n_taps):
            lo = pad - (n_taps - 1) + k
            term = ext[:, c, lo:lo + tt, :] * w_ref[k:k + 1, _slab(c)]
            acc = term if acc is None else acc + term
        outs.append(acc.reshape(m, LANES))
    return jnp.concatenate(outs, axis=-1)


def _project_in(x_ref, pre_ref, win_ref, m):
    x = x_ref[...].reshape(m, D_MODEL)
    h = _rms_norm(x, pre_ref[...]).astype(jnp.bfloat16)
    return jnp.dot(h, win_ref[...], preferred_element_type=jnp.float32)


def _project_out(x_ref, y_ref, post_ref, wout_ref, y1, y2, bb, tt):
    m = bb * tt
    o = jnp.dot(y1.astype(jnp.bfloat16), wout_ref[:BRANCH, :], preferred_element_type=jnp.float32)
    o = o + jnp.dot(y2.astype(jnp.bfloat16), wout_ref[BRANCH:, :], preferred_element_type=jnp.float32)
    out = x_ref[...].reshape(m, D_MODEL) + _rms_norm(o, post_ref[...])
    y_ref[...] = out.reshape(bb, tt, D_MODEL)


def _even_kernel(*refs, bb, tt, n_t, start_pos, has_hist, n_alias):
    refs = list(refs)
    x_ref = refs.pop(0)
    ha_ref, hb_ref = (refs.pop(0), refs.pop(0)) if has_hist else (None, None)
    pre_ref, post_ref, win_ref, wout_ref, cw_ref, pw_ref, ps_ref = refs[:7]
    y_ref, na_ref, nb_ref, ext_a, ext_b = refs[7 + n_alias:]
    m = bb * tt
    _carry_history(ext_a, ha_ref, PAD_A, CONV_A - 1, tt, n_t)
    _carry_history(ext_b, hb_ref, PAD_B, POOL_HIST, tt, n_t)

    z = _project_in(x_ref, pre_ref, win_ref, m)
    zb, zc, zh, ga, p, gb = (z[:, i * BRANCH:(i + 1) * BRANCH] for i in range(6))

    _store_slabs(ext_a, zc * zh, PAD_A, bb, tt)
    ya = zb * _causal_conv(ext_a, cw_ref, PAD_A, CONV_A, bb, tt) * _silu(ga)

    _store_slabs(ext_b, p, PAD_B, bb, tt)
    pos = start_pos + pl.program_id(1) * tt + lax.broadcasted_iota(jnp.int32, (bb, tt, LANES), 1)
    mixed = []
    for g, w in enumerate(POOL_WINDOWS):
        acc = ext_b[:, g, PAD_B:PAD_B + tt, :]
        for i in range(1, w):
            acc = acc + ext_b[:, g, PAD_B - i:PAD_B - i + tt, :]
        cnt = jnp.minimum(pos + 1, w).astype(jnp.float32)
        d = (acc / cnt).reshape(m, LANES) - p[:, _slab(g)]
        mixed.append(jnp.dot(d.astype(jnp.bfloat16), pw_ref[g], preferred_element_type=jnp.float32))
    yb = jnp.concatenate(mixed, axis=-1) * ps_ref[...] * _silu(gb)

    _project_out(x_ref, y_ref, post_ref, wout_ref, ya, yb, bb, tt)
    _emit_history(ext_a, na_ref, PAD_A, CONV_A - 1, tt, n_t)
    _emit_history(ext_b, nb_ref, PAD_B, POOL_HIST, tt, n_t)


def _odd_kernel(*refs, bb, tt, n_t, mix_len, has_hist, emit_vn, n_alias):
    refs = list(refs)
    x_ref = refs.pop(0)
    hd_ref = refs.pop(0) if has_hist else None
    (pre_ref, post_ref, win_ref, wout_ref, lng_ref, lnb_ref, ws_ref, bs_ref,
     dw_ref, db_ref, cg_ref, cb_ref) = refs[:12]
    outs = refs[12 + n_alias:]
    if emit_vn:
        y_ref, nd_ref, vn_ref, ext_d = outs
    else:
        y_ref, nd_ref, ext_d = outs
    m = bb * tt
    _carry_history(ext_d, hd_ref, PAD_D, CONV_D - 1, tt, n_t)

    z = _project_in(x_ref, pre_ref, win_ref, m)
    u, v, gc, za, zb, gd = (z[:, i * BRANCH:(i + 1) * BRANCH] for i in range(6))

    vn = _layer_norm(v, lng_ref[...], lnb_ref[...])
    if emit_vn:
        vn_ref[...] = vn.reshape(bb, tt, BRANCH)
    vnb = vn.astype(jnp.bfloat16)
    row = lax.broadcasted_iota(jnp.int32, (CHUNK, CHUNK), 0)
    col = lax.broadcasted_iota(jnp.int32, (CHUNK, CHUNK), 1)
    keep = (row // mix_len == col // mix_len) & (col <= row)
    wm = [jnp.where(keep, ws_ref[g], 0.0).astype(jnp.bfloat16) for g in range(N_SLABS)]
    rows = []
    for ch in range(m // CHUNK):
        blk = vnb[ch * CHUNK:(ch + 1) * CHUNK, :]
        rows.append(jnp.concatenate(
            [jnp.dot(wm[g], blk[:, _slab(g)], preferred_element_type=jnp.float32) + bs_ref[:, g:g + 1]
             for g in range(N_SLABS)], axis=-1))
    yc = u * jnp.concatenate(rows, axis=0) * _silu(gc)

    _store_slabs(ext_d, za * _sigmoid(zb), PAD_D, bb, tt)
    y = _causal_conv(ext_d, dw_ref, PAD_D, CONV_D, bb, tt) + db_ref[...]
    yd = _silu(_layer_norm(y, cg_ref[...], cb_ref[...])) * _silu(gd)

    _project_out(x_ref, y_ref, post_ref, wout_ref, yc, yd, bb, tt)
    _emit_history(ext_d, nd_ref, PAD_D, CONV_D - 1, tt, n_t)


def _layer_spec(arr, idx):
    tail = arr.shape[1:]
    return pl.BlockSpec((None,) + tail, lambda b, j: (idx,) + (0,) * len(tail), pipeline_mode=pl.Buffered(1))


def _state_spec(n_hist, bb, idx):
    return pl.BlockSpec((None, bb, n_hist, BRANCH), lambda b, j: (idx, b, 0, 0))


def _compiler_params():
    return pltpu.CompilerParams(dimension_semantics=("arbitrary", "arbitrary"),
                                vmem_limit_bytes=VMEM_LIMIT_BYTES)


def _stacked_out(n_layers, bsz, rows, dtype):
    return jax.ShapeDtypeStruct((n_layers, bsz, rows, BRANCH), dtype)


def _even_layer(x, states, params, prev_new, *, layer, idx, n_layers, bb, tt, start_pos):
    bsz, t, _ = x.shape
    n_t = t // tt
    x_spec = pl.BlockSpec((bb, tt, D_MODEL), lambda b, j: (b, j, 0))
    pre_g, post_g, w_in, w_out, conv_w, pool_w, pool_scale = params
    ins, in_specs = [x], [x_spec]
    if states is not None:
        ins += list(states)
        in_specs += [_state_spec(CONV_A - 1, bb, idx), _state_spec(POOL_HIST, bb, idx)]
    ins += [pre_g, post_g, w_in, w_out, conv_w, pool_w, pool_scale]
    in_specs += [_layer_spec(pre_g, layer), _layer_spec(post_g, layer), _layer_spec(w_in, idx),
                 _layer_spec(w_out, idx), _layer_spec(conv_w, idx), _layer_spec(pool_w, idx),
                 _layer_spec(pool_scale, idx)]
    aliases = {}
    if prev_new is not None:
        for k, arr in enumerate(prev_new):
            aliases[len(ins)] = 1 + k
            ins.append(arr)
            in_specs.append(pl.BlockSpec(memory_space=pl.ANY))
    n_alias = len(aliases)
    return pl.pallas_call(
        functools.partial(_even_kernel, bb=bb, tt=tt, n_t=n_t, start_pos=start_pos,
                          has_hist=states is not None, n_alias=n_alias),
        grid=(bsz // bb, n_t),
        in_specs=in_specs,
        out_specs=[x_spec, _state_spec(CONV_A - 1, bb, idx), _state_spec(POOL_HIST, bb, idx)],
        out_shape=[jax.ShapeDtypeStruct(x.shape, x.dtype),
                   _stacked_out(n_layers, bsz, CONV_A - 1, x.dtype),
                   _stacked_out(n_layers, bsz, POOL_HIST, x.dtype)],
        scratch_shapes=[pltpu.VMEM((bb, N_SLABS, PAD_A + tt, LANES), jnp.float32),
                        pltpu.VMEM((bb, N_SLABS, PAD_B + tt, LANES), jnp.float32)],
        input_output_aliases=aliases,
        compiler_params=_compiler_params(),
        name="even_layer",
    )(*ins)


def _odd_layer(x, state, params, prev_new, *, layer, idx, n_layers, bb, tt, mix_len, emit_vn):
    bsz, t, _ = x.shape
    n_t = t // tt
    x_spec = pl.BlockSpec((bb, tt, D_MODEL), lambda b, j: (b, j, 0))
    ins, in_specs = [x], [x_spec]
    if state is not None:
        ins.append(state)
        in_specs.append(_state_spec(CONV_D - 1, bb, idx))
    for k, arr in enumerate(params):
        ins.append(arr)
        in_specs.append(_layer_spec(arr, layer if k < 2 else idx))
    out_specs = [x_spec, _state_spec(CONV_D - 1, bb, idx)]
    out_shape = [jax.ShapeDtypeStruct(x.shape, x.dtype), _stacked_out(n_layers, bsz, CONV_D - 1, x.dtype)]
    if emit_vn:
        out_specs.append(pl.BlockSpec((None, bb, tt, BRANCH), lambda b, j: (idx, b, j, 0)))
        out_shape.append(_stacked_out(n_layers, bsz, t, x.dtype))
    aliases = {}
    if prev_new is not None:
        for k, arr in enumerate(prev_new):
            aliases[len(ins)] = 1 + k
            ins.append(arr)
            in_specs.append(pl.BlockSpec(memory_space=pl.ANY))
    return pl.pallas_call(
        functools.partial(_odd_kernel, bb=bb, tt=tt, n_t=n_t, mix_len=mix_len,
                          has_hist=state is not None, emit_vn=emit_vn, n_alias=len(aliases)),
        grid=(bsz // bb, n_t),
        in_specs=in_specs,
        out_specs=out_specs,
        out_shape=out_shape,
        scratch_shapes=[pltpu.VMEM((bb, N_SLABS, PAD_D + tt, LANES), jnp.float32)],
        input_output_aliases=aliases,
        compiler_params=_compiler_params(),
        name="odd_layer",
    )(*ins)


def kernel(x_prompt, x_sample, state_conv_a, state_pool_b, state_conv_d, norm_pre, norm_post, w_in_even, w_out_even, conv_a_w, pool_w, pool_scale, w_in_odd, w_out_odd, sgu_ln_g, sgu_ln_b, sgu_w, sgu_b, conf_dw_w, conf_dw_b, conf_ln_g, conf_ln_b):
    xp, xs = x_prompt, x_sample
    dec_seq = xs.shape[1]
    depth = norm_pre.shape[0]
    n_even, n_odd = w_in_even.shape[0], w_in_odd.shape[0]
    bf16 = jnp.bfloat16
    rows = lambda a: a.reshape(a.shape[0], 1, a.shape[1])
    reps = CHUNK // dec_seq

    even_params = (rows(norm_pre), rows(norm_post), w_in_even.astype(bf16), w_out_even.astype(bf16),
                   conv_a_w, pool_w.astype(bf16), rows(pool_scale))
    odd_head = (rows(norm_pre), rows(norm_post), w_in_odd.astype(bf16), w_out_odd.astype(bf16),
                rows(sgu_ln_g), rows(sgu_ln_b))
    odd_tail = (conf_dw_w, rows(conf_dw_b), rows(conf_ln_g), rows(conf_ln_b))
    odd_params_p = odd_head + (sgu_w, jnp.swapaxes(sgu_b, 1, 2)) + odd_tail
    ws_s = jnp.tile(sgu_w[:, :, :dec_seq, :dec_seq], (1, 1, reps, reps))
    bs_s = jnp.swapaxes(jnp.tile(sgu_b[:, :, :dec_seq], (1, 1, reps)), 1, 2)
    odd_params_s = odd_head + (ws_s, bs_s) + odd_tail

    prompt = dict(bb=1, tt=PROMPT_TILE)
    sample = dict(bb=SAMPLE_BATCH, tt=dec_seq)
    new_even_p = new_even_s = new_odd_p = new_odd_s = None
    for l in range(depth):
        i = l // 2
        if l % 2 == 0:
            common = dict(layer=l, idx=i, n_layers=n_even)
            xp, *new_even_p = _even_layer(xp, None, even_params, new_even_p, start_pos=0, **common, **prompt)
            xs, *new_even_s = _even_layer(xs, (state_conv_a, state_pool_b), even_params, new_even_s,
                                          start_pos=PAST_LEN, **common, **sample)
        else:
            common = dict(layer=l, idx=i, n_layers=n_odd)
            xp, *new_odd_p = _odd_layer(xp, None, odd_params_p, new_odd_p, mix_len=CHUNK, emit_vn=False,
                                        **common, **prompt)
            xs, *new_odd_s = _odd_layer(xs, state_conv_d, odd_params_s, new_odd_s, mix_len=dec_seq, emit_vn=True,
                                        **common, **sample)
    ca_p, pb_p = new_even_p
    ca_s, pb_s = new_even_s
    (cd_p,) = new_odd_p
    cd_s, v_s = new_odd_s
    return (xp, xs, ca_p, ca_s, pb_p, pb_s, cd_p, cd_s, v_s)
```

```python
import functools

import jax
import jax.numpy as jnp
from jax import lax
from jax.experimental import pallas as pl
from jax.experimental.pallas import tpu as pltpu

D_MODEL = 1024
BRANCH = 512
LANES = 128
N_SLABS = BRANCH // LANES
IN_WIDTH = 6 * BRANCH
CONV_A = 3
POOL_WINDOWS = (2, 4, 8, 16)
POOL_HIST = max(POOL_WINDOWS) - 1
CHUNK = 128
CONV_D = 31
PAST_LEN = 16384
EPS = 1e-6

PAD_A = 8
PAD_B = 16
PAD_D = 32

VMEM_LIMIT_BYTES = 56 * 1024 * 1024

PROMPT_TILE = 512
SAMPLE_BATCH = 64


def _rms_norm(x, g):
    ms = jnp.mean(x * x, axis=-1, keepdims=True)
    return x * lax.rsqrt(ms + EPS) * g


def _layer_norm(x, g, b):
    mu = jnp.mean(x, axis=-1, keepdims=True)
    xc = x - mu
    var = jnp.mean(xc * xc, axis=-1, keepdims=True)
    return xc * lax.rsqrt(var + EPS) * g + b


def _sigmoid(x):
    return 0.5 * jnp.tanh(0.5 * x) + 0.5


def _silu(x):
    return x * _sigmoid(x)


def _slab(c):
    return slice(c * LANES, (c + 1) * LANES)


def _carry_history(ext, hist_ref, pad, n_hist, tt, n_t):
    j = pl.program_id(1)

    @pl.when(j == 0)
    def _():
        for c in range(N_SLABS):
            if hist_ref is None:
                ext[:, c, pad - n_hist:pad, :] = jnp.zeros((ext.shape[0], n_hist, LANES), ext.dtype)
            else:
                ext[:, c, pad - n_hist:pad, :] = hist_ref[:, :, _slab(c)]

    if n_t > 1:
        @pl.when(j > 0)
        def _():
            for c in range(N_SLABS):
                ext[:, c, pad - n_hist:pad, :] = ext[:, c, pad + tt - n_hist:pad + tt, :]


def _emit_history(ext, new_ref, pad, n_hist, tt, n_t):
    @pl.when(pl.program_id(1) == n_t - 1)
    def _():
        for c in range(N_SLABS):
            new_ref[:, :, _slab(c)] = ext[:, c, pad + tt - n_hist:pad + tt, :]


def _store_slabs(ext, val, pad, bb, tt):
    for c in range(N_SLABS):
        ext[:, c, pad:pad + tt, :] = val[:, _slab(c)].reshape(bb, tt, LANES)


def _causal_conv(ext, w_ref, pad, n_taps, bb, tt):
    m = bb * tt
    outs = []
    for c in range(N_SLABS):
        acc = None
        for k in range(n_taps):
            lo = pad - (n_taps - 1) + k
            term = ext[:, c, lo:lo + tt, :] * w_ref[k:k + 1, _slab(c)]
            acc = term if acc is None else acc + term
        outs.append(acc.reshape(m, LANES))
    return jnp.concatenate(outs, axis=-1)


def _project_in(x_ref, pre_ref, win_ref, m):
    x = x_ref[...].reshape(m, D_MODEL)
    h = _rms_norm(x, pre_ref[...]).astype(jnp.bfloat16)
    return jnp.dot(h, win_ref[...], preferred_element_type=jnp.float32)


def _project_out(x_ref, y_ref, post_ref, wout_ref, y1, y2, bb, tt):
    m = bb * tt
    o = jnp.dot(y1.astype(jnp.bfloat16), wout_ref[:BRANCH, :], preferred_element_type=jnp.float32)
    o = o + jnp.dot(y2.astype(jnp.bfloat16), wout_ref[BRANCH:, :], preferred_element_type=jnp.float32)
    out = x_ref[...].reshape(m, D_MODEL) + _rms_norm(o, post_ref[...])
    y_ref[...] = out.reshape(bb, tt, D_MODEL)


def _even_kernel(*refs, bb, tt, n_t, start_pos, has_hist, n_alias):
    refs = list(refs)
    x_ref = refs.pop(0)
    ha_ref, hb_ref = (refs.pop(0), refs.pop(0)) if has_hist else (None, None)
    pre_ref, post_ref, win_ref, wout_ref, cw_ref, pw_ref, ps_ref = refs[:7]
    y_ref, na_ref, nb_ref, ext_a, ext_b = refs[7 + n_alias:]
    m = bb * tt
    _carry_history(ext_a, ha_ref, PAD_A, CONV_A - 1, tt, n_t)
    _carry_history(ext_b, hb_ref, PAD_B, POOL_HIST, tt, n_t)

    z = _project_in(x_ref, pre_ref, win_ref, m)
    zb, zc, zh, ga, p, gb = (z[:, i * BRANCH:(i + 1) * BRANCH] for i in range(6))

    _store_slabs(ext_a, zc * zh, PAD_A, bb, tt)
    ya = zb * _causal_conv(ext_a, cw_ref, PAD_A, CONV_A, bb, tt) * _silu(ga)

    _store_slabs(ext_b, p, PAD_B, bb, tt)
    pos = start_pos + pl.program_id(1) * tt + lax.broadcasted_iota(jnp.int32, (bb, tt, LANES), 1)
    mixed = []
    for g, w in enumerate(POOL_WINDOWS):
        acc = ext_b[:, g, PAD_B:PAD_B + tt, :]
        for i in range(1, w):
            acc = acc + ext_b[:, g, PAD_B - i:PAD_B - i + tt, :]
        cnt = jnp.minimum(pos + 1, w).astype(jnp.float32)
        d = (acc / cnt).reshape(m, LANES) - p[:, _slab(g)]
        mixed.append(jnp.dot(d.astype(jnp.bfloat16), pw_ref[g], preferred_element_type=jnp.float32))
    yb = jnp.concatenate(mixed, axis=-1) * ps_ref[...] * _silu(gb)

    _project_out(x_ref, y_ref, post_ref, wout_ref, ya, yb, bb, tt)
    _emit_history(ext_a, na_ref, PAD_A, CONV_A - 1, tt, n_t)
    _emit_history(ext_b, nb_ref, PAD_B, POOL_HIST, tt, n_t)


def _odd_kernel(*refs, bb, tt, n_t, mix_len, has_hist, emit_vn, n_alias):
    refs = list(refs)
    x_ref = refs.pop(0)
    hd_ref = refs.pop(0) if has_hist else None
    (pre_ref, post_ref, win_ref, wout_ref, lng_ref, lnb_ref, ws_ref, bs_ref,
     dw_ref, db_ref, cg_ref, cb_ref) = refs[:12]
    outs = refs[12 + n_alias:]
    if emit_vn:
        y_ref, nd_ref, vn_ref, ext_d, h_ref = outs
    else:
        y_ref, nd_ref, ext_d, h_ref = outs
    m = bb * tt
    _carry_history(ext_d, hd_ref, PAD_D, CONV_D - 1, tt, n_t)

    x = x_ref[...].reshape(m, D_MODEL)
    h_ref[...] = _rms_norm(x, pre_ref[...]).astype(jnp.bfloat16)

    conv_out = []
    for c in range(N_SLABS):
        w_c = jnp.concatenate([win_ref[:, 3 * BRANCH + c * LANES:3 * BRANCH + (c + 1) * LANES],
                               win_ref[:, 4 * BRANCH + c * LANES:4 * BRANCH + (c + 1) * LANES]], axis=1)
        zc = jnp.dot(h_ref[...], w_c, preferred_element_type=jnp.float32)
        glu = zc[:, :LANES] * _sigmoid(zc[:, LANES:])
        ext_d[:, c, PAD_D:PAD_D + tt, :] = glu.reshape(bb, tt, LANES)
        acc = None
        for k in range(CONV_D):
            lo = PAD_D - (CONV_D - 1) + k
            term = ext_d[:, c, lo:lo + tt, :] * dw_ref[k:k + 1, _slab(c)]
            acc = term if acc is None else acc + term
        conv_out.append(acc.reshape(m, LANES))

    z3 = jnp.dot(h_ref[...], win_ref[:, :3 * BRANCH], preferred_element_type=jnp.float32)
    u, v, gc = (z3[:, i * BRANCH:(i + 1) * BRANCH] for i in range(3))
    gd = jnp.dot(h_ref[...], win_ref[:, 5 * BRANCH:], preferred_element_type=jnp.float32)

    vn = _layer_norm(v, lng_ref[...], lnb_ref[...])
    if emit_vn:
        vn_ref[...] = vn.reshape(bb, tt, BRANCH)
    vnb = vn.astype(jnp.bfloat16)
    row = lax.broadcasted_iota(jnp.int32, (CHUNK, CHUNK), 0)
    col = lax.broadcasted_iota(jnp.int32, (CHUNK, CHUNK), 1)
    keep = (row // mix_len == col // mix_len) & (col <= row)
    wm = [jnp.where(keep, ws_ref[g], 0.0).astype(jnp.bfloat16) for g in range(N_SLABS)]
    rows = []
    for ch in range(m // CHUNK):
        blk = vnb[ch * CHUNK:(ch + 1) * CHUNK, :]
        rows.append(jnp.concatenate(
            [jnp.dot(wm[g], blk[:, _slab(g)], preferred_element_type=jnp.float32) + bs_ref[:, g:g + 1]
             for g in range(N_SLABS)], axis=-1))
    yc = u * jnp.concatenate(rows, axis=0) * _silu(gc)

    y = jnp.concatenate(conv_out, axis=-1) + db_ref[...]
    yd = _silu(_layer_norm(y, cg_ref[...], cb_ref[...])) * _silu(gd)

    _project_out(x_ref, y_ref, post_ref, wout_ref, yc, yd, bb, tt)
    _emit_history(ext_d, nd_ref, PAD_D, CONV_D - 1, tt, n_t)


def _layer_spec(arr, idx):
    tail = arr.shape[1:]
    return pl.BlockSpec((None,) + tail, lambda b, j: (idx,) + (0,) * len(tail), pipeline_mode=pl.Buffered(1))


def _state_spec(n_hist, bb, idx):
    return pl.BlockSpec((None, bb, n_hist, BRANCH), lambda b, j: (idx, b, 0, 0))


def _compiler_params():
    return pltpu.CompilerParams(dimension_semantics=("arbitrary", "arbitrary"),
                                vmem_limit_bytes=VMEM_LIMIT_BYTES)


def _stacked_out(n_layers, bsz, rows, dtype):
    return jax.ShapeDtypeStruct((n_layers, bsz, rows, BRANCH), dtype)


def _even_layer(x, states, params, prev_new, *, layer, idx, n_layers, bb, tt, start_pos):
    bsz, t, _ = x.shape
    n_t = t // tt
    x_spec = pl.BlockSpec((bb, tt, D_MODEL), lambda b, j: (b, j, 0))
    pre_g, post_g, w_in, w_out, conv_w, pool_w, pool_scale = params
    ins, in_specs = [x], [x_spec]
    if states is not None:
        ins += list(states)
        in_specs += [_state_spec(CONV_A - 1, bb, idx), _state_spec(POOL_HIST, bb, idx)]
    ins += [pre_g, post_g, w_in, w_out, conv_w, pool_w, pool_scale]
    in_specs += [_layer_spec(pre_g, layer), _layer_spec(post_g, layer), _layer_spec(w_in, idx),
                 _layer_spec(w_out, idx), _layer_spec(conv_w, idx), _layer_spec(pool_w, idx),
                 _layer_spec(pool_scale, idx)]
    aliases = {}
    if prev_new is not None:
        for k, arr in enumerate(prev_new):
            aliases[len(ins)] = 1 + k
            ins.append(arr)
            in_specs.append(pl.BlockSpec(memory_space=pl.ANY))
    n_alias = len(aliases)
    return pl.pallas_call(
        functools.partial(_even_kernel, bb=bb, tt=tt, n_t=n_t, start_pos=start_pos,
                          has_hist=states is not None, n_alias=n_alias),
        grid=(bsz // bb, n_t),
        in_specs=in_specs,
        out_specs=[x_spec, _state_spec(CONV_A - 1, bb, idx), _state_spec(POOL_HIST, bb, idx)],
        out_shape=[jax.ShapeDtypeStruct(x.shape, x.dtype),
                   _stacked_out(n_layers, bsz, CONV_A - 1, x.dtype),
                   _stacked_out(n_layers, bsz, POOL_HIST, x.dtype)],
        scratch_shapes=[pltpu.VMEM((bb, N_SLABS, PAD_A + tt, LANES), jnp.float32),
                        pltpu.VMEM((bb, N_SLABS, PAD_B + tt, LANES), jnp.float32)],
        input_output_aliases=aliases,
        compiler_params=_compiler_params(),
        name="even_layer",
    )(*ins)


def _odd_layer(x, state, params, prev_new, *, layer, idx, n_layers, bb, tt, mix_len, emit_vn):
    bsz, t, _ = x.shape
    n_t = t // tt
    x_spec = pl.BlockSpec((bb, tt, D_MODEL), lambda b, j: (b, j, 0))
    ins, in_specs = [x], [x_spec]
    if state is not None:
        ins.append(state)
        in_specs.append(_state_spec(CONV_D - 1, bb, idx))
    for k, arr in enumerate(params):
        ins.append(arr)
        in_specs.append(_layer_spec(arr, layer if k < 2 else idx))
    out_specs = [x_spec, _state_spec(CONV_D - 1, bb, idx)]
    out_shape = [jax.ShapeDtypeStruct(x.shape, x.dtype), _stacked_out(n_layers, bsz, CONV_D - 1, x.dtype)]
    if emit_vn:
        out_specs.append(pl.BlockSpec((None, bb, tt, BRANCH), lambda b, j: (idx, b, j, 0)))
        out_shape.append(_stacked_out(n_layers, bsz, t, x.dtype))
    aliases = {}
    if prev_new is not None:
        for k, arr in enumerate(prev_new):
            aliases[len(ins)] = 1 + k
            ins.append(arr)
            in_specs.append(pl.BlockSpec(memory_space=pl.ANY))
    return pl.pallas_call(
        functools.partial(_odd_kernel, bb=bb, tt=tt, n_t=n_t, mix_len=mix_len,
                          has_hist=state is not None, emit_vn=emit_vn, n_alias=len(aliases)),
        grid=(bsz // bb, n_t),
        in_specs=in_specs,
        out_specs=out_specs,
        out_shape=out_shape,
        scratch_shapes=[pltpu.VMEM((bb, N_SLABS, PAD_D + tt, LANES), jnp.float32),
                        pltpu.VMEM((bb * tt, D_MODEL), jnp.bfloat16)],
        input_output_aliases=aliases,
        compiler_params=_compiler_params(),
        name="odd_layer",
    )(*ins)


def kernel(x_prompt, x_sample, state_conv_a, state_pool_b, state_conv_d, norm_pre, norm_post, w_in_even, w_out_even, conv_a_w, pool_w, pool_scale, w_in_odd, w_out_odd, sgu_ln_g, sgu_ln_b, sgu_w, sgu_b, conf_dw_w, conf_dw_b, conf_ln_g, conf_ln_b):
    xp, xs = x_prompt, x_sample
    dec_seq = xs.shape[1]
    depth = norm_pre.shape[0]
    n_even, n_odd = w_in_even.shape[0], w_in_odd.shape[0]
    bf16 = jnp.bfloat16
    rows = lambda a: a.reshape(a.shape[0], 1, a.shape[1])
    reps = CHUNK // dec_seq

    even_params = (rows(norm_pre), rows(norm_post), w_in_even.astype(bf16), w_out_even.astype(bf16),
                   conv_a_w, pool_w.astype(bf16), rows(pool_scale))
    odd_head = (rows(norm_pre), rows(norm_post), w_in_odd.astype(bf16), w_out_odd.astype(bf16),
                rows(sgu_ln_g), rows(sgu_ln_b))
    odd_tail = (conf_dw_w, rows(conf_dw_b), rows(conf_ln_g), rows(conf_ln_b))
    odd_params_p = odd_head + (sgu_w, jnp.swapaxes(sgu_b, 1, 2)) + odd_tail
    ws_s = jnp.tile(sgu_w[:, :, :dec_seq, :dec_seq], (1, 1, reps, reps))
    bs_s = jnp.swapaxes(jnp.tile(sgu_b[:, :, :dec_seq], (1, 1, reps)), 1, 2)
    odd_params_s = odd_head + (ws_s, bs_s) + odd_tail

    prompt = dict(bb=1, tt=PROMPT_TILE)
    sample = dict(bb=SAMPLE_BATCH, tt=dec_seq)
    new_even_p = new_even_s = new_odd_p = new_odd_s = None
    for l in range(depth):
        i = l // 2
        if l % 2 == 0:
            common = dict(layer=l, idx=i, n_layers=n_even)
            xp, *new_even_p = _even_layer(xp, None, even_params, new_even_p, start_pos=0, **common, **prompt)
            xs, *new_even_s = _even_layer(xs, (state_conv_a, state_pool_b), even_params, new_even_s,
                                          start_pos=PAST_LEN, **common, **sample)
        else:
            common = dict(layer=l, idx=i, n_layers=n_odd)
            xp, *new_odd_p = _odd_layer(xp, None, odd_params_p, new_odd_p, mix_len=CHUNK, emit_vn=False,
                                        **common, **prompt)
            xs, *new_odd_s = _odd_layer(xs, state_conv_d, odd_params_s, new_odd_s, mix_len=dec_seq, emit_vn=True,
                                        **common, **sample)
    ca_p, pb_p = new_even_p
    ca_s, pb_s = new_even_s
    (cd_p,) = new_odd_p
    cd_s, v_s = new_odd_s
    return (xp, xs, ca_p, ca_s, pb_p, pb_s, cd_p, cd_s, v_s)
```

```python
import functools

import jax
import jax.numpy as jnp
from jax import lax
from jax.experimental import pallas as pl
from jax.experimental.pallas import tpu as pltpu

D_MODEL = 1024
BRANCH = 512
LANES = 128
N_SLABS = BRANCH // LANES
IN_WIDTH = 6 * BRANCH
CONV_A = 3
POOL_WINDOWS = (2, 4, 8, 16)
POOL_HIST = max(POOL_WINDOWS) - 1
CHUNK = 128
CONV_D = 31
PAST_LEN = 16384
EPS = 1e-6

PAD_A = 8
PAD_B = 16
PAD_D = 32

VMEM_LIMIT_BYTES = 56 * 1024 * 1024

EVEN_ROWS = 1024
ODD_ROWS = 512


def _rms_norm(x, g):
    ms = jnp.mean(x * x, axis=-1, keepdims=True)
    return x * lax.rsqrt(ms + EPS) * g


def _layer_norm(x, g, b):
    mu = jnp.mean(x, axis=-1, keepdims=True)
    xc = x - mu
    var = jnp.mean(xc * xc, axis=-1, keepdims=True)
    return xc * lax.rsqrt(var + EPS) * g + b


def _sigmoid(x):
    return 0.5 * jnp.tanh(0.5 * x) + 0.5


def _silu(x):
    return x * _sigmoid(x)


def _slab(c):
    return slice(c * LANES, (c + 1) * LANES)


def _carry_history(ext, hist_ref, pad, n_hist, tt, n_t):
    j = pl.program_id(1)

    @pl.when(j == 0)
    def _():
        for c in range(N_SLABS):
            if hist_ref is None:
                ext[:, c, pad - n_hist:pad, :] = jnp.zeros((ext.shape[0], n_hist, LANES), ext.dtype)
            else:
                ext[:, c, pad - n_hist:pad, :] = hist_ref[:, :, _slab(c)]

    if n_t > 1:
        @pl.when(j > 0)
        def _():
            for c in range(N_SLABS):
                ext[:, c, pad - n_hist:pad, :] = ext[:, c, pad + tt - n_hist:pad + tt, :]


def _emit_history(ext, new_ref, pad, n_hist, tt, n_t, fill):
    @pl.when(pl.program_id(1) == n_t - 1)
    def _():
        dst = new_ref if fill is None else new_ref.at[fill[0]]
        for c in range(N_SLABS):
            dst[:, :, _slab(c)] = ext[:, c, pad + tt - n_hist:pad + tt, :]
        _zero_other_layers(new_ref, fill)


def _zero_other_layers(ref, fill):
    if fill is not None:
        for o in range(fill[1]):
            if o != fill[0]:
                ref[o] = jnp.zeros(ref.shape[1:], ref.dtype)


def _store_slabs(ext, val, pad, bb, tt):
    for c in range(N_SLABS):
        ext[:, c, pad:pad + tt, :] = val[:, _slab(c)].reshape(bb, tt, LANES)


def _causal_conv(ext, w_ref, pad, n_taps, bb, tt):
    m = bb * tt
    outs = []
    for c in range(N_SLABS):
        acc = None
        for k in range(n_taps):
            lo = pad - (n_taps - 1) + k
            term = ext[:, c, lo:lo + tt, :] * w_ref[k:k + 1, _slab(c)]
            acc = term if acc is None else acc + term
        outs.append(acc.reshape(m, LANES))
    return jnp.concatenate(outs, axis=-1)


def _project_in(x_ref, pre_ref, win_ref, m):
    x = x_ref[...].reshape(m, D_MODEL)
    h = _rms_norm(x, pre_ref[...]).astype(jnp.bfloat16)
    return jnp.dot(h, win_ref[...], preferred_element_type=jnp.float32)


def _project_out(x_ref, y_ref, post_ref, wout_ref, y1, y2, bb, tt):
    m = bb * tt
    o = jnp.dot(y1.astype(jnp.bfloat16), wout_ref[:BRANCH, :], preferred_element_type=jnp.float32)
    o = o + jnp.dot(y2.astype(jnp.bfloat16), wout_ref[BRANCH:, :], preferred_element_type=jnp.float32)
    out = x_ref[...].reshape(m, D_MODEL) + _rms_norm(o, post_ref[...])
    y_ref[...] = out.reshape(bb, tt, D_MODEL)


def _even_kernel(*refs, bb, tt, n_t, start_pos, has_hist, n_alias, fill):
    refs = list(refs)
    x_ref = refs.pop(0)
    ha_ref, hb_ref = (refs.pop(0), refs.pop(0)) if has_hist else (None, None)
    pre_ref, post_ref, win_ref, wout_ref, cw_ref, pw_ref, ps_ref = refs[:7]
    y_ref, na_ref, nb_ref, ext_a, ext_b = refs[7 + n_alias:]
    m = bb * tt
    _carry_history(ext_a, ha_ref, PAD_A, CONV_A - 1, tt, n_t)
    _carry_history(ext_b, hb_ref, PAD_B, POOL_HIST, tt, n_t)

    z = _project_in(x_ref, pre_ref, win_ref, m)
    zb, zc, zh, ga, p, gb = (z[:, i * BRANCH:(i + 1) * BRANCH] for i in range(6))

    _store_slabs(ext_a, zc * zh, PAD_A, bb, tt)
    ya = zb * _causal_conv(ext_a, cw_ref, PAD_A, CONV_A, bb, tt) * _silu(ga)

    _store_slabs(ext_b, p, PAD_B, bb, tt)
    pos = start_pos + pl.program_id(1) * tt + lax.broadcasted_iota(jnp.int32, (bb, tt, LANES), 1)
    mixed = []
    for g, w in enumerate(POOL_WINDOWS):
        acc = ext_b[:, g, PAD_B:PAD_B + tt, :]
        for i in range(1, w):
            acc = acc + ext_b[:, g, PAD_B - i:PAD_B - i + tt, :]
        cnt = jnp.minimum(pos + 1, w).astype(jnp.float32)
        d = (acc / cnt).reshape(m, LANES) - p[:, _slab(g)]
        mixed.append(jnp.dot(d.astype(jnp.bfloat16), pw_ref[g], preferred_element_type=jnp.float32))
    yb = jnp.concatenate(mixed, axis=-1) * ps_ref[...] * _silu(gb)

    _project_out(x_ref, y_ref, post_ref, wout_ref, ya, yb, bb, tt)
    _emit_history(ext_a, na_ref, PAD_A, CONV_A - 1, tt, n_t, fill)
    _emit_history(ext_b, nb_ref, PAD_B, POOL_HIST, tt, n_t, fill)


def _odd_kernel(*refs, bb, tt, n_t, mix_len, has_hist, emit_vn, n_alias, fill):
    refs = list(refs)
    x_ref = refs.pop(0)
    hd_ref = refs.pop(0) if has_hist else None
    (pre_ref, post_ref, win_ref, wout_ref, lng_ref, lnb_ref, ws_ref, bs_ref,
     dw_ref, db_ref, cg_ref, cb_ref) = refs[:12]
    outs = refs[12 + n_alias:]
    if emit_vn:
        y_ref, nd_ref, vn_ref, ext_d, h_ref = outs
    else:
        y_ref, nd_ref, ext_d, h_ref = outs
    m = bb * tt
    _carry_history(ext_d, hd_ref, PAD_D, CONV_D - 1, tt, n_t)

    x = x_ref[...].reshape(m, D_MODEL)
    h_ref[...] = _rms_norm(x, pre_ref[...]).astype(jnp.bfloat16)

    conv_out = []
    for c in range(N_SLABS):
        w_c = jnp.concatenate([win_ref[:, 3 * BRANCH + c * LANES:3 * BRANCH + (c + 1) * LANES],
                               win_ref[:, 4 * BRANCH + c * LANES:4 * BRANCH + (c + 1) * LANES]], axis=1)
        zc = jnp.dot(h_ref[...], w_c, preferred_element_type=jnp.float32)
        glu = zc[:, :LANES] * _sigmoid(zc[:, LANES:])
        ext_d[:, c, PAD_D:PAD_D + tt, :] = glu.reshape(bb, tt, LANES)
        acc = None
        for k in range(CONV_D):
            lo = PAD_D - (CONV_D - 1) + k
            term = ext_d[:, c, lo:lo + tt, :] * dw_ref[k:k + 1, _slab(c)]
            acc = term if acc is None else acc + term
        conv_out.append(acc.reshape(m, LANES))

    z3 = jnp.dot(h_ref[...], win_ref[:, :3 * BRANCH], preferred_element_type=jnp.float32)
    u, v, gc = (z3[:, i * BRANCH:(i + 1) * BRANCH] for i in range(3))
    gd = jnp.dot(h_ref[...], win_ref[:, 5 * BRANCH:], preferred_element_type=jnp.float32)

    vn = _layer_norm(v, lng_ref[...], lnb_ref[...])
    if emit_vn:
        (vn_ref if fill is None else vn_ref.at[fill[0]])[...] = vn.reshape(bb, tt, BRANCH)
        _zero_other_layers(vn_ref, fill)
    vnb = vn.astype(jnp.bfloat16)
    row = lax.broadcasted_iota(jnp.int32, (CHUNK, CHUNK), 0)
    col = lax.broadcasted_iota(jnp.int32, (CHUNK, CHUNK), 1)
    keep = (row // mix_len == col // mix_len) & (col <= row)
    wm = [jnp.where(keep, ws_ref[g], 0.0).astype(jnp.bfloat16) for g in range(N_SLABS)]
    rows = []
    for ch in range(m // CHUNK):
        blk = vnb[ch * CHUNK:(ch + 1) * CHUNK, :]
        rows.append(jnp.concatenate(
            [jnp.dot(wm[g], blk[:, _slab(g)], preferred_element_type=jnp.float32) + bs_ref[:, g:g + 1]
             for g in range(N_SLABS)], axis=-1))
    yc = u * jnp.concatenate(rows, axis=0) * _silu(gc)

    y = jnp.concatenate(conv_out, axis=-1) + db_ref[...]
    yd = _silu(_layer_norm(y, cg_ref[...], cb_ref[...])) * _silu(gd)

    _project_out(x_ref, y_ref, post_ref, wout_ref, yc, yd, bb, tt)
    _emit_history(ext_d, nd_ref, PAD_D, CONV_D - 1, tt, n_t, fill)


def _layer_spec(arr, idx):
    tail = arr.shape[1:]
    return pl.BlockSpec((None,) + tail, lambda b, j: (idx,) + (0,) * len(tail), pipeline_mode=pl.Buffered(1))


def _state_spec(n_hist, bb, idx):
    return pl.BlockSpec((None, bb, n_hist, BRANCH), lambda b, j: (idx, b, 0, 0))


def _new_state_spec(n_hist, bb, idx, n_layers, first):
    if first:
        return pl.BlockSpec((n_layers, bb, n_hist, BRANCH), lambda b, j: (0, b, 0, 0))
    return _state_spec(n_hist, bb, idx)


def _compiler_params():
    return pltpu.CompilerParams(dimension_semantics=("arbitrary", "arbitrary"),
                                vmem_limit_bytes=VMEM_LIMIT_BYTES)


def _stacked_out(n_layers, bsz, rows, dtype):
    return jax.ShapeDtypeStruct((n_layers, bsz, rows, BRANCH), dtype)


def _even_layer(x, states, params, prev_new, *, layer, idx, n_layers, bb, tt, start_pos):
    bsz, t, _ = x.shape
    n_t = t // tt
    x_spec = pl.BlockSpec((bb, tt, D_MODEL), lambda b, j: (b, j, 0))
    pre_g, post_g, w_in, w_out, conv_w, pool_w, pool_scale = params
    ins, in_specs = [x], [x_spec]
    if states is not None:
        ins += list(states)
        in_specs += [_state_spec(CONV_A - 1, bb, idx), _state_spec(POOL_HIST, bb, idx)]
    ins += [pre_g, post_g, w_in, w_out, conv_w, pool_w, pool_scale]
    in_specs += [_layer_spec(pre_g, layer), _layer_spec(post_g, layer), _layer_spec(w_in, idx),
                 _layer_spec(w_out, idx), _layer_spec(conv_w, idx), _layer_spec(pool_w, idx),
                 _layer_spec(pool_scale, idx)]
    aliases = {}
    if prev_new is not None:
        for k, arr in enumerate(prev_new):
            aliases[len(ins)] = 1 + k
            ins.append(arr)
            in_specs.append(pl.BlockSpec(memory_space=pl.ANY))
    n_alias = len(aliases)
    first = prev_new is None
    return pl.pallas_call(
        functools.partial(_even_kernel, bb=bb, tt=tt, n_t=n_t, start_pos=start_pos,
                          has_hist=states is not None, n_alias=n_alias, fill=(idx, n_layers) if first else None),
        grid=(bsz // bb, n_t),
        in_specs=in_specs,
        out_specs=[x_spec, _new_state_spec(CONV_A - 1, bb, idx, n_layers, first),
                   _new_state_spec(POOL_HIST, bb, idx, n_layers, first)],
        out_shape=[jax.ShapeDtypeStruct(x.shape, x.dtype),
                   _stacked_out(n_layers, bsz, CONV_A - 1, x.dtype),
                   _stacked_out(n_layers, bsz, POOL_HIST, x.dtype)],
        scratch_shapes=[pltpu.VMEM((bb, N_SLABS, PAD_A + tt, LANES), jnp.float32),
                        pltpu.VMEM((bb, N_SLABS, PAD_B + tt, LANES), jnp.float32)],
        input_output_aliases=aliases,
        compiler_params=_compiler_params(),
        name="even_layer",
    )(*ins)


def _odd_layer(x, state, params, prev_new, *, layer, idx, n_layers, bb, tt, mix_len, emit_vn):
    bsz, t, _ = x.shape
    n_t = t // tt
    x_spec = pl.BlockSpec((bb, tt, D_MODEL), lambda b, j: (b, j, 0))
    ins, in_specs = [x], [x_spec]
    if state is not None:
        ins.append(state)
        in_specs.append(_state_spec(CONV_D - 1, bb, idx))
    for k, arr in enumerate(params):
        ins.append(arr)
        in_specs.append(_layer_spec(arr, layer if k < 2 else idx))
    first = prev_new is None
    out_specs = [x_spec, _new_state_spec(CONV_D - 1, bb, idx, n_layers, first)]
    out_shape = [jax.ShapeDtypeStruct(x.shape, x.dtype), _stacked_out(n_layers, bsz, CONV_D - 1, x.dtype)]
    if emit_vn:
        if first:
            out_specs.append(pl.BlockSpec((n_layers, bb, tt, BRANCH), lambda b, j: (0, b, j, 0)))
        else:
            out_specs.append(pl.BlockSpec((None, bb, tt, BRANCH), lambda b, j: (idx, b, j, 0)))
        out_shape.append(_stacked_out(n_layers, bsz, t, x.dtype))
    aliases = {}
    if prev_new is not None:
        for k, arr in enumerate(prev_new):
            aliases[len(ins)] = 1 + k
            ins.append(arr)
            in_specs.append(pl.BlockSpec(memory_space=pl.ANY))
    return pl.pallas_call(
        functools.partial(_odd_kernel, bb=bb, tt=tt, n_t=n_t, mix_len=mix_len,
                          has_hist=state is not None, emit_vn=emit_vn, n_alias=len(aliases),
                          fill=(idx, n_layers) if first else None),
        grid=(bsz // bb, n_t),
        in_specs=in_specs,
        out_specs=out_specs,
        out_shape=out_shape,
        scratch_shapes=[pltpu.VMEM((bb, N_SLABS, PAD_D + tt, LANES), jnp.float32),
                        pltpu.VMEM((bb * tt, D_MODEL), jnp.bfloat16)],
        input_output_aliases=aliases,
        compiler_params=_compiler_params(),
        name="odd_layer",
    )(*ins)


def kernel(x_prompt, x_sample, state_conv_a, state_pool_b, state_conv_d, norm_pre, norm_post, w_in_even, w_out_even, conv_a_w, pool_w, pool_scale, w_in_odd, w_out_odd, sgu_ln_g, sgu_ln_b, sgu_w, sgu_b, conf_dw_w, conf_dw_b, conf_ln_g, conf_ln_b):
    xp, xs = x_prompt, x_sample
    dec_seq = xs.shape[1]
    depth = norm_pre.shape[0]
    n_even, n_odd = w_in_even.shape[0], w_in_odd.shape[0]
    bf16 = jnp.bfloat16
    rows = lambda a: a.reshape(a.shape[0], 1, a.shape[1])
    reps = CHUNK // dec_seq

    even_params = (rows(norm_pre), rows(norm_post), w_in_even.astype(bf16), w_out_even.astype(bf16),
                   conv_a_w, pool_w.astype(bf16), rows(pool_scale))
    odd_head = (rows(norm_pre), rows(norm_post), w_in_odd.astype(bf16), w_out_odd.astype(bf16),
                rows(sgu_ln_g), rows(sgu_ln_b))
    odd_tail = (conf_dw_w, rows(conf_dw_b), rows(conf_ln_g), rows(conf_ln_b))
    odd_params_p = odd_head + (sgu_w, jnp.swapaxes(sgu_b, 1, 2)) + odd_tail
    ws_s = jnp.tile(sgu_w[:, :, :dec_seq, :dec_seq], (1, 1, reps, reps))
    bs_s = jnp.swapaxes(jnp.tile(sgu_b[:, :, :dec_seq], (1, 1, reps)), 1, 2)
    odd_params_s = odd_head + (ws_s, bs_s) + odd_tail

    tiles = lambda rows: (dict(bb=1, tt=rows), dict(bb=rows // dec_seq, tt=dec_seq))
    new_even_p = new_even_s = new_odd_p = new_odd_s = None
    for l in range(depth):
        i = l // 2
        if l % 2 == 0:
            common = dict(layer=l, idx=i, n_layers=n_even)
            prompt, sample = tiles(EVEN_ROWS)
            xp, *new_even_p = _even_layer(xp, None, even_params, new_even_p, start_pos=0, **common, **prompt)
            xs, *new_even_s = _even_layer(xs, (state_conv_a, state_pool_b), even_params, new_even_s,
                                          start_pos=PAST_LEN, **common, **sample)
        else:
            common = dict(layer=l, idx=i, n_layers=n_odd)
            prompt, sample = tiles(ODD_ROWS)
            xp, *new_odd_p = _odd_layer(xp, None, odd_params_p, new_odd_p, mix_len=CHUNK, emit_vn=False,
                                        **common, **prompt)
            xs, *new_odd_s = _odd_layer(xs, state_conv_d, odd_params_s, new_odd_s, mix_len=dec_seq, emit_vn=True,
                                        **common, **sample)
    ca_p, pb_p = new_even_p
    ca_s, pb_s = new_even_s
    (cd_p,) = new_odd_p
    cd_s, v_s = new_odd_s
    return (xp, xs, ca_p, ca_s, pb_p, pb_s, cd_p, cd_s, v_s)
```

```python
import functools

import jax
import jax.numpy as jnp
from jax import lax
from jax.experimental import pallas as pl
from jax.experimental.pallas import tpu as pltpu

D_MODEL = 1024
BRANCH = 512
LANES = 128
N_SLABS = BRANCH // LANES
IN_WIDTH = 6 * BRANCH
CONV_A = 3
POOL_WINDOWS = (2, 4, 8, 16)
POOL_HIST = max(POOL_WINDOWS) - 1
CHUNK = 128
CONV_D = 31
PAST_LEN = 16384
EPS = 1e-6

PAD_A = 8
PAD_B = 16
PAD_D = 32

VMEM_LIMIT_BYTES = 56 * 1024 * 1024

EVEN_TILE = 1024
ODD_TILE = 1024
ODD_SUB = 512
SAMPLE_SEQS = 64


def _rms_norm(x, g):
    ms = jnp.mean(x * x, axis=-1, keepdims=True)
    return x * lax.rsqrt(ms + EPS) * g


def _layer_norm(x, g, b):
    mu = jnp.mean(x, axis=-1, keepdims=True)
    xc = x - mu
    var = jnp.mean(xc * xc, axis=-1, keepdims=True)
    return xc * lax.rsqrt(var + EPS) * g + b


def _sigmoid(x):
    return 0.5 * jnp.tanh(0.5 * x) + 0.5


def _silu(x):
    return x * _sigmoid(x)


def _slab(c):
    return slice(c * LANES, (c + 1) * LANES)


def _get_rows(ref, t0, ts, r):
    if r == 1:
        return ref[0, t0:t0 + ts, :]
    return ref[t0:t0 + ts].reshape(ts * r, ref.shape[-1])


def _set_rows(ref, t0, ts, r, val):
    if r == 1:
        ref[0, t0:t0 + ts, :] = val
    else:
        ref[t0:t0 + ts] = val.reshape(ts, r, ref.shape[-1])


def _carry_history(ext, hist_ref, pad, n_hist, tt, n_t, r):
    j = pl.program_id(1)
    lo, hi = (pad - n_hist) * r, pad * r

    @pl.when(j == 0)
    def _():
        for c in range(N_SLABS):
            if hist_ref is None:
                ext[c, lo:hi, :] = jnp.zeros((hi - lo, LANES), ext.dtype)
            else:
                ext[c, lo:hi, :] = hist_ref[:, :, _slab(c)].reshape(hi - lo, LANES)

    if n_t > 1:
        @pl.when(j > 0)
        def _():
            for c in range(N_SLABS):
                ext[c, lo:hi, :] = ext[c, lo + tt * r:hi + tt * r, :]


def _zero_other_layers(ref, fill):
    if fill is not None:
        for o in range(fill[1]):
            if o != fill[0]:
                ref[o] = jnp.zeros(ref.shape[1:], ref.dtype)


def _emit_history(ext, new_ref, pad, n_hist, tt, n_t, r, fill):
    @pl.when(pl.program_id(1) == n_t - 1)
    def _():
        dst = new_ref if fill is None else new_ref.at[fill[0]]
        for c in range(N_SLABS):
            last = ext[c, (pad + tt - n_hist) * r:(pad + tt) * r, :]
            if r == 1:
                dst[0, :, _slab(c)] = last
            else:
                dst[:, :, _slab(c)] = last.reshape(n_hist, r, LANES)
        _zero_other_layers(new_ref, fill)


def _conv_slab(ext, w_ref, c, pad, t0, ts, n_taps, r):
    acc = None
    for k in range(n_taps):
        lo = (pad + t0 - (n_taps - 1) + k) * r
        term = ext[c, lo:lo + ts * r, :] * w_ref[k:k + 1, _slab(c)]
        acc = term if acc is None else acc + term
    return acc


def _store_slabs(ext, val, pad, t0, ts, r):
    for c in range(N_SLABS):
        ext[c, (pad + t0) * r:(pad + t0 + ts) * r, :] = val[:, _slab(c)]


def _project_out(x, post_ref, wout_ref, y1, y2):
    o = jnp.dot(y1.astype(jnp.bfloat16), wout_ref[:BRANCH, :], preferred_element_type=jnp.float32)
    o = o + jnp.dot(y2.astype(jnp.bfloat16), wout_ref[BRANCH:, :], preferred_element_type=jnp.float32)
    return x + _rms_norm(o, post_ref[...])


def _even_kernel(*refs, tt, r, n_t, start_pos, has_hist, n_alias, fill):
    refs = list(refs)
    x_ref = refs.pop(0)
    ha_ref, hb_ref = (refs.pop(0), refs.pop(0)) if has_hist else (None, None)
    pre_ref, post_ref, win_ref, wout_ref, cw_ref, pw_ref, ps_ref = refs[:7]
    y_ref, na_ref, nb_ref, ext_a, ext_b = refs[7 + n_alias:]
    m = tt * r
    _carry_history(ext_a, ha_ref, PAD_A, CONV_A - 1, tt, n_t, r)
    _carry_history(ext_b, hb_ref, PAD_B, POOL_HIST, tt, n_t, r)

    x = _get_rows(x_ref, 0, tt, r)
    h = _rms_norm(x, pre_ref[...]).astype(jnp.bfloat16)
    z = jnp.dot(h, win_ref[...], preferred_element_type=jnp.float32)
    zb, zc, zh, ga, p, gb = (z[:, i * BRANCH:(i + 1) * BRANCH] for i in range(6))

    _store_slabs(ext_a, zc * zh, PAD_A, 0, tt, r)
    conv = jnp.concatenate([_conv_slab(ext_a, cw_ref, c, PAD_A, 0, tt, CONV_A, r) for c in range(N_SLABS)], axis=-1)
    ya = zb * conv * _silu(ga)

    _store_slabs(ext_b, p, PAD_B, 0, tt, r)
    step = lax.broadcasted_iota(jnp.int32, (m, LANES), 0) // r
    pos = start_pos + pl.program_id(1) * tt + step
    mixed = []
    for g, w in enumerate(POOL_WINDOWS):
        acc = ext_b[g, PAD_B * r:(PAD_B + tt) * r, :]
        for i in range(1, w):
            acc = acc + ext_b[g, (PAD_B - i) * r:(PAD_B - i + tt) * r, :]
        cnt = jnp.minimum(pos + 1, w).astype(jnp.float32)
        d = acc / cnt - p[:, _slab(g)]
        mixed.append(jnp.dot(d.astype(jnp.bfloat16), pw_ref[g], preferred_element_type=jnp.float32))
    yb = jnp.concatenate(mixed, axis=-1) * ps_ref[...] * _silu(gb)

    _set_rows(y_ref, 0, tt, r, _project_out(_get_rows(x_ref, 0, tt, r), post_ref, wout_ref, ya, yb))
    _emit_history(ext_a, na_ref, PAD_A, CONV_A - 1, tt, n_t, r, fill)
    _emit_history(ext_b, nb_ref, PAD_B, POOL_HIST, tt, n_t, r, fill)


def _mix_chunks(vn, ws_ref, bs_ref):
    row = lax.broadcasted_iota(jnp.int32, (CHUNK, CHUNK), 0)
    col = lax.broadcasted_iota(jnp.int32, (CHUNK, CHUNK), 1)
    wm = [jnp.where(col <= row, ws_ref[g], 0.0).astype(jnp.bfloat16) for g in range(N_SLABS)]
    vnb = vn.astype(jnp.bfloat16)
    rows = []
    for ch in range(vn.shape[0] // CHUNK):
        blk = vnb[ch * CHUNK:(ch + 1) * CHUNK, :]
        rows.append(jnp.concatenate(
            [jnp.dot(wm[g], blk[:, _slab(g)], preferred_element_type=jnp.float32) + bs_ref[:, g:g + 1]
             for g in range(N_SLABS)], axis=-1))
    return jnp.concatenate(rows, axis=0)


def _mix_steps(vn, ws_ref, bs_ref, n_steps, r):
    out = []
    for i in range(n_steps):
        acc = None
        for j in range(i + 1):
            term = vn[j * r:(j + 1) * r, :] * ws_ref[i, j:j + 1, :]
            acc = term if acc is None else acc + term
        out.append(acc + bs_ref[i:i + 1, :])
    return jnp.concatenate(out, axis=0)


def _odd_kernel(*refs, tt, ts, r, n_t, has_hist, emit_vn, n_alias, fill):
    refs = list(refs)
    x_ref = refs.pop(0)
    hd_ref = refs.pop(0) if has_hist else None
    (pre_ref, post_ref, win_ref, wout_ref, lng_ref, lnb_ref, ws_ref, bs_ref,
     dw_ref, db_ref, cg_ref, cb_ref) = refs[:12]
    outs = refs[12 + n_alias:]
    if emit_vn:
        y_ref, nd_ref, vn_ref, ext_d, h_ref = outs
    else:
        y_ref, nd_ref, ext_d, h_ref = outs
    _carry_history(ext_d, hd_ref, PAD_D, CONV_D - 1, tt, n_t, r)
    subs = list(range(0, tt, ts))
    ms = ts * r
    hrows = lambda t0: slice(t0 * r, (t0 + ts) * r)

    for t0 in subs:
        h_ref[hrows(t0), :] = _rms_norm(_get_rows(x_ref, t0, ts, r), pre_ref[...]).astype(jnp.bfloat16)

    conv_out = {t0: [] for t0 in subs}
    for c in range(N_SLABS):
        w_c = jnp.concatenate([win_ref[:, 3 * BRANCH + c * LANES:3 * BRANCH + (c + 1) * LANES],
                               win_ref[:, 4 * BRANCH + c * LANES:4 * BRANCH + (c + 1) * LANES]], axis=1)
        for t0 in subs:
            zc = jnp.dot(h_ref[hrows(t0), :], w_c, preferred_element_type=jnp.float32)
            ext_d[c, (PAD_D + t0) * r:(PAD_D + t0 + ts) * r, :] = zc[:, :LANES] * _sigmoid(zc[:, LANES:])
            conv_out[t0].append(_conv_slab(ext_d, dw_ref, c, PAD_D, t0, ts, CONV_D, r))

    z3 = {t0: jnp.dot(h_ref[hrows(t0), :], win_ref[:, :3 * BRANCH], preferred_element_type=jnp.float32)
          for t0 in subs}
    gd = {t0: jnp.dot(h_ref[hrows(t0), :], win_ref[:, 5 * BRANCH:], preferred_element_type=jnp.float32)
          for t0 in subs}

    for t0 in subs:
        u, v, gc = (z3[t0][:, i * BRANCH:(i + 1) * BRANCH] for i in range(3))
        vn = _layer_norm(v, lng_ref[...], lnb_ref[...])
        if emit_vn:
            _set_rows(vn_ref if fill is None else vn_ref.at[fill[0]], t0, ts, r, vn)
        mixed = _mix_chunks(vn, ws_ref, bs_ref) if r == 1 else _mix_steps(vn, ws_ref, bs_ref, ts, r)
        yc = u * mixed * _silu(gc)
        y = jnp.concatenate(conv_out[t0], axis=-1) + db_ref[...]
        yd = _silu(_layer_norm(y, cg_ref[...], cb_ref[...])) * _silu(gd[t0])
        _set_rows(y_ref, t0, ts, r, _project_out(_get_rows(x_ref, t0, ts, r), post_ref, wout_ref, yc, yd))

    if emit_vn:
        _zero_other_layers(vn_ref, fill)
    _emit_history(ext_d, nd_ref, PAD_D, CONV_D - 1, tt, n_t, r, fill)


def _layer_spec(arr, idx):
    tail = arr.shape[1:]
    return pl.BlockSpec((None,) + tail, lambda b, j: (idx,) + (0,) * len(tail), pipeline_mode=pl.Buffered(1))


def _stacked_spec(rows, r, idx, n_layers, whole):
    lead = n_layers if whole else None
    l0 = 0 if whole else idx
    if r == 1:
        return pl.BlockSpec((lead, 1, rows, BRANCH), lambda b, j: (l0, b, 0, 0))
    return pl.BlockSpec((lead, rows, r, BRANCH), lambda b, j: (l0, 0, b, 0))


def _stacked_shape(n_layers, bsz, rows, r, dtype):
    shape = (n_layers, bsz, rows, BRANCH) if r == 1 else (n_layers, rows, bsz, BRANCH)
    return jax.ShapeDtypeStruct(shape, dtype)


def _x_spec(tt, r):
    if r == 1:
        return pl.BlockSpec((1, tt, D_MODEL), lambda b, j: (b, j, 0))
    return pl.BlockSpec((tt, r, D_MODEL), lambda b, j: (0, b, 0))


def _compiler_params():
    return pltpu.CompilerParams(dimension_semantics=("arbitrary", "arbitrary"),
                                vmem_limit_bytes=VMEM_LIMIT_BYTES)


def _alias_inputs(ins, in_specs, prev_new):
    aliases = {}
    for k, arr in enumerate(prev_new or ()):
        aliases[len(ins)] = 1 + k
        ins.append(arr)
        in_specs.append(pl.BlockSpec(memory_space=pl.ANY))
    return aliases


def _even_layer(x, states, params, prev_new, *, layer, idx, n_layers, tt, r, start_pos):
    bsz, t = (x.shape[0], x.shape[1]) if r == 1 else (x.shape[1], x.shape[0])
    n_t = t // tt
    first = prev_new is None
    pre_g, post_g, w_in, w_out, conv_w, pool_w, pool_scale = params
    ins, in_specs = [x], [_x_spec(tt, r)]
    if states is not None:
        ins += list(states)
        in_specs += [_stacked_spec(CONV_A - 1, r, idx, n_layers, False),
                     _stacked_spec(POOL_HIST, r, idx, n_layers, False)]
    ins += [pre_g, post_g, w_in, w_out, conv_w, pool_w, pool_scale]
    in_specs += [_layer_spec(pre_g, layer), _layer_spec(post_g, layer), _layer_spec(w_in, idx),
                 _layer_spec(w_out, idx), _layer_spec(conv_w, idx), _layer_spec(pool_w, idx),
                 _layer_spec(pool_scale, idx)]
    aliases = _alias_inputs(ins, in_specs, prev_new)
    return pl.pallas_call(
        functools.partial(_even_kernel, tt=tt, r=r, n_t=n_t, start_pos=start_pos, has_hist=states is not None,
                          n_alias=len(aliases), fill=(idx, n_layers) if first else None),
        grid=(bsz // r, n_t),
        in_specs=in_specs,
        out_specs=[_x_spec(tt, r), _stacked_spec(CONV_A - 1, r, idx, n_layers, first),
                   _stacked_spec(POOL_HIST, r, idx, n_layers, first)],
        out_shape=[jax.ShapeDtypeStruct(x.shape, x.dtype),
                   _stacked_shape(n_layers, bsz, CONV_A - 1, r, x.dtype),
                   _stacked_shape(n_layers, bsz, POOL_HIST, r, x.dtype)],
        scratch_shapes=[pltpu.VMEM((N_SLABS, (PAD_A + tt) * r, LANES), jnp.float32),
                        pltpu.VMEM((N_SLABS, (PAD_B + tt) * r, LANES), jnp.float32)],
        input_output_aliases=aliases,
        compiler_params=_compiler_params(),
        name="even_layer",
    )(*ins)


def _odd_layer(x, state, params, prev_new, *, layer, idx, n_layers, tt, ts, r, emit_vn):
    bsz, t = (x.shape[0], x.shape[1]) if r == 1 else (x.shape[1], x.shape[0])
    n_t = t // tt
    first = prev_new is None
    ins, in_specs = [x], [_x_spec(tt, r)]
    if state is not None:
        ins.append(state)
        in_specs.append(_stacked_spec(CONV_D - 1, r, idx, n_layers, False))
    for k, arr in enumerate(params):
        ins.append(arr)
        in_specs.append(_layer_spec(arr, layer if k < 2 else idx))
    out_specs = [_x_spec(tt, r), _stacked_spec(CONV_D - 1, r, idx, n_layers, first)]
    out_shape = [jax.ShapeDtypeStruct(x.shape, x.dtype), _stacked_shape(n_layers, bsz, CONV_D - 1, r, x.dtype)]
    if emit_vn:
        out_specs.append(_stacked_spec(tt, r, idx, n_layers, first))
        out_shape.append(_stacked_shape(n_layers, bsz, t, r, x.dtype))
    aliases = _alias_inputs(ins, in_specs, prev_new)
    return pl.pallas_call(
        functools.partial(_odd_kernel, tt=tt, ts=ts, r=r, n_t=n_t, has_hist=state is not None, emit_vn=emit_vn,
                          n_alias=len(aliases), fill=(idx, n_layers) if first else None),
        grid=(bsz // r, n_t),
        in_specs=in_specs,
        out_specs=out_specs,
        out_shape=out_shape,
        scratch_shapes=[pltpu.VMEM((N_SLABS, (PAD_D + tt) * r, LANES), jnp.float32),
                        pltpu.VMEM((tt * r, D_MODEL), jnp.bfloat16)],
        input_output_aliases=aliases,
        compiler_params=_compiler_params(),
        name="odd_layer",
    )(*ins)


def kernel(x_prompt, x_sample, state_conv_a, state_pool_b, state_conv_d, norm_pre, norm_post, w_in_even, w_out_even, conv_a_w, pool_w, pool_scale, w_in_odd, w_out_odd, sgu_ln_g, sgu_ln_b, sgu_w, sgu_b, conf_dw_w, conf_dw_b, conf_ln_g, conf_ln_b):
    xp = x_prompt
    dec_seq = x_sample.shape[1]
    depth = norm_pre.shape[0]
    n_even, n_odd = w_in_even.shape[0], w_in_odd.shape[0]
    bf16 = jnp.bfloat16
    rows = lambda a: a.reshape(a.shape[0], 1, a.shape[1])
    time_major = lambda a: jnp.swapaxes(a, -3, -2)

    even_params = (rows(norm_pre), rows(norm_post), w_in_even.astype(bf16), w_out_even.astype(bf16),
                   conv_a_w, pool_w.astype(bf16), rows(pool_scale))
    odd_head = (rows(norm_pre), rows(norm_post), w_in_odd.astype(bf16), w_out_odd.astype(bf16),
                rows(sgu_ln_g), rows(sgu_ln_b))
    odd_tail = (conf_dw_w, rows(conf_dw_b), rows(conf_ln_g), rows(conf_ln_b))
    odd_params_p = odd_head + (sgu_w, jnp.swapaxes(sgu_b, 1, 2)) + odd_tail
    ws_s = jnp.repeat(jnp.transpose(sgu_w[:, :, :dec_seq, :dec_seq], (0, 2, 3, 1)), LANES, axis=-1)
    bs_s = jnp.repeat(jnp.swapaxes(sgu_b[:, :, :dec_seq], 1, 2), LANES, axis=-1)
    odd_params_s = odd_head + (ws_s, bs_s) + odd_tail

    xs = time_major(x_sample)
    even_states_s = (time_major(state_conv_a), time_major(state_pool_b))
    odd_state_s = time_major(state_conv_d)
    sample = dict(tt=dec_seq, r=SAMPLE_SEQS)
    new_even_p = new_even_s = new_odd_p = new_odd_s = None
    for l in range(depth):
        i = l // 2
        if l % 2 == 0:
            common = dict(layer=l, idx=i, n_layers=n_even)
            xp, *new_even_p = _even_layer(xp, None, even_params, new_even_p, start_pos=0, tt=EVEN_TILE, r=1,
                                          **common)
            xs, *new_even_s = _even_layer(xs, even_states_s, even_params, new_even_s, start_pos=PAST_LEN,
                                          **common, **sample)
        else:
            common = dict(layer=l, idx=i, n_layers=n_odd)
            xp, *new_odd_p = _odd_layer(xp, None, odd_params_p, new_odd_p, emit_vn=False, tt=ODD_TILE, ts=ODD_SUB,
                                        r=1, **common)
            xs, *new_odd_s = _odd_layer(xs, odd_state_s, odd_params_s, new_odd_s, emit_vn=True, ts=dec_seq,
                                        **common, **sample)
    ca_p, pb_p = new_even_p
    ca_s, pb_s = new_even_s
    (cd_p,) = new_odd_p
    cd_s, v_s = new_odd_s
    return (xp, time_major(xs), ca_p, time_major(ca_s), pb_p, time_major(pb_s),
            cd_p, time_major(cd_s), time_major(v_s))
```

```python
import functools

import jax
import jax.numpy as jnp
from jax import lax
from jax.experimental import pallas as pl
from jax.experimental.pallas import tpu as pltpu

D_MODEL = 1024
BRANCH = 512
LANES = 128
N_SLABS = BRANCH // LANES
IN_WIDTH = 6 * BRANCH
CONV_A = 3
POOL_WINDOWS = (2, 4, 8, 16)
POOL_HIST = max(POOL_WINDOWS) - 1
CHUNK = 128
CONV_D = 31
PAST_LEN = 16384
EPS = 1e-6

PAD_A = 8
PAD_B = 16
PAD_D = 32

VMEM_LIMIT_BYTES = 56 * 1024 * 1024

PROMPT_TILE = 1024
PROMPT_SUB = 512
SAMPLE_SEQS = 64


def _rms_norm(x, g):
    ms = jnp.mean(x * x, axis=-1, keepdims=True)
    return x * lax.rsqrt(ms + EPS) * g


def _layer_norm(x, g, b):
    mu = jnp.mean(x, axis=-1, keepdims=True)
    xc = x - mu
    var = jnp.mean(xc * xc, axis=-1, keepdims=True)
    return xc * lax.rsqrt(var + EPS) * g + b


def _sigmoid(x):
    return 0.5 * jnp.tanh(0.5 * x) + 0.5


def _silu(x):
    hx = 0.5 * x
    return hx * jnp.tanh(hx) + hx


def _slab(c):
    return slice(c * LANES, (c + 1) * LANES)


def _row(ref, i):
    return ref[i:i + 1, :]


def _get_rows(ref, t0, ts, r):
    if r == 1:
        return ref[0, t0:t0 + ts, :]
    return ref[t0:t0 + ts].reshape(ts * r, ref.shape[-1])


def _set_rows(ref, t0, ts, r, val):
    if r == 1:
        ref[0, t0:t0 + ts, :] = val
    else:
        ref[t0:t0 + ts] = val.reshape(ts, r, ref.shape[-1])


def _carry_history(ext, hist_ref, pad, n_hist, tt, n_t, r):
    j = pl.program_id(1)
    lo, hi = (pad - n_hist) * r, pad * r

    @pl.when(j == 0)
    def _():
        for c in range(N_SLABS):
            if hist_ref is None:
                ext[c, lo:hi, :] = jnp.zeros((hi - lo, LANES), ext.dtype)
            else:
                ext[c, lo:hi, :] = hist_ref[:, :, _slab(c)].reshape(hi - lo, LANES)

    if n_t > 1:
        @pl.when(j > 0)
        def _():
            for c in range(N_SLABS):
                ext[c, lo:hi, :] = ext[c, lo + tt * r:hi + tt * r, :]


def _zero_other_layers(ref, fill):
    if fill is not None:
        for o in range(fill[1]):
            if o != fill[0]:
                ref[o] = jnp.zeros(ref.shape[1:], ref.dtype)


def _emit_history(ext, new_ref, pad, n_hist, tt, n_t, r, fill):
    @pl.when(pl.program_id(1) == n_t - 1)
    def _():
        dst = new_ref if fill is None else new_ref.at[fill[0]]
        for c in range(N_SLABS):
            last = ext[c, (pad + tt - n_hist) * r:(pad + tt) * r, :]
            if r == 1:
                dst[0, :, _slab(c)] = last
            else:
                dst[:, :, _slab(c)] = last.reshape(n_hist, r, LANES)
        _zero_other_layers(new_ref, fill)


def _conv_slab(ext, w_ref, c, pad, t0, ts, n_taps, r):
    acc = None
    for k in range(n_taps):
        lo = (pad + t0 - (n_taps - 1) + k) * r
        term = ext[c, lo:lo + ts * r, :] * w_ref[k:k + 1, _slab(c)]
        acc = term if acc is None else acc + term
    return acc


def _store_slabs(ext, val, pad, t0, ts, r):
    for c in range(N_SLABS):
        ext[c, (pad + t0) * r:(pad + t0 + ts) * r, :] = val[:, _slab(c)]


def _half_out(wout_ref, y, half):
    return jnp.dot(y.astype(jnp.bfloat16), wout_ref[half * BRANCH:(half + 1) * BRANCH, :],
                   preferred_element_type=jnp.float32)


def _even_kernel(*refs, tt, ts, r, n_t, start_pos, has_hist, n_alias, fill, layer, idx):
    refs = list(refs)
    x_ref = refs.pop(0)
    ha_ref, hb_ref = (refs.pop(0), refs.pop(0)) if has_hist else (None, None)
    pre_ref, post_ref, win_ref, wout_ref, cw_ref, pw_ref, ps_ref = refs[:7]
    y_ref, na_ref, nb_ref, ext_a, ext_b = refs[7 + n_alias:]
    _carry_history(ext_a, ha_ref, PAD_A, CONV_A - 1, tt, n_t, r)
    _carry_history(ext_b, hb_ref, PAD_B, POOL_HIST, tt, n_t, r)
    subs = list(range(0, tt, ts))
    ms = ts * r

    z = {}
    for t0 in subs:
        h = _rms_norm(_get_rows(x_ref, t0, ts, r), _row(pre_ref, layer)).astype(jnp.bfloat16)
        z[t0] = jnp.dot(h, win_ref[...], preferred_element_type=jnp.float32)
    part = lambda t0, i: z[t0][:, i * BRANCH:(i + 1) * BRANCH]

    ya, yb = {}, {}
    for t0 in subs:
        _store_slabs(ext_a, part(t0, 1) * part(t0, 2), PAD_A, t0, ts, r)
        conv = jnp.concatenate([_conv_slab(ext_a, cw_ref, c, PAD_A, t0, ts, CONV_A, r) for c in range(N_SLABS)],
                               axis=-1)
        ya[t0] = part(t0, 0) * conv * _silu(part(t0, 3))
    for t0 in subs:
        p = part(t0, 4)
        _store_slabs(ext_b, p, PAD_B, t0, ts, r)
        step = lax.broadcasted_iota(jnp.int32, (ms, LANES), 0) // r
        pos = start_pos + pl.program_id(1) * tt + t0 + step
        mixed = []
        for g, w in enumerate(POOL_WINDOWS):
            acc = ext_b[g, (PAD_B + t0) * r:(PAD_B + t0 + ts) * r, :]
            for i in range(1, w):
                acc = acc + ext_b[g, (PAD_B + t0 - i) * r:(PAD_B + t0 - i + ts) * r, :]
            cnt = jnp.minimum(pos + 1, w).astype(jnp.float32)
            d = acc / cnt - p[:, _slab(g)]
            mixed.append(jnp.dot(d.astype(jnp.bfloat16), pw_ref[g], preferred_element_type=jnp.float32))
        yb[t0] = jnp.concatenate(mixed, axis=-1) * _row(ps_ref, idx) * _silu(part(t0, 5))
    for t0 in subs:
        o = _half_out(wout_ref, ya[t0], 0) + _half_out(wout_ref, yb[t0], 1)
        _set_rows(y_ref, t0, ts, r, _get_rows(x_ref, t0, ts, r) + _rms_norm(o, _row(post_ref, layer)))

    _emit_history(ext_a, na_ref, PAD_A, CONV_A - 1, tt, n_t, r, fill)
    _emit_history(ext_b, nb_ref, PAD_B, POOL_HIST, tt, n_t, r, fill)


def _mix_chunks(vn, ws_ref, bs_ref):
    row = lax.broadcasted_iota(jnp.int32, (CHUNK, CHUNK), 0)
    col = lax.broadcasted_iota(jnp.int32, (CHUNK, CHUNK), 1)
    wm = [jnp.where(col <= row, ws_ref[g], 0.0).astype(jnp.bfloat16) for g in range(N_SLABS)]
    vnb = vn.astype(jnp.bfloat16)
    rows = []
    for ch in range(vn.shape[0] // CHUNK):
        blk = vnb[ch * CHUNK:(ch + 1) * CHUNK, :]
        rows.append(jnp.concatenate(
            [jnp.dot(wm[g], blk[:, _slab(g)], preferred_element_type=jnp.float32) + bs_ref[:, g:g + 1]
             for g in range(N_SLABS)], axis=-1))
    return jnp.concatenate(rows, axis=0)


def _mix_steps(vn, ws_ref, bs_ref, n_steps, r):
    out = []
    for i in range(n_steps):
        acc = None
        for j in range(i + 1):
            term = vn[j * r:(j + 1) * r, :] * ws_ref[i, j:j + 1, :]
            acc = term if acc is None else acc + term
        out.append(acc + bs_ref[i:i + 1, :])
    return jnp.concatenate(out, axis=0)


def _odd_kernel(*refs, tt, ts, r, n_t, has_hist, emit_vn, n_alias, fill, layer, idx):
    refs = list(refs)
    x_ref = refs.pop(0)
    hd_ref = refs.pop(0) if has_hist else None
    (pre_ref, post_ref, win_ref, wout_ref, lng_ref, lnb_ref, ws_ref, bs_ref,
     dw_ref, db_ref, cg_ref, cb_ref) = refs[:12]
    outs = refs[12 + n_alias:]
    if emit_vn:
        y_ref, nd_ref, vn_ref, ext_d, h_ref = outs
    else:
        y_ref, nd_ref, ext_d, h_ref = outs
    _carry_history(ext_d, hd_ref, PAD_D, CONV_D - 1, tt, n_t, r)
    subs = list(range(0, tt, ts))
    hrows = lambda t0: slice(t0 * r, (t0 + ts) * r)

    for t0 in subs:
        h_ref[hrows(t0), :] = _rms_norm(_get_rows(x_ref, t0, ts, r), _row(pre_ref, layer)).astype(jnp.bfloat16)

    conv_out = {t0: [] for t0 in subs}
    for c in range(N_SLABS):
        w_c = jnp.concatenate([win_ref[:, 3 * BRANCH + c * LANES:3 * BRANCH + (c + 1) * LANES],
                               win_ref[:, 4 * BRANCH + c * LANES:4 * BRANCH + (c + 1) * LANES]], axis=1)
        for t0 in subs:
            zc = jnp.dot(h_ref[hrows(t0), :], w_c, preferred_element_type=jnp.float32)
            ext_d[c, (PAD_D + t0) * r:(PAD_D + t0 + ts) * r, :] = zc[:, :LANES] * _sigmoid(zc[:, LANES:])
            conv_out[t0].append(_conv_slab(ext_d, dw_ref, c, PAD_D, t0, ts, CONV_D, r))

    z3 = {t0: jnp.dot(h_ref[hrows(t0), :], win_ref[:, :3 * BRANCH], preferred_element_type=jnp.float32)
          for t0 in subs}
    gd = {t0: jnp.dot(h_ref[hrows(t0), :], win_ref[:, 5 * BRANCH:], preferred_element_type=jnp.float32)
          for t0 in subs}

    o1, yd = {}, {}
    for t0 in subs:
        u, v, gc = (z3[t0][:, i * BRANCH:(i + 1) * BRANCH] for i in range(3))
        vn = _layer_norm(v, _row(lng_ref, idx), _row(lnb_ref, idx))
        if emit_vn:
            _set_rows(vn_ref if fill is None else vn_ref.at[fill[0]], t0, ts, r, vn)
        mixed = _mix_chunks(vn, ws_ref, bs_ref) if r == 1 else _mix_steps(vn, ws_ref, bs_ref, ts, r)
        o1[t0] = _half_out(wout_ref, u * mixed * _silu(gc), 0)
    for t0 in subs:
        y = jnp.concatenate(conv_out[t0], axis=-1) + _row(db_ref, idx)
        yd[t0] = _silu(_layer_norm(y, _row(cg_ref, idx), _row(cb_ref, idx))) * _silu(gd[t0])
    for t0 in subs:
        o = o1[t0] + _half_out(wout_ref, yd[t0], 1)
        _set_rows(y_ref, t0, ts, r, _get_rows(x_ref, t0, ts, r) + _rms_norm(o, _row(post_ref, layer)))

    if emit_vn:
        _zero_other_layers(vn_ref, fill)
    _emit_history(ext_d, nd_ref, PAD_D, CONV_D - 1, tt, n_t, r, fill)


def _layer_spec(arr, idx):
    if arr.ndim == 2:
        return pl.BlockSpec(arr.shape, lambda b, j: (0, 0), pipeline_mode=pl.Buffered(1))
    tail = arr.shape[1:]
    return pl.BlockSpec((None,) + tail, lambda b, j: (idx,) + (0,) * len(tail), pipeline_mode=pl.Buffered(1))


def _stacked_spec(rows, r, idx, n_layers, whole):
    lead = n_layers if whole else None
    l0 = 0 if whole else idx
    if r == 1:
        return pl.BlockSpec((lead, 1, rows, BRANCH), lambda b, j: (l0, b, 0, 0))
    return pl.BlockSpec((lead, rows, r, BRANCH), lambda b, j: (l0, 0, b, 0))


def _stacked_shape(n_layers, bsz, rows, r, dtype):
    shape = (n_layers, bsz, rows, BRANCH) if r == 1 else (n_layers, rows, bsz, BRANCH)
    return jax.ShapeDtypeStruct(shape, dtype)


def _x_spec(tt, r):
    if r == 1:
        return pl.BlockSpec((1, tt, D_MODEL), lambda b, j: (b, j, 0))
    return pl.BlockSpec((tt, r, D_MODEL), lambda b, j: (0, b, 0))


def _compiler_params():
    return pltpu.CompilerParams(dimension_semantics=("arbitrary", "arbitrary"),
                                vmem_limit_bytes=VMEM_LIMIT_BYTES)


def _alias_inputs(ins, in_specs, prev_new):
    aliases = {}
    for k, arr in enumerate(prev_new or ()):
        aliases[len(ins)] = 1 + k
        ins.append(arr)
        in_specs.append(pl.BlockSpec(memory_space=pl.ANY))
    return aliases


def _even_layer(x, states, params, prev_new, *, layer, idx, n_layers, tt, ts, r, start_pos):
    bsz, t = (x.shape[0], x.shape[1]) if r == 1 else (x.shape[1], x.shape[0])
    n_t = t // tt
    first = prev_new is None
    pre_g, post_g, w_in, w_out, conv_w, pool_w, pool_scale = params
    ins, in_specs = [x], [_x_spec(tt, r)]
    if states is not None:
        ins += list(states)
        in_specs += [_stacked_spec(CONV_A - 1, r, idx, n_layers, False),
                     _stacked_spec(POOL_HIST, r, idx, n_layers, False)]
    ins += [pre_g, post_g, w_in, w_out, conv_w, pool_w, pool_scale]
    in_specs += [_layer_spec(pre_g, layer), _layer_spec(post_g, layer), _layer_spec(w_in, idx),
                 _layer_spec(w_out, idx), _layer_spec(conv_w, idx), _layer_spec(pool_w, idx),
                 _layer_spec(pool_scale, idx)]
    aliases = _alias_inputs(ins, in_specs, prev_new)
    return pl.pallas_call(
        functools.partial(_even_kernel, tt=tt, ts=ts, r=r, n_t=n_t, start_pos=start_pos, has_hist=states is not None,
                          n_alias=len(aliases), fill=(idx, n_layers) if first else None, layer=layer, idx=idx),
        grid=(bsz // r, n_t),
        in_specs=in_specs,
        out_specs=[_x_spec(tt, r), _stacked_spec(CONV_A - 1, r, idx, n_layers, first),
                   _stacked_spec(POOL_HIST, r, idx, n_layers, first)],
        out_shape=[jax.ShapeDtypeStruct(x.shape, x.dtype),
                   _stacked_shape(n_layers, bsz, CONV_A - 1, r, x.dtype),
                   _stacked_shape(n_layers, bsz, POOL_HIST, r, x.dtype)],
        scratch_shapes=[pltpu.VMEM((N_SLABS, (PAD_A + tt) * r, LANES), jnp.float32),
                        pltpu.VMEM((N_SLABS, (PAD_B + tt) * r, LANES), jnp.float32)],
        input_output_aliases=aliases,
        compiler_params=_compiler_params(),
        name="even_layer",
    )(*ins)


def _odd_layer(x, state, params, prev_new, *, layer, idx, n_layers, tt, ts, r, emit_vn):
    bsz, t = (x.shape[0], x.shape[1]) if r == 1 else (x.shape[1], x.shape[0])
    n_t = t // tt
    first = prev_new is None
    ins, in_specs = [x], [_x_spec(tt, r)]
    if state is not None:
        ins.append(state)
        in_specs.append(_stacked_spec(CONV_D - 1, r, idx, n_layers, False))
    for k, arr in enumerate(params):
        ins.append(arr)
        in_specs.append(_layer_spec(arr, layer if k < 2 else idx))
    out_specs = [_x_spec(tt, r), _stacked_spec(CONV_D - 1, r, idx, n_layers, first)]
    out_shape = [jax.ShapeDtypeStruct(x.shape, x.dtype), _stacked_shape(n_layers, bsz, CONV_D - 1, r, x.dtype)]
    if emit_vn:
        out_specs.append(_stacked_spec(tt, r, idx, n_layers, first))
        out_shape.append(_stacked_shape(n_layers, bsz, t, r, x.dtype))
    aliases = _alias_inputs(ins, in_specs, prev_new)
    return pl.pallas_call(
        functools.partial(_odd_kernel, tt=tt, ts=ts, r=r, n_t=n_t, has_hist=state is not None, emit_vn=emit_vn,
                          n_alias=len(aliases), fill=(idx, n_layers) if first else None, layer=layer, idx=idx),
        grid=(bsz // r, n_t),
        in_specs=in_specs,
        out_specs=out_specs,
        out_shape=out_shape,
        scratch_shapes=[pltpu.VMEM((N_SLABS, (PAD_D + tt) * r, LANES), jnp.float32),
                        pltpu.VMEM((tt * r, D_MODEL), jnp.bfloat16)],
        input_output_aliases=aliases,
        compiler_params=_compiler_params(),
        name="odd_layer",
    )(*ins)


def kernel(x_prompt, x_sample, state_conv_a, state_pool_b, state_conv_d, norm_pre, norm_post, w_in_even, w_out_even, conv_a_w, pool_w, pool_scale, w_in_odd, w_out_odd, sgu_ln_g, sgu_ln_b, sgu_w, sgu_b, conf_dw_w, conf_dw_b, conf_ln_g, conf_ln_b):
    xp = x_prompt
    dec_seq = x_sample.shape[1]
    depth = norm_pre.shape[0]
    n_even, n_odd = w_in_even.shape[0], w_in_odd.shape[0]
    bf16 = jnp.bfloat16
    time_major = lambda a: jnp.swapaxes(a, -3, -2)

    even_params = (norm_pre, norm_post, w_in_even.astype(bf16), w_out_even.astype(bf16),
                   conv_a_w, pool_w.astype(bf16), pool_scale)
    odd_head = (norm_pre, norm_post, w_in_odd.astype(bf16), w_out_odd.astype(bf16), sgu_ln_g, sgu_ln_b)
    odd_tail = (conf_dw_w, conf_dw_b, conf_ln_g, conf_ln_b)
    odd_params_p = odd_head + (sgu_w, jnp.swapaxes(sgu_b, 1, 2)) + odd_tail
    ws_s = jnp.repeat(jnp.transpose(sgu_w[:, :, :dec_seq, :dec_seq], (0, 2, 3, 1)), LANES, axis=-1)
    bs_s = jnp.repeat(jnp.swapaxes(sgu_b[:, :, :dec_seq], 1, 2), LANES, axis=-1)
    odd_params_s = odd_head + (ws_s, bs_s) + odd_tail

    xs = time_major(x_sample)
    even_states_s = (time_major(state_conv_a), time_major(state_pool_b))
    odd_state_s = time_major(state_conv_d)
    prompt = dict(tt=PROMPT_TILE, ts=PROMPT_SUB, r=1)
    sample = dict(tt=dec_seq, ts=dec_seq, r=SAMPLE_SEQS)
    new_even_p = new_even_s = new_odd_p = new_odd_s = None
    for l in range(depth):
        i = l // 2
        if l % 2 == 0:
            common = dict(layer=l, idx=i, n_layers=n_even)
            xp, *new_even_p = _even_layer(xp, None, even_params, new_even_p, start_pos=0, **common, **prompt)
            xs, *new_even_s = _even_layer(xs, even_states_s, even_params, new_even_s, start_pos=PAST_LEN,
                                          **common, **sample)
        else:
            common = dict(layer=l, idx=i, n_layers=n_odd)
            xp, *new_odd_p = _odd_layer(xp, None, odd_params_p, new_odd_p, emit_vn=False, **common, **prompt)
            xs, *new_odd_s = _odd_layer(xs, odd_state_s, odd_params_s, new_odd_s, emit_vn=True, **common, **sample)
    ca_p, pb_p = new_even_p
    ca_s, pb_s = new_even_s
    (cd_p,) = new_odd_p
    cd_s, v_s = new_odd_s
    return (xp, time_major(xs), ca_p, time_major(ca_s), pb_p, time_major(pb_s),
            cd_p, time_major(cd_s), time_major(v_s))
```

```python
import functools

import jax
import jax.numpy as jnp
from jax import lax
from jax.experimental import pallas as pl
from jax.experimental.pallas import tpu as pltpu

D_MODEL = 1024
BRANCH = 512
LANES = 128
N_SLABS = BRANCH // LANES
IN_WIDTH = 6 * BRANCH
CONV_A = 3
POOL_WINDOWS = (2, 4, 8, 16)
POOL_HIST = max(POOL_WINDOWS) - 1
CHUNK = 128
CONV_D = 31
PAST_LEN = 16384
EPS = 1e-6

PAD_A = 8
PAD_B = 16
PAD_D = 32

VMEM_LIMIT_BYTES = 56 * 1024 * 1024

PROMPT_TILE = 1024
PROMPT_SUB = 512
SAMPLE_SEQS = 64


def _rms_norm(x, g):
    ms = jnp.mean(x * x, axis=-1, keepdims=True)
    return x * lax.rsqrt(ms + EPS) * g


def _layer_norm(x, g, b):
    mu = jnp.mean(x, axis=-1, keepdims=True)
    xc = x - mu
    var = jnp.mean(xc * xc, axis=-1, keepdims=True)
    return xc * lax.rsqrt(var + EPS) * g + b


def _sigmoid(x):
    return 0.5 * jnp.tanh(0.5 * x) + 0.5


def _silu(x):
    hx = 0.5 * x
    return hx * jnp.tanh(hx) + hx


def _slab(c):
    return slice(c * LANES, (c + 1) * LANES)


def _row(ref, i):
    return ref[i:i + 1, :]


def _get_rows(ref, t0, ts, r):
    if r == 1:
        return ref[0, t0:t0 + ts, :]
    return ref[t0:t0 + ts].reshape(ts * r, ref.shape[-1])


def _set_rows(ref, t0, ts, r, val):
    if r == 1:
        ref[0, t0:t0 + ts, :] = val
    else:
        ref[t0:t0 + ts] = val.reshape(ts, r, ref.shape[-1])


def _carry_history(ext, hist_ref, pad, n_hist, tt, n_t, r):
    j = pl.program_id(1)
    lo, hi = (pad - n_hist) * r, pad * r

    @pl.when(j == 0)
    def _():
        for c in range(N_SLABS):
            if hist_ref is None:
                ext[c, lo:hi, :] = jnp.zeros((hi - lo, LANES), ext.dtype)
            else:
                ext[c, lo:hi, :] = hist_ref[:, :, _slab(c)].reshape(hi - lo, LANES)

    if n_t > 1:
        @pl.when(j > 0)
        def _():
            for c in range(N_SLABS):
                ext[c, lo:hi, :] = ext[c, lo + tt * r:hi + tt * r, :]


def _zero_other_layers(ref, fill):
    if fill is not None:
        for o in range(fill[1]):
            if o != fill[0]:
                ref[o] = jnp.zeros(ref.shape[1:], ref.dtype)


def _emit_history(ext, new_ref, pad, n_hist, tt, n_t, r, fill):
    @pl.when(pl.program_id(1) == n_t - 1)
    def _():
        dst = new_ref if fill is None else new_ref.at[fill[0]]
        for c in range(N_SLABS):
            last = ext[c, (pad + tt - n_hist) * r:(pad + tt) * r, :]
            if r == 1:
                dst[0, :, _slab(c)] = last
            else:
                dst[:, :, _slab(c)] = last.reshape(n_hist, r, LANES)
        _zero_other_layers(new_ref, fill)


def _conv_slab(ext, w_ref, c, pad, t0, ts, n_taps, r):
    acc = None
    for k in range(n_taps):
        lo = (pad + t0 - (n_taps - 1) + k) * r
        term = ext[c, lo:lo + ts * r, :] * w_ref[k:k + 1, _slab(c)]
        acc = term if acc is None else acc + term
    return acc


def _store_slabs(ext, val, pad, t0, ts, r):
    for c in range(N_SLABS):
        ext[c, (pad + t0) * r:(pad + t0 + ts) * r, :] = val[:, _slab(c)]


def _half_out(wout_ref, y, half):
    return jnp.dot(y.astype(jnp.bfloat16), wout_ref[half * BRANCH:(half + 1) * BRANCH, :],
                   preferred_element_type=jnp.float32)


STAGE_ROWS = 256


def _weight_export(w_vmem, w_copy_hbm, sem):
    return pltpu.make_async_copy(w_vmem, w_copy_hbm, sem)


def _stage_weights(w_hbm, idx, w_vmem, w_copy_hbm, stage, sems, export_sem):
    n = w_vmem.shape[0] // STAGE_ROWS

    def fetch(k):
        return pltpu.make_async_copy(w_hbm.at[idx, pl.ds(k * STAGE_ROWS, STAGE_ROWS), :], stage.at[k % 2],
                                     sems.at[k % 2])

    fetch(0).start()
    for k in range(n):
        if k + 1 < n:
            fetch(k + 1).start()
        fetch(k).wait()
        w_vmem[k * STAGE_ROWS:(k + 1) * STAGE_ROWS, :] = stage[k % 2].astype(jnp.bfloat16)
    _weight_export(w_vmem, w_copy_hbm, export_sem).start()


def _prepare_weights(win_ref, wout_ref, tail, idx, n_t):
    win_copy, wout_copy, win_bf, wout_bf, stage_in, stage_out, sems = tail
    first = (pl.program_id(0) == 0) & (pl.program_id(1) == 0)
    last = (pl.program_id(0) == pl.num_programs(0) - 1) & (pl.program_id(1) == n_t - 1)

    @pl.when(first)
    def _():
        _stage_weights(win_ref, idx, win_bf, win_copy, stage_in, sems, sems.at[2])
        _stage_weights(wout_ref, idx, wout_bf, wout_copy, stage_out, sems, sems.at[3])

    def finish():
        @pl.when(last)
        def _():
            _weight_export(win_bf, win_copy, sems.at[2]).wait()
            _weight_export(wout_bf, wout_copy, sems.at[3]).wait()

    return win_bf, wout_bf, finish


def _even_kernel(*refs, tt, ts, r, n_t, start_pos, has_hist, n_alias, fill, layer, idx, cast):
    refs = list(refs)
    x_ref = refs.pop(0)
    ha_ref, hb_ref = (refs.pop(0), refs.pop(0)) if has_hist else (None, None)
    pre_ref, post_ref, win_ref, wout_ref, cw_ref, pw_ref, ps_ref = refs[:7]
    y_ref, na_ref, nb_ref = refs[7 + n_alias:10 + n_alias]
    if cast:
        win_copy, wout_copy, ext_a, ext_b, *tail = refs[10 + n_alias:]
        win_ref, wout_ref, finish = _prepare_weights(win_ref, wout_ref, [win_copy, wout_copy] + tail, idx, n_t)
    else:
        ext_a, ext_b = refs[10 + n_alias:]
    _carry_history(ext_a, ha_ref, PAD_A, CONV_A - 1, tt, n_t, r)
    _carry_history(ext_b, hb_ref, PAD_B, POOL_HIST, tt, n_t, r)
    subs = list(range(0, tt, ts))
    ms = ts * r

    z = {}
    for t0 in subs:
        h = _rms_norm(_get_rows(x_ref, t0, ts, r), _row(pre_ref, layer)).astype(jnp.bfloat16)
        z[t0] = jnp.dot(h, win_ref[...], preferred_element_type=jnp.float32)
    part = lambda t0, i: z[t0][:, i * BRANCH:(i + 1) * BRANCH]

    ya, yb = {}, {}
    for t0 in subs:
        _store_slabs(ext_a, part(t0, 1) * part(t0, 2), PAD_A, t0, ts, r)
        conv = jnp.concatenate([_conv_slab(ext_a, cw_ref, c, PAD_A, t0, ts, CONV_A, r) for c in range(N_SLABS)],
                               axis=-1)
        ya[t0] = part(t0, 0) * conv * _silu(part(t0, 3))
    for t0 in subs:
        p = part(t0, 4)
        _store_slabs(ext_b, p, PAD_B, t0, ts, r)
        step = lax.broadcasted_iota(jnp.int32, (ms, LANES), 0) // r
        pos = start_pos + pl.program_id(1) * tt + t0 + step
        mixed = []
        for g, w in enumerate(POOL_WINDOWS):
            acc = ext_b[g, (PAD_B + t0) * r:(PAD_B + t0 + ts) * r, :]
            for i in range(1, w):
                acc = acc + ext_b[g, (PAD_B + t0 - i) * r:(PAD_B + t0 - i + ts) * r, :]
            cnt = jnp.minimum(pos + 1, w).astype(jnp.float32)
            d = acc / cnt - p[:, _slab(g)]
            mixed.append(jnp.dot(d.astype(jnp.bfloat16), pw_ref[g], preferred_element_type=jnp.float32))
        yb[t0] = jnp.concatenate(mixed, axis=-1) * _row(ps_ref, idx) * _silu(part(t0, 5))
    for t0 in subs:
        o = _half_out(wout_ref, ya[t0], 0) + _half_out(wout_ref, yb[t0], 1)
        _set_rows(y_ref, t0, ts, r, _get_rows(x_ref, t0, ts, r) + _rms_norm(o, _row(post_ref, layer)))

    _emit_history(ext_a, na_ref, PAD_A, CONV_A - 1, tt, n_t, r, fill)
    _emit_history(ext_b, nb_ref, PAD_B, POOL_HIST, tt, n_t, r, fill)
    if cast:
        finish()


def _mix_chunks(vn, ws_ref, bs_ref):
    row = lax.broadcasted_iota(jnp.int32, (CHUNK, CHUNK), 0)
    col = lax.broadcasted_iota(jnp.int32, (CHUNK, CHUNK), 1)
    wm = [jnp.where(col <= row, ws_ref[g], 0.0).astype(jnp.bfloat16) for g in range(N_SLABS)]
    vnb = vn.astype(jnp.bfloat16)
    rows = []
    for ch in range(vn.shape[0] // CHUNK):
        blk = vnb[ch * CHUNK:(ch + 1) * CHUNK, :]
        rows.append(jnp.concatenate(
            [jnp.dot(wm[g], blk[:, _slab(g)], preferred_element_type=jnp.float32) + bs_ref[:, g:g + 1]
             for g in range(N_SLABS)], axis=-1))
    return jnp.concatenate(rows, axis=0)


def _mix_steps(vn, ws_ref, bs_ref, n_steps, r):
    out = []
    for i in range(n_steps):
        acc = None
        for j in range(i + 1):
            term = vn[j * r:(j + 1) * r, :] * ws_ref[i, j:j + 1, :]
            acc = term if acc is None else acc + term
        out.append(acc + bs_ref[i:i + 1, :])
    return jnp.concatenate(out, axis=0)


def _odd_kernel(*refs, tt, ts, r, n_t, has_hist, emit_vn, n_alias, fill, layer, idx, cast):
    refs = list(refs)
    x_ref = refs.pop(0)
    hd_ref = refs.pop(0) if has_hist else None
    (pre_ref, post_ref, win_ref, wout_ref, lng_ref, lnb_ref, ws_ref, bs_ref,
     dw_ref, db_ref, cg_ref, cb_ref) = refs[:12]
    outs = refs[12 + n_alias:]
    y_ref, nd_ref = outs[:2]
    outs = outs[2:]
    vn_ref = outs.pop(0) if emit_vn else None
    if cast:
        win_copy, wout_copy, ext_d, h_ref, *tail = outs
        win_ref, wout_ref, finish = _prepare_weights(win_ref, wout_ref, [win_copy, wout_copy] + tail, idx, n_t)
    else:
        ext_d, h_ref = outs
    _carry_history(ext_d, hd_ref, PAD_D, CONV_D - 1, tt, n_t, r)
    subs = list(range(0, tt, ts))
    hrows = lambda t0: slice(t0 * r, (t0 + ts) * r)

    for t0 in subs:
        h_ref[hrows(t0), :] = _rms_norm(_get_rows(x_ref, t0, ts, r), _row(pre_ref, layer)).astype(jnp.bfloat16)

    conv_out = {t0: [] for t0 in subs}
    for c in range(N_SLABS):
        w_c = jnp.concatenate([win_ref[:, 3 * BRANCH + c * LANES:3 * BRANCH + (c + 1) * LANES],
                               win_ref[:, 4 * BRANCH + c * LANES:4 * BRANCH + (c + 1) * LANES]], axis=1)
        for t0 in subs:
            zc = jnp.dot(h_ref[hrows(t0), :], w_c, preferred_element_type=jnp.float32)
            ext_d[c, (PAD_D + t0) * r:(PAD_D + t0 + ts) * r, :] = zc[:, :LANES] * _sigmoid(zc[:, LANES:])
            conv_out[t0].append(_conv_slab(ext_d, dw_ref, c, PAD_D, t0, ts, CONV_D, r))

    z3 = {t0: jnp.dot(h_ref[hrows(t0), :], win_ref[:, :3 * BRANCH], preferred_element_type=jnp.float32)
          for t0 in subs}
    gd = {t0: jnp.dot(h_ref[hrows(t0), :], win_ref[:, 5 * BRANCH:], preferred_element_type=jnp.float32)
          for t0 in subs}

    o1, yd = {}, {}
    for t0 in subs:
        u, v, gc = (z3[t0][:, i * BRANCH:(i + 1) * BRANCH] for i in range(3))
        vn = _layer_norm(v, _row(lng_ref, idx), _row(lnb_ref, idx))
        if emit_vn:
            _set_rows(vn_ref if fill is None else vn_ref.at[fill[0]], t0, ts, r, vn)
        mixed = _mix_chunks(vn, ws_ref, bs_ref) if r == 1 else _mix_steps(vn, ws_ref, bs_ref, ts, r)
        o1[t0] = _half_out(wout_ref, u * mixed * _silu(gc), 0)
    for t0 in subs:
        y = jnp.concatenate(conv_out[t0], axis=-1) + _row(db_ref, idx)
        yd[t0] = _silu(_layer_norm(y, _row(cg_ref, idx), _row(cb_ref, idx))) * _silu(gd[t0])
    for t0 in subs:
        o = o1[t0] + _half_out(wout_ref, yd[t0], 1)
        _set_rows(y_ref, t0, ts, r, _get_rows(x_ref, t0, ts, r) + _rms_norm(o, _row(post_ref, layer)))

    if emit_vn:
        _zero_other_layers(vn_ref, fill)
    _emit_history(ext_d, nd_ref, PAD_D, CONV_D - 1, tt, n_t, r, fill)
    if cast:
        finish()


def _layer_spec(arr, idx):
    if arr.ndim == 2:
        return pl.BlockSpec(arr.shape, lambda b, j: (0, 0), pipeline_mode=pl.Buffered(1))
    tail = arr.shape[1:]
    return pl.BlockSpec((None,) + tail, lambda b, j: (idx,) + (0,) * len(tail), pipeline_mode=pl.Buffered(1))


def _weight_io(w_in, w_out, cast):
    if not cast:
        return [_layer_spec(w_in, 0), _layer_spec(w_out, 0)], [], [], []
    any_spec = pl.BlockSpec(memory_space=pl.ANY)
    shapes = [jax.ShapeDtypeStruct(w.shape[1:], jnp.bfloat16) for w in (w_in, w_out)]
    scratch = [pltpu.VMEM(w_in.shape[1:], jnp.bfloat16), pltpu.VMEM(w_out.shape[1:], jnp.bfloat16),
               pltpu.VMEM((2, STAGE_ROWS, w_in.shape[2]), jnp.float32),
               pltpu.VMEM((2, STAGE_ROWS, w_out.shape[2]), jnp.float32),
               pltpu.SemaphoreType.DMA((4,))]
    return [any_spec, any_spec], [any_spec, any_spec], shapes, scratch


def _stacked_spec(rows, r, idx, n_layers, whole):
    lead = n_layers if whole else None
    l0 = 0 if whole else idx
    if r == 1:
        return pl.BlockSpec((lead, 1, rows, BRANCH), lambda b, j: (l0, b, 0, 0))
    return pl.BlockSpec((lead, rows, r, BRANCH), lambda b, j: (l0, 0, b, 0))


def _stacked_shape(n_layers, bsz, rows, r, dtype):
    shape = (n_layers, bsz, rows, BRANCH) if r == 1 else (n_layers, rows, bsz, BRANCH)
    return jax.ShapeDtypeStruct(shape, dtype)


def _x_spec(tt, r):
    if r == 1:
        return pl.BlockSpec((1, tt, D_MODEL), lambda b, j: (b, j, 0))
    return pl.BlockSpec((tt, r, D_MODEL), lambda b, j: (0, b, 0))


def _compiler_params():
    return pltpu.CompilerParams(dimension_semantics=("arbitrary", "arbitrary"),
                                vmem_limit_bytes=VMEM_LIMIT_BYTES)


def _alias_inputs(ins, in_specs, prev_new):
    aliases = {}
    for k, arr in enumerate(prev_new or ()):
        aliases[len(ins)] = 1 + k
        ins.append(arr)
        in_specs.append(pl.BlockSpec(memory_space=pl.ANY))
    return aliases


def _even_layer(x, states, params, prev_new, *, layer, idx, n_layers, tt, ts, r, start_pos):
    bsz, t = (x.shape[0], x.shape[1]) if r == 1 else (x.shape[1], x.shape[0])
    n_t = t // tt
    first = prev_new is None
    pre_g, post_g, w_in, w_out, conv_w, pool_w, pool_scale = params
    ins, in_specs = [x], [_x_spec(tt, r)]
    if states is not None:
        ins += list(states)
        in_specs += [_stacked_spec(CONV_A - 1, r, idx, n_layers, False),
                     _stacked_spec(POOL_HIST, r, idx, n_layers, False)]
    cast = w_in.dtype == jnp.float32
    w_specs, w_out_specs, w_out_shapes, w_scratch = _weight_io(w_in, w_out, cast)
    ins += [pre_g, post_g, w_in, w_out, conv_w, pool_w, pool_scale]
    in_specs += [_layer_spec(pre_g, layer), _layer_spec(post_g, layer), *w_specs,
                 _layer_spec(conv_w, idx), _layer_spec(pool_w, idx), _layer_spec(pool_scale, idx)]
    aliases = _alias_inputs(ins, in_specs, prev_new)
    return pl.pallas_call(
        functools.partial(_even_kernel, tt=tt, ts=ts, r=r, n_t=n_t, start_pos=start_pos, has_hist=states is not None,
                          n_alias=len(aliases), fill=(idx, n_layers) if first else None, layer=layer, idx=idx,
                          cast=cast),
        grid=(bsz // r, n_t),
        in_specs=in_specs,
        out_specs=[_x_spec(tt, r), _stacked_spec(CONV_A - 1, r, idx, n_layers, first),
                   _stacked_spec(POOL_HIST, r, idx, n_layers, first)] + w_out_specs,
        out_shape=[jax.ShapeDtypeStruct(x.shape, x.dtype),
                   _stacked_shape(n_layers, bsz, CONV_A - 1, r, x.dtype),
                   _stacked_shape(n_layers, bsz, POOL_HIST, r, x.dtype)] + w_out_shapes,
        scratch_shapes=[pltpu.VMEM((N_SLABS, (PAD_A + tt) * r, LANES), jnp.float32),
                        pltpu.VMEM((N_SLABS, (PAD_B + tt) * r, LANES), jnp.float32)] + w_scratch,
        input_output_aliases=aliases,
        compiler_params=_compiler_params(),
        name="even_layer",
    )(*ins)


def _odd_layer(x, state, params, prev_new, *, layer, idx, n_layers, tt, ts, r, emit_vn):
    bsz, t = (x.shape[0], x.shape[1]) if r == 1 else (x.shape[1], x.shape[0])
    n_t = t // tt
    first = prev_new is None
    ins, in_specs = [x], [_x_spec(tt, r)]
    if state is not None:
        ins.append(state)
        in_specs.append(_stacked_spec(CONV_D - 1, r, idx, n_layers, False))
    cast = params[2].dtype == jnp.float32
    w_specs, w_out_specs, w_out_shapes, w_scratch = _weight_io(params[2], params[3], cast)
    for k, arr in enumerate(params):
        ins.append(arr)
        in_specs.append(w_specs[k - 2] if k in (2, 3) else _layer_spec(arr, layer if k < 2 else idx))
    out_specs = [_x_spec(tt, r), _stacked_spec(CONV_D - 1, r, idx, n_layers, first)]
    out_shape = [jax.ShapeDtypeStruct(x.shape, x.dtype), _stacked_shape(n_layers, bsz, CONV_D - 1, r, x.dtype)]
    if emit_vn:
        out_specs.append(_stacked_spec(tt, r, idx, n_layers, first))
        out_shape.append(_stacked_shape(n_layers, bsz, t, r, x.dtype))
    out_specs += w_out_specs
    out_shape += w_out_shapes
    aliases = _alias_inputs(ins, in_specs, prev_new)
    return pl.pallas_call(
        functools.partial(_odd_kernel, tt=tt, ts=ts, r=r, n_t=n_t, has_hist=state is not None, emit_vn=emit_vn,
                          n_alias=len(aliases), fill=(idx, n_layers) if first else None, layer=layer, idx=idx,
                          cast=cast),
        grid=(bsz // r, n_t),
        in_specs=in_specs,
        out_specs=out_specs,
        out_shape=out_shape,
        scratch_shapes=[pltpu.VMEM((N_SLABS, (PAD_D + tt) * r, LANES), jnp.float32),
                        pltpu.VMEM((tt * r, D_MODEL), jnp.bfloat16)] + w_scratch,
        input_output_aliases=aliases,
        compiler_params=_compiler_params(),
        name="odd_layer",
    )(*ins)


def kernel(x_prompt, x_sample, state_conv_a, state_pool_b, state_conv_d, norm_pre, norm_post, w_in_even, w_out_even, conv_a_w, pool_w, pool_scale, w_in_odd, w_out_odd, sgu_ln_g, sgu_ln_b, sgu_w, sgu_b, conf_dw_w, conf_dw_b, conf_ln_g, conf_ln_b):
    xp = x_prompt
    dec_seq = x_sample.shape[1]
    depth = norm_pre.shape[0]
    n_even, n_odd = w_in_even.shape[0], w_in_odd.shape[0]
    bf16 = jnp.bfloat16
    time_major = lambda a: jnp.swapaxes(a, -3, -2)

    even_tail = (conv_a_w, pool_w.astype(bf16), pool_scale)
    odd_tail = (conf_dw_w, conf_dw_b, conf_ln_g, conf_ln_b)
    mix_p = (sgu_w, jnp.swapaxes(sgu_b, 1, 2))
    mix_s = (jnp.repeat(jnp.transpose(sgu_w[:, :, :dec_seq, :dec_seq], (0, 2, 3, 1)), LANES, axis=-1),
             jnp.repeat(jnp.swapaxes(sgu_b[:, :, :dec_seq], 1, 2), LANES, axis=-1))

    xs = time_major(x_sample)
    even_states_s = (time_major(state_conv_a), time_major(state_pool_b))
    odd_state_s = time_major(state_conv_d)
    prompt = dict(tt=PROMPT_TILE, ts=PROMPT_SUB, r=1)
    sample = dict(tt=dec_seq, ts=dec_seq, r=SAMPLE_SEQS)
    new_even_p = new_even_s = new_odd_p = new_odd_s = None
    for l in range(depth):
        i = l // 2
        if l % 2 == 0:
            common = dict(layer=l, idx=i, n_layers=n_even)
            params = (norm_pre, norm_post, w_in_even, w_out_even) + even_tail
            xp, *new_even_p, w_in_bf, w_out_bf = _even_layer(xp, None, params, new_even_p, start_pos=0,
                                                             **common, **prompt)
            params = (norm_pre, norm_post, w_in_bf, w_out_bf) + even_tail
            xs, *new_even_s = _even_layer(xs, even_states_s, params, new_even_s, start_pos=PAST_LEN,
                                          **common, **sample)
        else:
            common = dict(layer=l, idx=i, n_layers=n_odd)
            params = (norm_pre, norm_post, w_in_odd, w_out_odd, sgu_ln_g, sgu_ln_b) + mix_p + odd_tail
            xp, *new_odd_p, w_in_bf, w_out_bf = _odd_layer(xp, None, params, new_odd_p, emit_vn=False,
                                                           **common, **prompt)
            params = (norm_pre, norm_post, w_in_bf, w_out_bf, sgu_ln_g, sgu_ln_b) + mix_s + odd_tail
            xs, *new_odd_s = _odd_layer(xs, odd_state_s, params, new_odd_s, emit_vn=True, **common, **sample)
    ca_p, pb_p = new_even_p
    ca_s, pb_s = new_even_s
    (cd_p,) = new_odd_p
    cd_s, v_s = new_odd_s
    return (xp, time_major(xs), ca_p, time_major(ca_s), pb_p, time_major(pb_s),
            cd_p, time_major(cd_s), time_major(v_s))
```

```python
import functools

import jax
import jax.numpy as jnp
from jax import lax
from jax.experimental import pallas as pl
from jax.experimental.pallas import tpu as pltpu

D_MODEL = 1024
BRANCH = 512
LANES = 128
N_SLABS = BRANCH // LANES
IN_WIDTH = 6 * BRANCH
CONV_A = 3
POOL_WINDOWS = (2, 4, 8, 16)
POOL_HIST = max(POOL_WINDOWS) - 1
CHUNK = 128
CONV_D = 31
PAST_LEN = 16384
EPS = 1e-6

PAD_A = 8
PAD_B = 16
PAD_D = 32

VMEM_LIMIT_BYTES = 56 * 1024 * 1024

PROMPT_TILE = 1024
EVEN_SUB = 256
ODD_SUB = 512
SAMPLE_SEQS = 64


def _rms_norm(x, g):
    ms = jnp.mean(x * x, axis=-1, keepdims=True)
    return x * lax.rsqrt(ms + EPS) * g


def _layer_norm(x, g, b):
    mu = jnp.mean(x, axis=-1, keepdims=True)
    xc = x - mu
    var = jnp.mean(xc * xc, axis=-1, keepdims=True)
    return xc * lax.rsqrt(var + EPS) * g + b


def _sigmoid(x):
    return 0.5 * jnp.tanh(0.5 * x) + 0.5


def _silu(x):
    hx = 0.5 * x
    return hx * jnp.tanh(hx) + hx


def _slab(c):
    return slice(c * LANES, (c + 1) * LANES)


def _row(ref, i):
    return ref[i:i + 1, :]


def _get_rows(ref, t0, ts, r):
    if r == 1:
        return ref[0, t0:t0 + ts, :]
    return ref[t0:t0 + ts].reshape(ts * r, ref.shape[-1])


def _set_rows(ref, t0, ts, r, val):
    if r == 1:
        ref[0, t0:t0 + ts, :] = val
    else:
        ref[t0:t0 + ts] = val.reshape(ts, r, ref.shape[-1])


def _carry_history(ext, hist_ref, pad, n_hist, tt, n_t, r):
    j = pl.program_id(1)
    lo, hi = (pad - n_hist) * r, pad * r

    @pl.when(j == 0)
    def _():
        for c in range(N_SLABS):
            if hist_ref is None:
                ext[c, lo:hi, :] = jnp.zeros((hi - lo, LANES), ext.dtype)
            else:
                ext[c, lo:hi, :] = hist_ref[:, :, _slab(c)].reshape(hi - lo, LANES)

    if n_t > 1:
        @pl.when(j > 0)
        def _():
            for c in range(N_SLABS):
                ext[c, lo:hi, :] = ext[c, lo + tt * r:hi + tt * r, :]


def _zero_other_layers(ref, fill):
    if fill is not None:
        for o in range(fill[1]):
            if o != fill[0]:
                ref[o] = jnp.zeros(ref.shape[1:], ref.dtype)


def _emit_history(ext, new_ref, pad, n_hist, tt, n_t, r, fill):
    @pl.when(pl.program_id(1) == n_t - 1)
    def _():
        dst = new_ref if fill is None else new_ref.at[fill[0]]
        for c in range(N_SLABS):
            last = ext[c, (pad + tt - n_hist) * r:(pad + tt) * r, :]
            if r == 1:
                dst[0, :, _slab(c)] = last
            else:
                dst[:, :, _slab(c)] = last.reshape(n_hist, r, LANES)
        _zero_other_layers(new_ref, fill)


def _conv_slab(ext, w_ref, c, pad, t0, ts, n_taps, r):
    acc = None
    for k in range(n_taps):
        lo = (pad + t0 - (n_taps - 1) + k) * r
        term = ext[c, lo:lo + ts * r, :] * w_ref[k:k + 1, _slab(c)]
        acc = term if acc is None else acc + term
    return acc


def _store_slabs(ext, val, pad, t0, ts, r):
    for c in range(N_SLABS):
        ext[c, (pad + t0) * r:(pad + t0 + ts) * r, :] = val[:, _slab(c)]


def _half_out(wout_ref, y, half):
    return jnp.dot(y.astype(jnp.bfloat16), wout_ref[half * BRANCH:(half + 1) * BRANCH, :],
                   preferred_element_type=jnp.float32)


def _even_kernel(*refs, tt, ts, r, n_t, start_pos, has_hist, n_alias, fill, layer, idx):
    refs = list(refs)
    x_ref = refs.pop(0)
    ha_ref, hb_ref = (refs.pop(0), refs.pop(0)) if has_hist else (None, None)
    pre_ref, post_ref, win_ref, wout_ref, cw_ref, pw_ref, ps_ref = refs[:7]
    y_ref, na_ref, nb_ref, ext_a, ext_b = refs[7 + n_alias:]
    _carry_history(ext_a, ha_ref, PAD_A, CONV_A - 1, tt, n_t, r)
    _carry_history(ext_b, hb_ref, PAD_B, POOL_HIST, tt, n_t, r)
    subs = list(range(0, tt, ts))
    ms = ts * r

    z = {}
    for t0 in subs:
        h = _rms_norm(_get_rows(x_ref, t0, ts, r), _row(pre_ref, layer)).astype(jnp.bfloat16)
        z[t0] = jnp.dot(h, win_ref[...], preferred_element_type=jnp.float32)
    part = lambda t0, i: z[t0][:, i * BRANCH:(i + 1) * BRANCH]

    ya, yb = {}, {}
    for t0 in subs:
        _store_slabs(ext_a, part(t0, 1) * part(t0, 2), PAD_A, t0, ts, r)
        conv = jnp.concatenate([_conv_slab(ext_a, cw_ref, c, PAD_A, t0, ts, CONV_A, r) for c in range(N_SLABS)],
                               axis=-1)
        ya[t0] = part(t0, 0) * conv * _silu(part(t0, 3))
    for t0 in subs:
        p = part(t0, 4)
        _store_slabs(ext_b, p, PAD_B, t0, ts, r)
        step = lax.broadcasted_iota(jnp.int32, (ms, LANES), 0) // r
        pos = start_pos + pl.program_id(1) * tt + t0 + step
        mixed = []
        for g, w in enumerate(POOL_WINDOWS):
            acc = ext_b[g, (PAD_B + t0) * r:(PAD_B + t0 + ts) * r, :]
            for i in range(1, w):
                acc = acc + ext_b[g, (PAD_B + t0 - i) * r:(PAD_B + t0 - i + ts) * r, :]
            cnt = jnp.minimum(pos + 1, w).astype(jnp.float32)
            d = acc / cnt - p[:, _slab(g)]
            mixed.append(jnp.dot(d.astype(jnp.bfloat16), pw_ref[g], preferred_element_type=jnp.float32))
        yb[t0] = jnp.concatenate(mixed, axis=-1) * _row(ps_ref, idx) * _silu(part(t0, 5))
    for t0 in subs:
        o = _half_out(wout_ref, ya[t0], 0) + _half_out(wout_ref, yb[t0], 1)
        _set_rows(y_ref, t0, ts, r, _get_rows(x_ref, t0, ts, r) + _rms_norm(o, _row(post_ref, layer)))

    _emit_history(ext_a, na_ref, PAD_A, CONV_A - 1, tt, n_t, r, fill)
    _emit_history(ext_b, nb_ref, PAD_B, POOL_HIST, tt, n_t, r, fill)


def _mix_chunks(vn, ws_ref, bs_ref):
    row = lax.broadcasted_iota(jnp.int32, (CHUNK, CHUNK), 0)
    col = lax.broadcasted_iota(jnp.int32, (CHUNK, CHUNK), 1)
    wm = [jnp.where(col <= row, ws_ref[g], 0.0).astype(jnp.bfloat16) for g in range(N_SLABS)]
    vnb = vn.astype(jnp.bfloat16)
    rows = []
    for ch in range(vn.shape[0] // CHUNK):
        blk = vnb[ch * CHUNK:(ch + 1) * CHUNK, :]
        rows.append(jnp.concatenate(
            [jnp.dot(wm[g], blk[:, _slab(g)], preferred_element_type=jnp.float32) + bs_ref[:, g:g + 1]
             for g in range(N_SLABS)], axis=-1))
    return jnp.concatenate(rows, axis=0)


def _mix_steps(vn, ws_ref, bs_ref, n_steps, r):
    out = []
    for i in range(n_steps):
        acc = None
        for j in range(i + 1):
            term = vn[j * r:(j + 1) * r, :] * ws_ref[i, j:j + 1, :]
            acc = term if acc is None else acc + term
        out.append(acc + bs_ref[i:i + 1, :])
    return jnp.concatenate(out, axis=0)


def _odd_kernel(*refs, tt, ts, r, n_t, has_hist, emit_vn, n_alias, fill, layer, idx):
    refs = list(refs)
    x_ref = refs.pop(0)
    hd_ref = refs.pop(0) if has_hist else None
    (pre_ref, post_ref, win_ref, wout_ref, lng_ref, lnb_ref, ws_ref, bs_ref,
     dw_ref, db_ref, cg_ref, cb_ref) = refs[:12]
    outs = refs[12 + n_alias:]
    if emit_vn:
        y_ref, nd_ref, vn_ref, ext_d, h_ref = outs
    else:
        y_ref, nd_ref, ext_d, h_ref = outs
    _carry_history(ext_d, hd_ref, PAD_D, CONV_D - 1, tt, n_t, r)
    subs = list(range(0, tt, ts))
    hrows = lambda t0: slice(t0 * r, (t0 + ts) * r)

    for t0 in subs:
        h_ref[hrows(t0), :] = _rms_norm(_get_rows(x_ref, t0, ts, r), _row(pre_ref, layer)).astype(jnp.bfloat16)

    conv_out = {t0: [] for t0 in subs}
    for c in range(N_SLABS):
        w_c = jnp.concatenate([win_ref[:, 3 * BRANCH + c * LANES:3 * BRANCH + (c + 1) * LANES],
                               win_ref[:, 4 * BRANCH + c * LANES:4 * BRANCH + (c + 1) * LANES]], axis=1)
        for t0 in subs:
            zc = jnp.dot(h_ref[hrows(t0), :], w_c, preferred_element_type=jnp.float32)
            ext_d[c, (PAD_D + t0) * r:(PAD_D + t0 + ts) * r, :] = zc[:, :LANES] * _sigmoid(zc[:, LANES:])
            conv_out[t0].append(_conv_slab(ext_d, dw_ref, c, PAD_D, t0, ts, CONV_D, r))

    z3 = {t0: jnp.dot(h_ref[hrows(t0), :], win_ref[:, :3 * BRANCH], preferred_element_type=jnp.float32)
          for t0 in subs}
    gd = {t0: jnp.dot(h_ref[hrows(t0), :], win_ref[:, 5 * BRANCH:], preferred_element_type=jnp.float32)
          for t0 in subs}

    o1, yd = {}, {}
    for t0 in subs:
        u, v, gc = (z3[t0][:, i * BRANCH:(i + 1) * BRANCH] for i in range(3))
        vn = _layer_norm(v, _row(lng_ref, idx), _row(lnb_ref, idx))
        if emit_vn:
            _set_rows(vn_ref if fill is None else vn_ref.at[fill[0]], t0, ts, r, vn)
        mixed = _mix_chunks(vn, ws_ref, bs_ref) if r == 1 else _mix_steps(vn, ws_ref, bs_ref, ts, r)
        o1[t0] = _half_out(wout_ref, u * mixed * _silu(gc), 0)
    for t0 in subs:
        y = jnp.concatenate(conv_out[t0], axis=-1) + _row(db_ref, idx)
        yd[t0] = _silu(_layer_norm(y, _row(cg_ref, idx), _row(cb_ref, idx))) * _silu(gd[t0])
    for t0 in subs:
        o = o1[t0] + _half_out(wout_ref, yd[t0], 1)
        _set_rows(y_ref, t0, ts, r, _get_rows(x_ref, t0, ts, r) + _rms_norm(o, _row(post_ref, layer)))

    if emit_vn:
        _zero_other_layers(vn_ref, fill)
    _emit_history(ext_d, nd_ref, PAD_D, CONV_D - 1, tt, n_t, r, fill)


def _layer_spec(arr, idx):
    if arr.ndim == 2:
        return pl.BlockSpec(arr.shape, lambda b, j: (0, 0), pipeline_mode=pl.Buffered(1))
    tail = arr.shape[1:]
    return pl.BlockSpec((None,) + tail, lambda b, j: (idx,) + (0,) * len(tail), pipeline_mode=pl.Buffered(1))


def _stacked_spec(rows, r, idx, n_layers, whole):
    lead = n_layers if whole else None
    l0 = 0 if whole else idx
    if r == 1:
        return pl.BlockSpec((lead, 1, rows, BRANCH), lambda b, j: (l0, b, 0, 0))
    return pl.BlockSpec((lead, rows, r, BRANCH), lambda b, j: (l0, 0, b, 0))


def _stacked_shape(n_layers, bsz, rows, r, dtype):
    shape = (n_layers, bsz, rows, BRANCH) if r == 1 else (n_layers, rows, bsz, BRANCH)
    return jax.ShapeDtypeStruct(shape, dtype)


def _x_spec(tt, r):
    if r == 1:
        return pl.BlockSpec((1, tt, D_MODEL), lambda b, j: (b, j, 0))
    return pl.BlockSpec((tt, r, D_MODEL), lambda b, j: (0, b, 0))


def _compiler_params():
    return pltpu.CompilerParams(dimension_semantics=("arbitrary", "arbitrary"),
                                vmem_limit_bytes=VMEM_LIMIT_BYTES)


def _alias_inputs(ins, in_specs, prev_new):
    aliases = {}
    for k, arr in enumerate(prev_new or ()):
        aliases[len(ins)] = 1 + k
        ins.append(arr)
        in_specs.append(pl.BlockSpec(memory_space=pl.ANY))
    return aliases


def _even_layer(x, states, params, prev_new, *, layer, idx, n_layers, tt, ts, r, start_pos):
    bsz, t = (x.shape[0], x.shape[1]) if r == 1 else (x.shape[1], x.shape[0])
    n_t = t // tt
    first = prev_new is None
    pre_g, post_g, w_in, w_out, conv_w, pool_w, pool_scale = params
    ins, in_specs = [x], [_x_spec(tt, r)]
    if states is not None:
        ins += list(states)
        in_specs += [_stacked_spec(CONV_A - 1, r, idx, n_layers, False),
                     _stacked_spec(POOL_HIST, r, idx, n_layers, False)]
    ins += [pre_g, post_g, w_in, w_out, conv_w, pool_w, pool_scale]
    in_specs += [_layer_spec(pre_g, layer), _layer_spec(post_g, layer), _layer_spec(w_in, idx),
                 _layer_spec(w_out, idx), _layer_spec(conv_w, idx), _layer_spec(pool_w, idx),
                 _layer_spec(pool_scale, idx)]
    aliases = _alias_inputs(ins, in_specs, prev_new)
    return pl.pallas_call(
        functools.partial(_even_kernel, tt=tt, ts=ts, r=r, n_t=n_t, start_pos=start_pos, has_hist=states is not None,
                          n_alias=len(aliases), fill=(idx, n_layers) if first else None, layer=layer, idx=idx),
        grid=(bsz // r, n_t),
        in_specs=in_specs,
        out_specs=[_x_spec(tt, r), _stacked_spec(CONV_A - 1, r, idx, n_layers, first),
                   _stacked_spec(POOL_HIST, r, idx, n_layers, first)],
        out_shape=[jax.ShapeDtypeStruct(x.shape, x.dtype),
                   _stacked_shape(n_layers, bsz, CONV_A - 1, r, x.dtype),
                   _stacked_shape(n_layers, bsz, POOL_HIST, r, x.dtype)],
        scratch_shapes=[pltpu.VMEM((N_SLABS, (PAD_A + tt) * r, LANES), jnp.float32),
                        pltpu.VMEM((N_SLABS, (PAD_B + tt) * r, LANES), jnp.float32)],
        input_output_aliases=aliases,
        compiler_params=_compiler_params(),
        name="even_layer",
    )(*ins)


def _odd_layer(x, state, params, prev_new, *, layer, idx, n_layers, tt, ts, r, emit_vn):
    bsz, t = (x.shape[0], x.shape[1]) if r == 1 else (x.shape[1], x.shape[0])
    n_t = t // tt
    first = prev_new is None
    ins, in_specs = [x], [_x_spec(tt, r)]
    if state is not None:
        ins.append(state)
        in_specs.append(_stacked_spec(CONV_D - 1, r, idx, n_layers, False))
    for k, arr in enumerate(params):
        ins.append(arr)
        in_specs.append(_layer_spec(arr, layer if k < 2 else idx))
    out_specs = [_x_spec(tt, r), _stacked_spec(CONV_D - 1, r, idx, n_layers, first)]
    out_shape = [jax.ShapeDtypeStruct(x.shape, x.dtype), _stacked_shape(n_layers, bsz, CONV_D - 1, r, x.dtype)]
    if emit_vn:
        out_specs.append(_stacked_spec(tt, r, idx, n_layers, first))
        out_shape.append(_stacked_shape(n_layers, bsz, t, r, x.dtype))
    aliases = _alias_inputs(ins, in_specs, prev_new)
    return pl.pallas_call(
        functools.partial(_odd_kernel, tt=tt, ts=ts, r=r, n_t=n_t, has_hist=state is not None, emit_vn=emit_vn,
                          n_alias=len(aliases), fill=(idx, n_layers) if first else None, layer=layer, idx=idx),
        grid=(bsz // r, n_t),
        in_specs=in_specs,
        out_specs=out_specs,
        out_shape=out_shape,
        scratch_shapes=[pltpu.VMEM((N_SLABS, (PAD_D + tt) * r, LANES), jnp.float32),
                        pltpu.VMEM((tt * r, D_MODEL), jnp.bfloat16)],
        input_output_aliases=aliases,
        compiler_params=_compiler_params(),
        name="odd_layer",
    )(*ins)


def kernel(x_prompt, x_sample, state_conv_a, state_pool_b, state_conv_d, norm_pre, norm_post, w_in_even, w_out_even, conv_a_w, pool_w, pool_scale, w_in_odd, w_out_odd, sgu_ln_g, sgu_ln_b, sgu_w, sgu_b, conf_dw_w, conf_dw_b, conf_ln_g, conf_ln_b):
    xp = x_prompt
    dec_seq = x_sample.shape[1]
    depth = norm_pre.shape[0]
    n_even, n_odd = w_in_even.shape[0], w_in_odd.shape[0]
    bf16 = jnp.bfloat16
    time_major = lambda a: jnp.swapaxes(a, -3, -2)

    even_params = (norm_pre, norm_post, w_in_even.astype(bf16), w_out_even.astype(bf16),
                   conv_a_w, pool_w.astype(bf16), pool_scale)
    odd_head = (norm_pre, norm_post, w_in_odd.astype(bf16), w_out_odd.astype(bf16), sgu_ln_g, sgu_ln_b)
    odd_tail = (conf_dw_w, conf_dw_b, conf_ln_g, conf_ln_b)
    odd_params_p = odd_head + (sgu_w, jnp.swapaxes(sgu_b, 1, 2)) + odd_tail
    ws_s = jnp.repeat(jnp.transpose(sgu_w[:, :, :dec_seq, :dec_seq], (0, 2, 3, 1)), LANES, axis=-1)
    bs_s = jnp.repeat(jnp.swapaxes(sgu_b[:, :, :dec_seq], 1, 2), LANES, axis=-1)
    odd_params_s = odd_head + (ws_s, bs_s) + odd_tail

    xs = time_major(x_sample)
    even_states_s = (time_major(state_conv_a), time_major(state_pool_b))
    odd_state_s = time_major(state_conv_d)
    sample = dict(tt=dec_seq, ts=dec_seq, r=SAMPLE_SEQS)
    new_even_p = new_even_s = new_odd_p = new_odd_s = None
    for l in range(depth):
        i = l // 2
        if l % 2 == 0:
            common = dict(layer=l, idx=i, n_layers=n_even)
            prompt = dict(tt=PROMPT_TILE, ts=EVEN_SUB, r=1)
            xp, *new_even_p = _even_layer(xp, None, even_params, new_even_p, start_pos=0, **common, **prompt)
            xs, *new_even_s = _even_layer(xs, even_states_s, even_params, new_even_s, start_pos=PAST_LEN,
                                          **common, **sample)
        else:
            common = dict(layer=l, idx=i, n_layers=n_odd)
            prompt = dict(tt=PROMPT_TILE, ts=ODD_SUB, r=1)
            xp, *new_odd_p = _odd_layer(xp, None, odd_params_p, new_odd_p, emit_vn=False, **common, **prompt)
            xs, *new_odd_s = _odd_layer(xs, odd_state_s, odd_params_s, new_odd_s, emit_vn=True, **common, **sample)
    ca_p, pb_p = new_even_p
    ca_s, pb_s = new_even_s
    (cd_p,) = new_odd_p
    cd_s, v_s = new_odd_s
    return (xp, time_major(xs), ca_p, time_major(ca_s), pb_p, time_major(pb_s),
            cd_p, time_major(cd_s), time_major(v_s))
```

```python
import functools

import jax
import jax.numpy as jnp
from jax import lax
from jax.experimental import pallas as pl
from jax.experimental.pallas import tpu as pltpu

D_MODEL = 1024
BRANCH = 512
LANES = 128
N_SLABS = BRANCH // LANES
IN_WIDTH = 6 * BRANCH
CONV_A = 3
POOL_WINDOWS = (2, 4, 8, 16)
POOL_HIST = max(POOL_WINDOWS) - 1
CHUNK = 128
CONV_D = 31
PAST_LEN = 16384
EPS = 1e-6

PAD_A = 8
PAD_B = 16
PAD_D = 32

VMEM_LIMIT_BYTES = 56 * 1024 * 1024

PROMPT_TILE = 1024
EVEN_SUB = 256
ODD_SUB = 512
SAMPLE_SEQS = 64


def _rms_norm(x, g):
    ms = jnp.mean(x * x, axis=-1, keepdims=True)
    return x * lax.rsqrt(ms + EPS) * g


def _layer_norm(x, g, b):
    mu = jnp.mean(x, axis=-1, keepdims=True)
    xc = x - mu
    var = jnp.mean(xc * xc, axis=-1, keepdims=True)
    return xc * lax.rsqrt(var + EPS) * g + b


def _sigmoid(x):
    return 0.5 * jnp.tanh(0.5 * x) + 0.5


def _silu(x):
    hx = 0.5 * x
    return hx * jnp.tanh(hx) + hx


def _slab(c):
    return slice(c * LANES, (c + 1) * LANES)


def _row(ref, i):
    return ref[i:i + 1, :]


def _get_rows(ref, t0, ts, r):
    if r == 1:
        return ref[0, t0:t0 + ts, :]
    return ref[t0:t0 + ts].reshape(ts * r, ref.shape[-1])


def _set_rows(ref, t0, ts, r, val):
    if r == 1:
        ref[0, t0:t0 + ts, :] = val
    else:
        ref[t0:t0 + ts] = val.reshape(ts, r, ref.shape[-1])


def _carry_history(ext, hist_ref, pad, n_hist, tt, n_t, r):
    j = pl.program_id(1)
    lo, hi = (pad - n_hist) * r, pad * r

    @pl.when(j == 0)
    def _():
        for c in range(N_SLABS):
            if hist_ref is None:
                ext[c, lo:hi, :] = jnp.zeros((hi - lo, LANES), ext.dtype)
            else:
                ext[c, lo:hi, :] = hist_ref[:, :, _slab(c)].reshape(hi - lo, LANES)

    if n_t > 1:
        @pl.when(j > 0)
        def _():
            for c in range(N_SLABS):
                ext[c, lo:hi, :] = ext[c, lo + tt * r:hi + tt * r, :]


def _zero_other_layers(ref, fill):
    if fill is not None:
        for o in range(fill[1]):
            if o != fill[0]:
                ref[o] = jnp.zeros(ref.shape[1:], ref.dtype)


def _emit_history(ext, new_ref, pad, n_hist, tt, n_t, r, fill):
    @pl.when(pl.program_id(1) == n_t - 1)
    def _():
        dst = new_ref if fill is None else new_ref.at[fill[0]]
        for c in range(N_SLABS):
            last = ext[c, (pad + tt - n_hist) * r:(pad + tt) * r, :]
            if r == 1:
                dst[0, :, _slab(c)] = last
            else:
                dst[:, :, _slab(c)] = last.reshape(n_hist, r, LANES)
        _zero_other_layers(new_ref, fill)


def _conv_slab(ext, w_ref, c, pad, t0, ts, n_taps, r):
    acc = None
    for k in range(n_taps):
        lo = (pad + t0 - (n_taps - 1) + k) * r
        term = ext[c, lo:lo + ts * r, :] * w_ref[k:k + 1, _slab(c)]
        acc = term if acc is None else acc + term
    return acc


def _store_slabs(ext, val, pad, t0, ts, r):
    for c in range(N_SLABS):
        ext[c, (pad + t0) * r:(pad + t0 + ts) * r, :] = val[:, _slab(c)]


def _half_out(wout_ref, y, half):
    return jnp.dot(y.astype(jnp.bfloat16), wout_ref[half * BRANCH:(half + 1) * BRANCH, :],
                   preferred_element_type=jnp.float32)


def _even_kernel(*refs, tt, ts, r, n_t, start_pos, has_hist, n_alias, fill, layer, idx):
    refs = list(refs)
    x_ref = refs.pop(0)
    ha_ref, hb_ref = (refs.pop(0), refs.pop(0)) if has_hist else (None, None)
    pre_ref, post_ref, win_ref, wout_ref, cw_ref, pw_ref, ps_ref = refs[:7]
    y_ref, na_ref, nb_ref, ext_a, ext_b = refs[7 + n_alias:]
    _carry_history(ext_a, ha_ref, PAD_A, CONV_A - 1, tt, n_t, r)
    _carry_history(ext_b, hb_ref, PAD_B, POOL_HIST, tt, n_t, r)
    subs = list(range(0, tt, ts))
    ms = ts * r

    z = {}
    for t0 in subs:
        h = _rms_norm(_get_rows(x_ref, t0, ts, r), _row(pre_ref, layer)).astype(jnp.bfloat16)
        z[t0] = jnp.dot(h, win_ref[...], preferred_element_type=jnp.float32)
    part = lambda t0, i: z[t0][:, i * BRANCH:(i + 1) * BRANCH]

    ya, yb = {}, {}
    for t0 in subs:
        _store_slabs(ext_a, part(t0, 1) * part(t0, 2), PAD_A, t0, ts, r)
        conv = jnp.concatenate([_conv_slab(ext_a, cw_ref, c, PAD_A, t0, ts, CONV_A, r) for c in range(N_SLABS)],
                               axis=-1)
        ya[t0] = part(t0, 0) * conv * _silu(part(t0, 3))
    for t0 in subs:
        p = part(t0, 4)
        _store_slabs(ext_b, p, PAD_B, t0, ts, r)
        step = lax.broadcasted_iota(jnp.int32, (ms, LANES), 0) // r
        pos = start_pos + pl.program_id(1) * tt + t0 + step
        ds = []
        for g, w in enumerate(POOL_WINDOWS):
            acc = ext_b[g, (PAD_B + t0) * r:(PAD_B + t0 + ts) * r, :]
            for i in range(1, w):
                acc = acc + ext_b[g, (PAD_B + t0 - i) * r:(PAD_B + t0 - i + ts) * r, :]
            cnt = jnp.minimum(pos + 1, w).astype(jnp.float32)
            ds.append((acc / cnt - p[:, _slab(g)]).astype(jnp.bfloat16))
        mixed = []
        zero = jnp.zeros((LANES, LANES), jnp.bfloat16)
        for g in range(0, N_SLABS, 2):
            w2 = jnp.concatenate([jnp.concatenate([pw_ref[g], zero], axis=1),
                                  jnp.concatenate([zero, pw_ref[g + 1]], axis=1)], axis=0)
            mixed.append(jnp.dot(jnp.concatenate(ds[g:g + 2], axis=1), w2, preferred_element_type=jnp.float32))
        yb[t0] = jnp.concatenate(mixed, axis=-1) * _row(ps_ref, idx) * _silu(part(t0, 5))
    for t0 in subs:
        o = _half_out(wout_ref, ya[t0], 0) + _half_out(wout_ref, yb[t0], 1)
        _set_rows(y_ref, t0, ts, r, _get_rows(x_ref, t0, ts, r) + _rms_norm(o, _row(post_ref, layer)))

    _emit_history(ext_a, na_ref, PAD_A, CONV_A - 1, tt, n_t, r, fill)
    _emit_history(ext_b, nb_ref, PAD_B, POOL_HIST, tt, n_t, r, fill)


def _mix_chunks(vn, ws_ref, bs_ref):
    row = lax.broadcasted_iota(jnp.int32, (CHUNK, CHUNK), 0)
    col = lax.broadcasted_iota(jnp.int32, (CHUNK, CHUNK), 1)
    wm = [jnp.where(col <= row, ws_ref[g], 0.0).astype(jnp.bfloat16) for g in range(N_SLABS)]
    vnb = vn.astype(jnp.bfloat16)
    n_ch = vn.shape[0] // CHUNK
    outs = []
    for g in range(N_SLABS):
        rhs = jnp.concatenate([vnb[ch * CHUNK:(ch + 1) * CHUNK, _slab(g)] for ch in range(n_ch)], axis=1)
        outs.append(jnp.dot(wm[g], rhs, preferred_element_type=jnp.float32) + bs_ref[:, g:g + 1])
    return jnp.concatenate(
        [jnp.concatenate([outs[g][:, _slab(ch)] for g in range(N_SLABS)], axis=1) for ch in range(n_ch)], axis=0)


def _mix_steps(vn, ws_ref, bs_ref, n_steps, r):
    out = []
    for i in range(n_steps):
        acc = None
        for j in range(i + 1):
            term = vn[j * r:(j + 1) * r, :] * ws_ref[i, j:j + 1, :]
            acc = term if acc is None else acc + term
        out.append(acc + bs_ref[i:i + 1, :])
    return jnp.concatenate(out, axis=0)


def _odd_kernel(*refs, tt, ts, r, n_t, has_hist, emit_vn, n_alias, fill, layer, idx):
    refs = list(refs)
    x_ref = refs.pop(0)
    hd_ref = refs.pop(0) if has_hist else None
    (pre_ref, post_ref, win_ref, wout_ref, lng_ref, lnb_ref, ws_ref, bs_ref,
     dw_ref, db_ref, cg_ref, cb_ref) = refs[:12]
    outs = refs[12 + n_alias:]
    if emit_vn:
        y_ref, nd_ref, vn_ref, ext_d, h_ref = outs
    else:
        y_ref, nd_ref, ext_d, h_ref = outs
    _carry_history(ext_d, hd_ref, PAD_D, CONV_D - 1, tt, n_t, r)
    subs = list(range(0, tt, ts))
    hrows = lambda t0: slice(t0 * r, (t0 + ts) * r)

    for t0 in subs:
        h_ref[hrows(t0), :] = _rms_norm(_get_rows(x_ref, t0, ts, r), _row(pre_ref, layer)).astype(jnp.bfloat16)

    conv_out = {t0: [] for t0 in subs}
    for c in range(N_SLABS):
        w_c = jnp.concatenate([win_ref[:, 3 * BRANCH + c * LANES:3 * BRANCH + (c + 1) * LANES],
                               win_ref[:, 4 * BRANCH + c * LANES:4 * BRANCH + (c + 1) * LANES]], axis=1)
        for t0 in subs:
            zc = jnp.dot(h_ref[hrows(t0), :], w_c, preferred_element_type=jnp.float32)
            ext_d[c, (PAD_D + t0) * r:(PAD_D + t0 + ts) * r, :] = zc[:, :LANES] * _sigmoid(zc[:, LANES:])
            conv_out[t0].append(_conv_slab(ext_d, dw_ref, c, PAD_D, t0, ts, CONV_D, r))

    z3 = {t0: jnp.dot(h_ref[hrows(t0), :], win_ref[:, :3 * BRANCH], preferred_element_type=jnp.float32)
          for t0 in subs}
    gd = {t0: jnp.dot(h_ref[hrows(t0), :], win_ref[:, 5 * BRANCH:], preferred_element_type=jnp.float32)
          for t0 in subs}

    o1, yd = {}, {}
    for t0 in subs:
        u, v, gc = (z3[t0][:, i * BRANCH:(i + 1) * BRANCH] for i in range(3))
        vn = _layer_norm(v, _row(lng_ref, idx), _row(lnb_ref, idx))
        if emit_vn:
            _set_rows(vn_ref if fill is None else vn_ref.at[fill[0]], t0, ts, r, vn)
        mixed = _mix_chunks(vn, ws_ref, bs_ref) if r == 1 else _mix_steps(vn, ws_ref, bs_ref, ts, r)
        o1[t0] = _half_out(wout_ref, u * mixed * _silu(gc), 0)
    for t0 in subs:
        y = jnp.concatenate(conv_out[t0], axis=-1) + _row(db_ref, idx)
        yd[t0] = _silu(_layer_norm(y, _row(cg_ref, idx), _row(cb_ref, idx))) * _silu(gd[t0])
    for t0 in subs:
        o = o1[t0] + _half_out(wout_ref, yd[t0], 1)
        _set_rows(y_ref, t0, ts, r, _get_rows(x_ref, t0, ts, r) + _rms_norm(o, _row(post_ref, layer)))

    if emit_vn:
        _zero_other_layers(vn_ref, fill)
    _emit_history(ext_d, nd_ref, PAD_D, CONV_D - 1, tt, n_t, r, fill)


def _layer_spec(arr, idx):
    if arr.ndim == 2:
        return pl.BlockSpec(arr.shape, lambda b, j: (0, 0), pipeline_mode=pl.Buffered(1))
    tail = arr.shape[1:]
    return pl.BlockSpec((None,) + tail, lambda b, j: (idx,) + (0,) * len(tail), pipeline_mode=pl.Buffered(1))


def _stacked_spec(rows, r, idx, n_layers, whole):
    lead = n_layers if whole else None
    l0 = 0 if whole else idx
    if r == 1:
        return pl.BlockSpec((lead, 1, rows, BRANCH), lambda b, j: (l0, b, 0, 0))
    return pl.BlockSpec((lead, rows, r, BRANCH), lambda b, j: (l0, 0, b, 0))


def _stacked_shape(n_layers, bsz, rows, r, dtype):
    shape = (n_layers, bsz, rows, BRANCH) if r == 1 else (n_layers, rows, bsz, BRANCH)
    return jax.ShapeDtypeStruct(shape, dtype)


def _x_spec(tt, r):
    if r == 1:
        return pl.BlockSpec((1, tt, D_MODEL), lambda b, j: (b, j, 0))
    return pl.BlockSpec((tt, r, D_MODEL), lambda b, j: (0, b, 0))


def _compiler_params():
    return pltpu.CompilerParams(dimension_semantics=("arbitrary", "arbitrary"),
                                vmem_limit_bytes=VMEM_LIMIT_BYTES)


def _alias_inputs(ins, in_specs, prev_new):
    aliases = {}
    for k, arr in enumerate(prev_new or ()):
        aliases[len(ins)] = 1 + k
        ins.append(arr)
        in_specs.append(pl.BlockSpec(memory_space=pl.ANY))
    return aliases


def _even_layer(x, states, params, prev_new, *, layer, idx, n_layers, tt, ts, r, start_pos):
    bsz, t = (x.shape[0], x.shape[1]) if r == 1 else (x.shape[1], x.shape[0])
    n_t = t // tt
    first = prev_new is None
    pre_g, post_g, w_in, w_out, conv_w, pool_w, pool_scale = params
    ins, in_specs = [x], [_x_spec(tt, r)]
    if states is not None:
        ins += list(states)
        in_specs += [_stacked_spec(CONV_A - 1, r, idx, n_layers, False),
                     _stacked_spec(POOL_HIST, r, idx, n_layers, False)]
    ins += [pre_g, post_g, w_in, w_out, conv_w, pool_w, pool_scale]
    in_specs += [_layer_spec(pre_g, layer), _layer_spec(post_g, layer), _layer_spec(w_in, idx),
                 _layer_spec(w_out, idx), _layer_spec(conv_w, idx), _layer_spec(pool_w, idx),
                 _layer_spec(pool_scale, idx)]
    aliases = _alias_inputs(ins, in_specs, prev_new)
    return pl.pallas_call(
        functools.partial(_even_kernel, tt=tt, ts=ts, r=r, n_t=n_t, start_pos=start_pos, has_hist=states is not None,
                          n_alias=len(aliases), fill=(idx, n_layers) if first else None, layer=layer, idx=idx),
        grid=(bsz // r, n_t),
        in_specs=in_specs,
        out_specs=[_x_spec(tt, r), _stacked_spec(CONV_A - 1, r, idx, n_layers, first),
                   _stacked_spec(POOL_HIST, r, idx, n_layers, first)],
        out_shape=[jax.ShapeDtypeStruct(x.shape, x.dtype),
                   _stacked_shape(n_layers, bsz, CONV_A - 1, r, x.dtype),
                   _stacked_shape(n_layers, bsz, POOL_HIST, r, x.dtype)],
        scratch_shapes=[pltpu.VMEM((N_SLABS, (PAD_A + tt) * r, LANES), jnp.float32),
                        pltpu.VMEM((N_SLABS, (PAD_B + tt) * r, LANES), jnp.float32)],
        input_output_aliases=aliases,
        compiler_params=_compiler_params(),
        name="even_layer",
    )(*ins)


def _odd_layer(x, state, params, prev_new, *, layer, idx, n_layers, tt, ts, r, emit_vn):
    bsz, t = (x.shape[0], x.shape[1]) if r == 1 else (x.shape[1], x.shape[0])
    n_t = t // tt
    first = prev_new is None
    ins, in_specs = [x], [_x_spec(tt, r)]
    if state is not None:
        ins.append(state)
        in_specs.append(_stacked_spec(CONV_D - 1, r, idx, n_layers, False))
    for k, arr in enumerate(params):
        ins.append(arr)
        in_specs.append(_layer_spec(arr, layer if k < 2 else idx))
    out_specs = [_x_spec(tt, r), _stacked_spec(CONV_D - 1, r, idx, n_layers, first)]
    out_shape = [jax.ShapeDtypeStruct(x.shape, x.dtype), _stacked_shape(n_layers, bsz, CONV_D - 1, r, x.dtype)]
    if emit_vn:
        out_specs.append(_stacked_spec(tt, r, idx, n_layers, first))
        out_shape.append(_stacked_shape(n_layers, bsz, t, r, x.dtype))
    aliases = _alias_inputs(ins, in_specs, prev_new)
    return pl.pallas_call(
        functools.partial(_odd_kernel, tt=tt, ts=ts, r=r, n_t=n_t, has_hist=state is not None, emit_vn=emit_vn,
                          n_alias=len(aliases), fill=(idx, n_layers) if first else None, layer=layer, idx=idx),
        grid=(bsz // r, n_t),
        in_specs=in_specs,
        out_specs=out_specs,
        out_shape=out_shape,
        scratch_shapes=[pltpu.VMEM((N_SLABS, (PAD_D + tt) * r, LANES), jnp.float32),
                        pltpu.VMEM((tt * r, D_MODEL), jnp.bfloat16)],
        input_output_aliases=aliases,
        compiler_params=_compiler_params(),
        name="odd_layer",
    )(*ins)


def kernel(x_prompt, x_sample, state_conv_a, state_pool_b, state_conv_d, norm_pre, norm_post, w_in_even, w_out_even, conv_a_w, pool_w, pool_scale, w_in_odd, w_out_odd, sgu_ln_g, sgu_ln_b, sgu_w, sgu_b, conf_dw_w, conf_dw_b, conf_ln_g, conf_ln_b):
    xp = x_prompt
    dec_seq = x_sample.shape[1]
    depth = norm_pre.shape[0]
    n_even, n_odd = w_in_even.shape[0], w_in_odd.shape[0]
    bf16 = jnp.bfloat16
    time_major = lambda a: jnp.swapaxes(a, -3, -2)

    even_params = (norm_pre, norm_post, w_in_even.astype(bf16), w_out_even.astype(bf16),
                   conv_a_w, pool_w.astype(bf16), pool_scale)
    odd_head = (norm_pre, norm_post, w_in_odd.astype(bf16), w_out_odd.astype(bf16), sgu_ln_g, sgu_ln_b)
    odd_tail = (conf_dw_w, conf_dw_b, conf_ln_g, conf_ln_b)
    odd_params_p = odd_head + (sgu_w, jnp.swapaxes(sgu_b, 1, 2)) + odd_tail
    ws_s = jnp.repeat(jnp.transpose(sgu_w[:, :, :dec_seq, :dec_seq], (0, 2, 3, 1)), LANES, axis=-1)
    bs_s = jnp.repeat(jnp.swapaxes(sgu_b[:, :, :dec_seq], 1, 2), LANES, axis=-1)
    odd_params_s = odd_head + (ws_s, bs_s) + odd_tail

    xs = time_major(x_sample)
    even_states_s = (time_major(state_conv_a), time_major(state_pool_b))
    odd_state_s = time_major(state_conv_d)
    sample = dict(tt=dec_seq, ts=dec_seq, r=SAMPLE_SEQS)
    new_even_p = new_even_s = new_odd_p = new_odd_s = None
    for l in range(depth):
        i = l // 2
        if l % 2 == 0:
            common = dict(layer=l, idx=i, n_layers=n_even)
            prompt = dict(tt=PROMPT_TILE, ts=EVEN_SUB, r=1)
            xp, *new_even_p = _even_layer(xp, None, even_params, new_even_p, start_pos=0, **common, **prompt)
            xs, *new_even_s = _even_layer(xs, even_states_s, even_params, new_even_s, start_pos=PAST_LEN,
                                          **common, **sample)
        else:
            common = dict(layer=l, idx=i, n_layers=n_odd)
            prompt = dict(tt=PROMPT_TILE, ts=ODD_SUB, r=1)
            xp, *new_odd_p = _odd_layer(xp, None, odd_params_p, new_odd_p, emit_vn=False, **common, **prompt)
            xs, *new_odd_s = _odd_layer(xs, odd_state_s, odd_params_s, new_odd_s, emit_vn=True, **common, **sample)
    ca_p, pb_p = new_even_p
    ca_s, pb_s = new_even_s
    (cd_p,) = new_odd_p
    cd_s, v_s = new_odd_s
    return (xp, time_major(xs), ca_p, time_major(ca_s), pb_p, time_major(pb_s),
            cd_p, time_major(cd_s), time_major(v_s))
```

```python
import functools

import jax
import jax.numpy as jnp
from jax import lax
from jax.experimental import pallas as pl
from jax.experimental.pallas import tpu as pltpu

D_MODEL = 1024
BRANCH = 512
LANES = 128
N_SLABS = BRANCH // LANES
IN_WIDTH = 6 * BRANCH
CONV_A = 3
POOL_WINDOWS = (2, 4, 8, 16)
POOL_HIST = max(POOL_WINDOWS) - 1
CHUNK = 128
CONV_D = 31
PAST_LEN = 16384
EPS = 1e-6

PAD_A = 8
PAD_B = 16
PAD_D = 32

VMEM_LIMIT_BYTES = 56 * 1024 * 1024

PROMPT_TILE = 1024
EVEN_SUB = 256
ODD_SUB = 512
SAMPLE_SEQS = 64


def _rms_norm(x, g):
    ms = jnp.mean(x * x, axis=-1, keepdims=True)
    return x * lax.rsqrt(ms + EPS) * g


def _layer_norm(x, g, b):
    mu = jnp.mean(x, axis=-1, keepdims=True)
    xc = x - mu
    var = jnp.mean(xc * xc, axis=-1, keepdims=True)
    return xc * lax.rsqrt(var + EPS) * g + b


def _sigmoid(x):
    return 0.5 * jnp.tanh(0.5 * x) + 0.5


def _silu(x):
    hx = 0.5 * x
    return hx * jnp.tanh(hx) + hx


def _slab(c):
    return slice(c * LANES, (c + 1) * LANES)


def _row(ref, i):
    return ref[i:i + 1, :]


def _get_rows(ref, t0, ts, r):
    if r == 1:
        return ref[0, t0:t0 + ts, :]
    return ref[t0:t0 + ts].reshape(ts * r, ref.shape[-1])


def _set_rows(ref, t0, ts, r, val):
    if r == 1:
        ref[0, t0:t0 + ts, :] = val
    else:
        ref[t0:t0 + ts] = val.reshape(ts, r, ref.shape[-1])


def _carry_history(ext, hist_ref, pad, n_hist, tt, n_t, r):
    j = pl.program_id(1)
    lo, hi = (pad - n_hist) * r, pad * r

    @pl.when(j == 0)
    def _():
        for c in range(N_SLABS):
            if hist_ref is None:
                ext[c, lo:hi, :] = jnp.zeros((hi - lo, LANES), ext.dtype)
            else:
                ext[c, lo:hi, :] = hist_ref[:, :, _slab(c)].reshape(hi - lo, LANES)

    if n_t > 1:
        @pl.when(j > 0)
        def _():
            for c in range(N_SLABS):
                ext[c, lo:hi, :] = ext[c, lo + tt * r:hi + tt * r, :]


def _zero_other_layers(ref, fill):
    if fill is not None:
        for o in range(fill[1]):
            if o != fill[0]:
                ref[o] = jnp.zeros(ref.shape[1:], ref.dtype)


def _emit_history(ext, new_ref, pad, n_hist, tt, n_t, r, fill):
    @pl.when(pl.program_id(1) == n_t - 1)
    def _():
        dst = new_ref if fill is None else new_ref.at[fill[0]]
        for c in range(N_SLABS):
            last = ext[c, (pad + tt - n_hist) * r:(pad + tt) * r, :]
            if r == 1:
                dst[0, :, _slab(c)] = last
            else:
                dst[:, :, _slab(c)] = last.reshape(n_hist, r, LANES)
        _zero_other_layers(new_ref, fill)


def _conv_slab(ext, w_ref, c, pad, t0, ts, n_taps, r):
    acc = None
    for k in range(n_taps):
        lo = (pad + t0 - (n_taps - 1) + k) * r
        term = ext[c, lo:lo + ts * r, :] * w_ref[k:k + 1, _slab(c)]
        acc = term if acc is None else acc + term
    return acc


def _store_slabs(ext, val, pad, t0, ts, r):
    for c in range(N_SLABS):
        ext[c, (pad + t0) * r:(pad + t0 + ts) * r, :] = val[:, _slab(c)]


def _half_out(wout_ref, y, half):
    return jnp.dot(y.astype(jnp.bfloat16), wout_ref[half * BRANCH:(half + 1) * BRANCH, :],
                   preferred_element_type=jnp.float32)


def _even_kernel(*refs, tt, ts, r, n_t, start_pos, has_hist, n_alias, fill, layer, idx):
    refs = list(refs)
    x_ref = refs.pop(0)
    ha_ref, hb_ref = (refs.pop(0), refs.pop(0)) if has_hist else (None, None)
    pre_ref, post_ref, win_ref, wout_ref, cw_ref, pw_ref, ps_ref = refs[:7]
    y_ref, na_ref, nb_ref, ext_a, ext_b = refs[7 + n_alias:]
    _carry_history(ext_a, ha_ref, PAD_A, CONV_A - 1, tt, n_t, r)
    _carry_history(ext_b, hb_ref, PAD_B, POOL_HIST, tt, n_t, r)
    subs = list(range(0, tt, ts))
    ms = ts * r

    z = {}
    for t0 in subs:
        h = _rms_norm(_get_rows(x_ref, t0, ts, r), _row(pre_ref, layer)).astype(jnp.bfloat16)
        z[t0] = jnp.dot(h, win_ref[...], preferred_element_type=jnp.float32)
    part = lambda t0, i: z[t0][:, i * BRANCH:(i + 1) * BRANCH]

    ya, yb = {}, {}
    for t0 in subs:
        _store_slabs(ext_a, part(t0, 1) * part(t0, 2), PAD_A, t0, ts, r)
        conv = jnp.concatenate([_conv_slab(ext_a, cw_ref, c, PAD_A, t0, ts, CONV_A, r) for c in range(N_SLABS)],
                               axis=-1)
        ya[t0] = part(t0, 0) * conv * _silu(part(t0, 3))
    for t0 in subs:
        p = part(t0, 4)
        _store_slabs(ext_b, p, PAD_B, t0, ts, r)
        step = lax.broadcasted_iota(jnp.int32, (ms, LANES), 0) // r
        pos = start_pos + pl.program_id(1) * tt + t0 + step
        ds = []
        for g, w in enumerate(POOL_WINDOWS):
            acc = ext_b[g, (PAD_B + t0) * r:(PAD_B + t0 + ts) * r, :]
            for i in range(1, w):
                acc = acc + ext_b[g, (PAD_B + t0 - i) * r:(PAD_B + t0 - i + ts) * r, :]
            cnt = jnp.minimum(pos + 1, w).astype(jnp.float32)
            ds.append((acc / cnt - p[:, _slab(g)]).astype(jnp.bfloat16))
        mixed = []
        zero = jnp.zeros((LANES, LANES), jnp.bfloat16)
        for g in range(0, N_SLABS, 2):
            w2 = jnp.concatenate([jnp.concatenate([pw_ref[g], zero], axis=1),
                                  jnp.concatenate([zero, pw_ref[g + 1]], axis=1)], axis=0)
            mixed.append(jnp.dot(jnp.concatenate(ds[g:g + 2], axis=1), w2, preferred_element_type=jnp.float32))
        yb[t0] = jnp.concatenate(mixed, axis=-1) * _row(ps_ref, idx) * _silu(part(t0, 5))
    for t0 in subs:
        o = _half_out(wout_ref, ya[t0], 0) + _half_out(wout_ref, yb[t0], 1)
        _set_rows(y_ref, t0, ts, r, _get_rows(x_ref, t0, ts, r) + _rms_norm(o, _row(post_ref, layer)))

    _emit_history(ext_a, na_ref, PAD_A, CONV_A - 1, tt, n_t, r, fill)
    _emit_history(ext_b, nb_ref, PAD_B, POOL_HIST, tt, n_t, r, fill)


def _mix_chunks(vn, ws_ref, bs_ref):
    row = lax.broadcasted_iota(jnp.int32, (CHUNK, CHUNK), 0)
    col = lax.broadcasted_iota(jnp.int32, (CHUNK, CHUNK), 1)
    wm = [jnp.where(col <= row, ws_ref[g], 0.0).astype(jnp.bfloat16) for g in range(N_SLABS)]
    vnb = vn.astype(jnp.bfloat16)
    n_ch = vn.shape[0] // CHUNK
    outs = []
    for g in range(N_SLABS):
        rhs = jnp.concatenate([vnb[ch * CHUNK:(ch + 1) * CHUNK, _slab(g)] for ch in range(n_ch)], axis=1)
        outs.append(jnp.dot(wm[g], rhs, preferred_element_type=jnp.float32) + bs_ref[:, g:g + 1])
    return jnp.concatenate(
        [jnp.concatenate([outs[g][:, _slab(ch)] for g in range(N_SLABS)], axis=1) for ch in range(n_ch)], axis=0)


def _mix_steps(vn, ws_ref, bs_ref, n_steps, r):
    out = []
    for i in range(n_steps):
        acc = None
        for j in range(i + 1):
            term = vn[j * r:(j + 1) * r, :] * ws_ref[i, j:j + 1, :]
            acc = term if acc is None else acc + term
        out.append(acc + bs_ref[i:i + 1, :])
    return jnp.concatenate(out, axis=0)


def _odd_kernel(*refs, tt, ts, r, n_t, has_hist, emit_vn, n_alias, fill, layer, idx):
    refs = list(refs)
    x_ref = refs.pop(0)
    hd_ref = refs.pop(0) if has_hist else None
    (pre_ref, post_ref, win_ref, wout_ref, lng_ref, lnb_ref, ws_ref, bs_ref,
     dw_ref, db_ref, cg_ref, cb_ref) = refs[:12]
    outs = refs[12 + n_alias:]
    if emit_vn:
        y_ref, nd_ref, vn_ref, ext_d, h_ref = outs
    else:
        y_ref, nd_ref, ext_d, h_ref = outs
    _carry_history(ext_d, hd_ref, PAD_D, CONV_D - 1, tt, n_t, r)
    subs = list(range(0, tt, ts))
    hrows = lambda t0: slice(t0 * r, (t0 + ts) * r)

    for t0 in subs:
        h_ref[hrows(t0), :] = _rms_norm(_get_rows(x_ref, t0, ts, r), _row(pre_ref, layer)).astype(jnp.bfloat16)

    conv_out = {t0: [] for t0 in subs}
    for c in range(N_SLABS):
        w_c = jnp.concatenate([win_ref[:, 3 * BRANCH + c * LANES:3 * BRANCH + (c + 1) * LANES],
                               win_ref[:, 4 * BRANCH + c * LANES:4 * BRANCH + (c + 1) * LANES]], axis=1)
        for t0 in subs:
            zc = jnp.dot(h_ref[hrows(t0), :], w_c, preferred_element_type=jnp.float32)
            ext_d[c, (PAD_D + t0) * r:(PAD_D + t0 + ts) * r, :] = zc[:, :LANES] * _sigmoid(zc[:, LANES:])
            conv_out[t0].append(_conv_slab(ext_d, dw_ref, c, PAD_D, t0, ts, CONV_D, r))

    z3 = {t0: jnp.dot(h_ref[hrows(t0), :], win_ref[:, :3 * BRANCH], preferred_element_type=jnp.float32)
          for t0 in subs}
    gd = {t0: jnp.dot(h_ref[hrows(t0), :], win_ref[:, 5 * BRANCH:], preferred_element_type=jnp.float32)
          for t0 in subs}

    o1, yd = {}, {}
    for t0 in subs:
        u, v, gc = (z3[t0][:, i * BRANCH:(i + 1) * BRANCH] for i in range(3))
        vn = _layer_norm(v, _row(lng_ref, idx), _row(lnb_ref, idx))
        if emit_vn:
            _set_rows(vn_ref if fill is None else vn_ref.at[fill[0]], t0, ts, r, vn)
        mixed = _mix_chunks(vn, ws_ref, bs_ref) if r == 1 else _mix_steps(vn, ws_ref, bs_ref, ts, r)
        o1[t0] = _half_out(wout_ref, u * mixed * _silu(gc), 0)
    for t0 in subs:
        y = jnp.concatenate(conv_out[t0], axis=-1) + _row(db_ref, idx)
        yd[t0] = _silu(_layer_norm(y, _row(cg_ref, idx), _row(cb_ref, idx))) * _silu(gd[t0])
    for t0 in subs:
        o = o1[t0] + _half_out(wout_ref, yd[t0], 1)
        _set_rows(y_ref, t0, ts, r, _get_rows(x_ref, t0, ts, r) + _rms_norm(o, _row(post_ref, layer)))

    if emit_vn:
        _zero_other_layers(vn_ref, fill)
    _emit_history(ext_d, nd_ref, PAD_D, CONV_D - 1, tt, n_t, r, fill)


def _chunk_copies(w, bf, stage, obuf, sems, kind, idx, k):
    rows = stage.shape[1]
    sl = pl.ds(pl.multiple_of(k * rows, rows), rows)
    slot = k % 2
    fetch = pltpu.make_async_copy(w.at[idx, sl, :], stage.at[slot], sems.at[2 * kind, slot])
    send = pltpu.make_async_copy(obuf.at[slot], bf.at[sl, :], sems.at[2 * kind + 1, slot])
    return fetch, send


def _cast_next_weights(nw_in, nw_out, bf_in, bf_out, stage_in, stage_out, obuf_in, obuf_out, sems, *, idx, n_steps):
    s = pl.program_id(0) * pl.num_programs(1) + pl.program_id(1)
    pairs = ((nw_in, bf_in, stage_in, obuf_in, 0), (nw_out, bf_out, stage_out, obuf_out, 1))
    copies = lambda k: [_chunk_copies(w, bf, st, ob, sems, kind, idx, k) for w, bf, st, ob, kind in pairs]

    @pl.when(s == 0)
    def _():
        for fetch, _ in copies(s):
            fetch.start()

    for fetch, _ in copies(s):
        fetch.wait()

    @pl.when(s >= 2)
    def _():
        for _, send in copies(s - 2):
            send.wait()

    slot = s % 2
    obuf_in[slot] = stage_in[slot].astype(jnp.bfloat16)
    obuf_out[slot] = stage_out[slot].astype(jnp.bfloat16)
    for _, send in copies(s):
        send.start()

    @pl.when(s + 1 < n_steps)
    def _():
        for fetch, _ in copies(s + 1):
            fetch.start()

    @pl.when(s == n_steps - 1)
    def _():
        for k in (s - 1, s):
            for _, send in copies(k):
                send.wait()


def _with_next_weights(body, n_in, n_out, n_scratch, idx, n_steps, *refs):
    ins, rest = refs[:n_in], refs[n_in:]
    nw, rest = rest[:2], rest[2:]
    outs, rest = rest[:n_out], rest[n_out:]
    bf, rest = rest[:2], rest[2:]
    scratch, staging = rest[:n_scratch], rest[n_scratch:]
    _cast_next_weights(*nw, *bf, *staging, idx=idx, n_steps=n_steps)
    body(*ins, *outs, *scratch)


def _launch(name, body, grid, ins, in_specs, out_specs, out_shape, scratch, aliases, next_w):
    if next_w is not None:
        nw_in, nw_out, nidx = next_w
        n_steps = grid[0] * grid[1]
        rows = nw_in.shape[1] // n_steps
        assert nw_in.shape[1] == nw_out.shape[1] == rows * n_steps and rows % 16 == 0
        any_spec = pl.BlockSpec(memory_space=pl.ANY)
        body = functools.partial(_with_next_weights, body, len(ins), len(out_specs), len(scratch), nidx, n_steps)
        ins = ins + [nw_in, nw_out]
        in_specs = in_specs + [any_spec, any_spec]
        out_specs = out_specs + [any_spec, any_spec]
        out_shape = out_shape + [jax.ShapeDtypeStruct(w.shape[1:], jnp.bfloat16) for w in (nw_in, nw_out)]
        scratch = scratch + [pltpu.VMEM((2, rows, nw_in.shape[2]), jnp.float32),
                             pltpu.VMEM((2, rows, nw_out.shape[2]), jnp.float32),
                             pltpu.VMEM((2, rows, nw_in.shape[2]), jnp.bfloat16),
                             pltpu.VMEM((2, rows, nw_out.shape[2]), jnp.bfloat16),
                             pltpu.SemaphoreType.DMA((4, 2))]
    return pl.pallas_call(body, grid=grid, in_specs=in_specs, out_specs=out_specs, out_shape=out_shape,
                          scratch_shapes=scratch, input_output_aliases=aliases,
                          compiler_params=_compiler_params(), name=name)(*ins)


def _layer_spec(arr, idx):
    if arr.ndim == 2:
        return pl.BlockSpec(arr.shape, lambda b, j: (0, 0), pipeline_mode=pl.Buffered(1))
    tail = arr.shape[1:]
    return pl.BlockSpec((None,) + tail, lambda b, j: (idx,) + (0,) * len(tail), pipeline_mode=pl.Buffered(1))


def _stacked_spec(rows, r, idx, n_layers, whole):
    lead = n_layers if whole else None
    l0 = 0 if whole else idx
    if r == 1:
        return pl.BlockSpec((lead, 1, rows, BRANCH), lambda b, j: (l0, b, 0, 0))
    return pl.BlockSpec((lead, rows, r, BRANCH), lambda b, j: (l0, 0, b, 0))


def _stacked_shape(n_layers, bsz, rows, r, dtype):
    shape = (n_layers, bsz, rows, BRANCH) if r == 1 else (n_layers, rows, bsz, BRANCH)
    return jax.ShapeDtypeStruct(shape, dtype)


def _x_spec(tt, r):
    if r == 1:
        return pl.BlockSpec((1, tt, D_MODEL), lambda b, j: (b, j, 0))
    return pl.BlockSpec((tt, r, D_MODEL), lambda b, j: (0, b, 0))


def _compiler_params():
    return pltpu.CompilerParams(dimension_semantics=("arbitrary", "arbitrary"),
                                vmem_limit_bytes=VMEM_LIMIT_BYTES)


def _alias_inputs(ins, in_specs, prev_new):
    aliases = {}
    for k, arr in enumerate(prev_new or ()):
        aliases[len(ins)] = 1 + k
        ins.append(arr)
        in_specs.append(pl.BlockSpec(memory_space=pl.ANY))
    return aliases


def _even_layer(x, states, params, prev_new, *, layer, idx, n_layers, tt, ts, r, start_pos, next_w=None):
    bsz, t = (x.shape[0], x.shape[1]) if r == 1 else (x.shape[1], x.shape[0])
    n_t = t // tt
    first = prev_new is None
    pre_g, post_g, w_in, w_out, conv_w, pool_w, pool_scale = params
    ins, in_specs = [x], [_x_spec(tt, r)]
    if states is not None:
        ins += list(states)
        in_specs += [_stacked_spec(CONV_A - 1, r, idx, n_layers, False),
                     _stacked_spec(POOL_HIST, r, idx, n_layers, False)]
    ins += [pre_g, post_g, w_in, w_out, conv_w, pool_w, pool_scale]
    in_specs += [_layer_spec(pre_g, layer), _layer_spec(post_g, layer), _layer_spec(w_in, idx),
                 _layer_spec(w_out, idx), _layer_spec(conv_w, idx), _layer_spec(pool_w, idx),
                 _layer_spec(pool_scale, idx)]
    aliases = _alias_inputs(ins, in_specs, prev_new)
    body = functools.partial(_even_kernel, tt=tt, ts=ts, r=r, n_t=n_t, start_pos=start_pos, has_hist=states is not None,
                             n_alias=len(aliases), fill=(idx, n_layers) if first else None, layer=layer, idx=idx)
    out_specs = [_x_spec(tt, r), _stacked_spec(CONV_A - 1, r, idx, n_layers, first),
                 _stacked_spec(POOL_HIST, r, idx, n_layers, first)]
    out_shape = [jax.ShapeDtypeStruct(x.shape, x.dtype),
                 _stacked_shape(n_layers, bsz, CONV_A - 1, r, x.dtype),
                 _stacked_shape(n_layers, bsz, POOL_HIST, r, x.dtype)]
    scratch = [pltpu.VMEM((N_SLABS, (PAD_A + tt) * r, LANES), jnp.float32),
               pltpu.VMEM((N_SLABS, (PAD_B + tt) * r, LANES), jnp.float32)]
    return _launch("even_layer", body, (bsz // r, n_t), ins, in_specs, out_specs, out_shape, scratch, aliases, next_w)


def _odd_layer(x, state, params, prev_new, *, layer, idx, n_layers, tt, ts, r, emit_vn, next_w=None):
    bsz, t = (x.shape[0], x.shape[1]) if r == 1 else (x.shape[1], x.shape[0])
    n_t = t // tt
    first = prev_new is None
    ins, in_specs = [x], [_x_spec(tt, r)]
    if state is not None:
        ins.append(state)
        in_specs.append(_stacked_spec(CONV_D - 1, r, idx, n_layers, False))
    for k, arr in enumerate(params):
        ins.append(arr)
        in_specs.append(_layer_spec(arr, layer if k < 2 else idx))
    out_specs = [_x_spec(tt, r), _stacked_spec(CONV_D - 1, r, idx, n_layers, first)]
    out_shape = [jax.ShapeDtypeStruct(x.shape, x.dtype), _stacked_shape(n_layers, bsz, CONV_D - 1, r, x.dtype)]
    if emit_vn:
        out_specs.append(_stacked_spec(tt, r, idx, n_layers, first))
        out_shape.append(_stacked_shape(n_layers, bsz, t, r, x.dtype))
    aliases = _alias_inputs(ins, in_specs, prev_new)
    body = functools.partial(_odd_kernel, tt=tt, ts=ts, r=r, n_t=n_t, has_hist=state is not None, emit_vn=emit_vn,
                             n_alias=len(aliases), fill=(idx, n_layers) if first else None, layer=layer, idx=idx)
    scratch = [pltpu.VMEM((N_SLABS, (PAD_D + tt) * r, LANES), jnp.float32),
               pltpu.VMEM((tt * r, D_MODEL), jnp.bfloat16)]
    return _launch("odd_layer", body, (bsz // r, n_t), ins, in_specs, out_specs, out_shape, scratch, aliases, next_w)


def kernel(x_prompt, x_sample, state_conv_a, state_pool_b, state_conv_d, norm_pre, norm_post, w_in_even, w_out_even, conv_a_w, pool_w, pool_scale, w_in_odd, w_out_odd, sgu_ln_g, sgu_ln_b, sgu_w, sgu_b, conf_dw_w, conf_dw_b, conf_ln_g, conf_ln_b):
    xp = x_prompt
    dec_seq = x_sample.shape[1]
    depth = norm_pre.shape[0]
    n_even, n_odd = w_in_even.shape[0], w_in_odd.shape[0]
    bf16 = jnp.bfloat16
    time_major = lambda a: jnp.swapaxes(a, -3, -2)

    even_tail = (conv_a_w, pool_w.astype(bf16), pool_scale)
    odd_tail = (conf_dw_w, conf_dw_b, conf_ln_g, conf_ln_b)
    mix_p = (sgu_w, jnp.swapaxes(sgu_b, 1, 2))
    mix_s = (jnp.repeat(jnp.transpose(sgu_w[:, :, :dec_seq, :dec_seq], (0, 2, 3, 1)), LANES, axis=-1),
             jnp.repeat(jnp.swapaxes(sgu_b[:, :, :dec_seq], 1, 2), LANES, axis=-1))

    xs = time_major(x_sample)
    even_states_s = (time_major(state_conv_a), time_major(state_pool_b))
    odd_state_s = time_major(state_conv_d)
    sample = dict(tt=dec_seq, ts=dec_seq, r=SAMPLE_SEQS)
    new_even_p = new_even_s = new_odd_p = new_odd_s = None
    w_in_bf, w_out_bf = w_in_even[0].astype(bf16), w_out_even[0].astype(bf16)
    for l in range(depth):
        i = l // 2
        stacks = (w_in_odd, w_out_odd, i) if l % 2 == 0 else (w_in_even, w_out_even, i + 1)
        next_w = stacks if l + 1 < depth else None
        if l % 2 == 0:
            common = dict(layer=l, idx=i, n_layers=n_even)
            params = (norm_pre, norm_post, w_in_bf, w_out_bf) + even_tail
            outs = _even_layer(xp, None, params, new_even_p, start_pos=0, tt=PROMPT_TILE, ts=EVEN_SUB, r=1,
                               next_w=next_w, **common)
            xp, new_even_p = outs[0], outs[1:3]
            xs, *new_even_s = _even_layer(xs, even_states_s, params, new_even_s, start_pos=PAST_LEN,
                                          **common, **sample)
        else:
            common = dict(layer=l, idx=i, n_layers=n_odd)
            head = (norm_pre, norm_post, w_in_bf, w_out_bf, sgu_ln_g, sgu_ln_b)
            outs = _odd_layer(xp, None, head + mix_p + odd_tail, new_odd_p, emit_vn=False, tt=PROMPT_TILE, ts=ODD_SUB,
                              r=1, next_w=next_w, **common)
            xp, new_odd_p = outs[0], outs[1:2]
            xs, *new_odd_s = _odd_layer(xs, odd_state_s, head + mix_s + odd_tail, new_odd_s, emit_vn=True,
                                        **common, **sample)
        if next_w is not None:
            w_in_bf, w_out_bf = outs[-2:]
    ca_p, pb_p = new_even_p
    ca_s, pb_s = new_even_s
    (cd_p,) = new_odd_p
    cd_s, v_s = new_odd_s
    return (xp, time_major(xs), ca_p, time_major(ca_s), pb_p, time_major(pb_s),
            cd_p, time_major(cd_s), time_major(v_s))
```

```python
import functools

import jax
import jax.numpy as jnp
from jax import lax
from jax.experimental import pallas as pl
from jax.experimental.pallas import tpu as pltpu

D_MODEL = 1024
BRANCH = 512
LANES = 128
N_SLABS = BRANCH // LANES
IN_WIDTH = 6 * BRANCH
CONV_A = 3
POOL_WINDOWS = (2, 4, 8, 16)
POOL_HIST = max(POOL_WINDOWS) - 1
CHUNK = 128
CONV_D = 31
PAST_LEN = 16384
EPS = 1e-6

PAD_A = 8
PAD_B = 16
PAD_D = 32

VMEM_LIMIT_BYTES = 56 * 1024 * 1024

PROMPT_TILE = 1024
EVEN_SUB = 256
ODD_SUB = 512
SAMPLE_SEQS = 32


def _rms_norm(x, g):
    ms = jnp.mean(x * x, axis=-1, keepdims=True)
    return x * lax.rsqrt(ms + EPS) * g


def _layer_norm(x, g, b):
    mu = jnp.mean(x, axis=-1, keepdims=True)
    xc = x - mu
    var = jnp.mean(xc * xc, axis=-1, keepdims=True)
    return xc * lax.rsqrt(var + EPS) * g + b


def _sigmoid(x):
    return 0.5 * jnp.tanh(0.5 * x) + 0.5


def _silu(x):
    hx = 0.5 * x
    return hx * jnp.tanh(hx) + hx


def _slab(c):
    return slice(c * LANES, (c + 1) * LANES)


def _row(ref, i):
    return ref[i:i + 1, :]


def _get_rows(ref, t0, ts, r):
    if r == 1:
        return ref[0, t0:t0 + ts, :]
    return ref[t0:t0 + ts].reshape(ts * r, ref.shape[-1])


def _set_rows(ref, t0, ts, r, val):
    if r == 1:
        ref[0, t0:t0 + ts, :] = val
    else:
        ref[t0:t0 + ts] = val.reshape(ts, r, ref.shape[-1])


def _carry_history(ext, hist_ref, pad, n_hist, tt, n_t, r):
    j = pl.program_id(1)
    lo, hi = (pad - n_hist) * r, pad * r

    @pl.when(j == 0)
    def _():
        for c in range(N_SLABS):
            if hist_ref is None:
                ext[c, lo:hi, :] = jnp.zeros((hi - lo, LANES), ext.dtype)
            else:
                ext[c, lo:hi, :] = hist_ref[:, :, _slab(c)].reshape(hi - lo, LANES)

    if n_t > 1:
        @pl.when(j > 0)
        def _():
            for c in range(N_SLABS):
                ext[c, lo:hi, :] = ext[c, lo + tt * r:hi + tt * r, :]


def _zero_other_layers(ref, fill):
    if fill is not None:
        for o in range(fill[1]):
            if o != fill[0]:
                ref[o] = jnp.zeros(ref.shape[1:], ref.dtype)


def _emit_history(ext, new_ref, pad, n_hist, tt, n_t, r, fill):
    @pl.when(pl.program_id(1) == n_t - 1)
    def _():
        dst = new_ref if fill is None else new_ref.at[fill[0]]
        for c in range(N_SLABS):
            last = ext[c, (pad + tt - n_hist) * r:(pad + tt) * r, :]
            if r == 1:
                dst[0, :, _slab(c)] = last
            else:
                dst[:, :, _slab(c)] = last.reshape(n_hist, r, LANES)
        _zero_other_layers(new_ref, fill)


def _conv_slab(ext, w_ref, c, pad, t0, ts, n_taps, r):
    acc = None
    for k in range(n_taps):
        lo = (pad + t0 - (n_taps - 1) + k) * r
        term = ext[c, lo:lo + ts * r, :] * w_ref[k:k + 1, _slab(c)]
        acc = term if acc is None else acc + term
    return acc


def _store_slabs(ext, val, pad, t0, ts, r):
    for c in range(N_SLABS):
        ext[c, (pad + t0) * r:(pad + t0 + ts) * r, :] = val[:, _slab(c)]


def _half_out(wout_ref, y, half):
    return jnp.dot(y.astype(jnp.bfloat16), wout_ref[half * BRANCH:(half + 1) * BRANCH, :],
                   preferred_element_type=jnp.float32)


def _even_kernel(*refs, tt, ts, r, n_t, start_pos, has_hist, n_alias, fill, layer, idx):
    refs = list(refs)
    x_ref = refs.pop(0)
    ha_ref, hb_ref = (refs.pop(0), refs.pop(0)) if has_hist else (None, None)
    pre_ref, post_ref, win_ref, wout_ref, cw_ref, pw_ref, ps_ref = refs[:7]
    y_ref, na_ref, nb_ref, ext_a, ext_b = refs[7 + n_alias:]
    _carry_history(ext_a, ha_ref, PAD_A, CONV_A - 1, tt, n_t, r)
    _carry_history(ext_b, hb_ref, PAD_B, POOL_HIST, tt, n_t, r)
    subs = list(range(0, tt, ts))
    ms = ts * r

    z = {}
    for t0 in subs:
        h = _rms_norm(_get_rows(x_ref, t0, ts, r), _row(pre_ref, layer)).astype(jnp.bfloat16)
        z[t0] = jnp.dot(h, win_ref[...], preferred_element_type=jnp.float32)
    part = lambda t0, i: z[t0][:, i * BRANCH:(i + 1) * BRANCH]

    ya, yb = {}, {}
    for t0 in subs:
        _store_slabs(ext_a, part(t0, 1) * part(t0, 2), PAD_A, t0, ts, r)
        conv = jnp.concatenate([_conv_slab(ext_a, cw_ref, c, PAD_A, t0, ts, CONV_A, r) for c in range(N_SLABS)],
                               axis=-1)
        ya[t0] = part(t0, 0) * conv * _silu(part(t0, 3))
    for t0 in subs:
        p = part(t0, 4)
        _store_slabs(ext_b, p, PAD_B, t0, ts, r)
        step = lax.broadcasted_iota(jnp.int32, (ms, LANES), 0) // r
        pos = start_pos + pl.program_id(1) * tt + t0 + step
        ds = []
        for g, w in enumerate(POOL_WINDOWS):
            acc = ext_b[g, (PAD_B + t0) * r:(PAD_B + t0 + ts) * r, :]
            for i in range(1, w):
                acc = acc + ext_b[g, (PAD_B + t0 - i) * r:(PAD_B + t0 - i + ts) * r, :]
            cnt = jnp.minimum(pos + 1, w).astype(jnp.float32)
            ds.append((acc / cnt - p[:, _slab(g)]).astype(jnp.bfloat16))
        mixed = []
        zero = jnp.zeros((LANES, LANES), jnp.bfloat16)
        for g in range(0, N_SLABS, 2):
            w2 = jnp.concatenate([jnp.concatenate([pw_ref[g], zero], axis=1),
                                  jnp.concatenate([zero, pw_ref[g + 1]], axis=1)], axis=0)
            mixed.append(jnp.dot(jnp.concatenate(ds[g:g + 2], axis=1), w2, preferred_element_type=jnp.float32))
        yb[t0] = jnp.concatenate(mixed, axis=-1) * _row(ps_ref, idx) * _silu(part(t0, 5))
    for t0 in subs:
        o = _half_out(wout_ref, ya[t0], 0) + _half_out(wout_ref, yb[t0], 1)
        _set_rows(y_ref, t0, ts, r, _get_rows(x_ref, t0, ts, r) + _rms_norm(o, _row(post_ref, layer)))

    _emit_history(ext_a, na_ref, PAD_A, CONV_A - 1, tt, n_t, r, fill)
    _emit_history(ext_b, nb_ref, PAD_B, POOL_HIST, tt, n_t, r, fill)


def _mix_chunks(vn, ws_ref, bs_ref):
    row = lax.broadcasted_iota(jnp.int32, (CHUNK, CHUNK), 0)
    col = lax.broadcasted_iota(jnp.int32, (CHUNK, CHUNK), 1)
    wm = [jnp.where(col <= row, ws_ref[g], 0.0).astype(jnp.bfloat16) for g in range(N_SLABS)]
    vnb = vn.astype(jnp.bfloat16)
    n_ch = vn.shape[0] // CHUNK
    outs = []
    for g in range(N_SLABS):
        rhs = jnp.concatenate([vnb[ch * CHUNK:(ch + 1) * CHUNK, _slab(g)] for ch in range(n_ch)], axis=1)
        outs.append(jnp.dot(wm[g], rhs, preferred_element_type=jnp.float32) + bs_ref[:, g:g + 1])
    return jnp.concatenate(
        [jnp.concatenate([outs[g][:, _slab(ch)] for g in range(N_SLABS)], axis=1) for ch in range(n_ch)], axis=0)


def _mix_steps(vn, ws_ref, bs_ref, n_steps, r):
    out = []
    for i in range(n_steps):
        acc = None
        for j in range(i + 1):
            term = vn[j * r:(j + 1) * r, :] * ws_ref[i, j:j + 1, :]
            acc = term if acc is None else acc + term
        out.append(acc + bs_ref[i:i + 1, :])
    return jnp.concatenate(out, axis=0)


def _odd_kernel(*refs, tt, ts, r, n_t, has_hist, emit_vn, n_alias, fill, layer, idx):
    refs = list(refs)
    x_ref = refs.pop(0)
    hd_ref = refs.pop(0) if has_hist else None
    (pre_ref, post_ref, win_ref, wout_ref, lng_ref, lnb_ref, ws_ref, bs_ref,
     dw_ref, db_ref, cg_ref, cb_ref) = refs[:12]
    outs = refs[12 + n_alias:]
    if emit_vn:
        y_ref, nd_ref, vn_ref, ext_d, h_ref = outs
    else:
        y_ref, nd_ref, ext_d, h_ref = outs
    _carry_history(ext_d, hd_ref, PAD_D, CONV_D - 1, tt, n_t, r)
    subs = list(range(0, tt, ts))
    hrows = lambda t0: slice(t0 * r, (t0 + ts) * r)

    for t0 in subs:
        h_ref[hrows(t0), :] = _rms_norm(_get_rows(x_ref, t0, ts, r), _row(pre_ref, layer)).astype(jnp.bfloat16)

    conv_out = {t0: [] for t0 in subs}
    for c in range(N_SLABS):
        w_c = jnp.concatenate([win_ref[:, 3 * BRANCH + c * LANES:3 * BRANCH + (c + 1) * LANES],
                               win_ref[:, 4 * BRANCH + c * LANES:4 * BRANCH + (c + 1) * LANES]], axis=1)
        for t0 in subs:
            zc = jnp.dot(h_ref[hrows(t0), :], w_c, preferred_element_type=jnp.float32)
            ext_d[c, (PAD_D + t0) * r:(PAD_D + t0 + ts) * r, :] = zc[:, :LANES] * _sigmoid(zc[:, LANES:])
            conv_out[t0].append(_conv_slab(ext_d, dw_ref, c, PAD_D, t0, ts, CONV_D, r))

    z3 = {t0: jnp.dot(h_ref[hrows(t0), :], win_ref[:, :3 * BRANCH], preferred_element_type=jnp.float32)
          for t0 in subs}
    gd = {t0: jnp.dot(h_ref[hrows(t0), :], win_ref[:, 5 * BRANCH:], preferred_element_type=jnp.float32)
          for t0 in subs}

    o1, yd = {}, {}
    for t0 in subs:
        u, v, gc = (z3[t0][:, i * BRANCH:(i + 1) * BRANCH] for i in range(3))
        vn = _layer_norm(v, _row(lng_ref, idx), _row(lnb_ref, idx))
        if emit_vn:
            _set_rows(vn_ref if fill is None else vn_ref.at[fill[0]], t0, ts, r, vn)
        mixed = _mix_chunks(vn, ws_ref, bs_ref) if r == 1 else _mix_steps(vn, ws_ref, bs_ref, ts, r)
        o1[t0] = _half_out(wout_ref, u * mixed * _silu(gc), 0)
    for t0 in subs:
        y = jnp.concatenate(conv_out[t0], axis=-1) + _row(db_ref, idx)
        yd[t0] = _silu(_layer_norm(y, _row(cg_ref, idx), _row(cb_ref, idx))) * _silu(gd[t0])
    for t0 in subs:
        o = o1[t0] + _half_out(wout_ref, yd[t0], 1)
        _set_rows(y_ref, t0, ts, r, _get_rows(x_ref, t0, ts, r) + _rms_norm(o, _row(post_ref, layer)))

    if emit_vn:
        _zero_other_layers(vn_ref, fill)
    _emit_history(ext_d, nd_ref, PAD_D, CONV_D - 1, tt, n_t, r, fill)


def _chunk_copies(w, bf, stage, obuf, sems, kind, idx, k):
    rows = stage.shape[1]
    sl = pl.ds(pl.multiple_of(k * rows, rows), rows)
    slot = k % 2
    fetch = pltpu.make_async_copy(w.at[idx, sl, :], stage.at[slot], sems.at[2 * kind, slot])
    send = pltpu.make_async_copy(obuf.at[slot], bf.at[sl, :], sems.at[2 * kind + 1, slot])
    return fetch, send


def _cast_next_weights(nw_in, nw_out, bf_in, bf_out, stage_in, stage_out, obuf_in, obuf_out, sems, *, idx, n_steps):
    s = pl.program_id(0) * pl.num_programs(1) + pl.program_id(1)
    pairs = ((nw_in, bf_in, stage_in, obuf_in, 0), (nw_out, bf_out, stage_out, obuf_out, 1))
    copies = lambda k: [_chunk_copies(w, bf, st, ob, sems, kind, idx, k) for w, bf, st, ob, kind in pairs]

    @pl.when(s == 0)
    def _():
        for fetch, _ in copies(s):
            fetch.start()

    for fetch, _ in copies(s):
        fetch.wait()

    @pl.when(s >= 2)
    def _():
        for _, send in copies(s - 2):
            send.wait()

    slot = s % 2
    obuf_in[slot] = stage_in[slot].astype(jnp.bfloat16)
    obuf_out[slot] = stage_out[slot].astype(jnp.bfloat16)
    for _, send in copies(s):
        send.start()

    @pl.when(s + 1 < n_steps)
    def _():
        for fetch, _ in copies(s + 1):
            fetch.start()

    @pl.when(s == n_steps - 1)
    def _():
        for k in (s - 1, s):
            for _, send in copies(k):
                send.wait()


def _with_next_weights(body, n_in, n_out, n_scratch, idx, n_steps, *refs):
    ins, rest = refs[:n_in], refs[n_in:]
    nw, rest = rest[:2], rest[2:]
    outs, rest = rest[:n_out], rest[n_out:]
    bf, rest = rest[:2], rest[2:]
    scratch, staging = rest[:n_scratch], rest[n_scratch:]
    _cast_next_weights(*nw, *bf, *staging, idx=idx, n_steps=n_steps)
    body(*ins, *outs, *scratch)


def _launch(name, body, grid, ins, in_specs, out_specs, out_shape, scratch, aliases, next_w):
    if next_w is not None:
        nw_in, nw_out, nidx = next_w
        n_steps = grid[0] * grid[1]
        rows = nw_in.shape[1] // n_steps
        assert nw_in.shape[1] == nw_out.shape[1] == rows * n_steps and rows % 16 == 0
        any_spec = pl.BlockSpec(memory_space=pl.ANY)
        body = functools.partial(_with_next_weights, body, len(ins), len(out_specs), len(scratch), nidx, n_steps)
        ins = ins + [nw_in, nw_out]
        in_specs = in_specs + [any_spec, any_spec]
        out_specs = out_specs + [any_spec, any_spec]
        out_shape = out_shape + [jax.ShapeDtypeStruct(w.shape[1:], jnp.bfloat16) for w in (nw_in, nw_out)]
        scratch = scratch + [pltpu.VMEM((2, rows, nw_in.shape[2]), jnp.float32),
                             pltpu.VMEM((2, rows, nw_out.shape[2]), jnp.float32),
                             pltpu.VMEM((2, rows, nw_in.shape[2]), jnp.bfloat16),
                             pltpu.VMEM((2, rows, nw_out.shape[2]), jnp.bfloat16),
                             pltpu.SemaphoreType.DMA((4, 2))]
    return pl.pallas_call(body, grid=grid, in_specs=in_specs, out_specs=out_specs, out_shape=out_shape,
                          scratch_shapes=scratch, input_output_aliases=aliases,
                          compiler_params=_compiler_params(), name=name)(*ins)


def _layer_spec(arr, idx):
    if arr.ndim == 2:
        return pl.BlockSpec(arr.shape, lambda b, j: (0, 0), pipeline_mode=pl.Buffered(1))
    tail = arr.shape[1:]
    return pl.BlockSpec((None,) + tail, lambda b, j: (idx,) + (0,) * len(tail), pipeline_mode=pl.Buffered(1))


def _stacked_spec(rows, r, idx, n_layers, whole):
    lead = n_layers if whole else None
    l0 = 0 if whole else idx
    if r == 1:
        return pl.BlockSpec((lead, 1, rows, BRANCH), lambda b, j: (l0, b, 0, 0))
    return pl.BlockSpec((lead, rows, r, BRANCH), lambda b, j: (l0, 0, b, 0))


def _stacked_shape(n_layers, bsz, rows, r, dtype):
    shape = (n_layers, bsz, rows, BRANCH) if r == 1 else (n_layers, rows, bsz, BRANCH)
    return jax.ShapeDtypeStruct(shape, dtype)


def _x_spec(tt, r):
    if r == 1:
        return pl.BlockSpec((1, tt, D_MODEL), lambda b, j: (b, j, 0))
    return pl.BlockSpec((tt, r, D_MODEL), lambda b, j: (0, b, 0))


def _compiler_params():
    return pltpu.CompilerParams(dimension_semantics=("arbitrary", "arbitrary"),
                                vmem_limit_bytes=VMEM_LIMIT_BYTES)


def _alias_inputs(ins, in_specs, prev_new):
    aliases = {}
    for k, arr in enumerate(prev_new or ()):
        aliases[len(ins)] = 1 + k
        ins.append(arr)
        in_specs.append(pl.BlockSpec(memory_space=pl.ANY))
    return aliases


def _even_layer(x, states, params, prev_new, *, layer, idx, n_layers, tt, ts, r, start_pos, next_w=None):
    bsz, t = (x.shape[0], x.shape[1]) if r == 1 else (x.shape[1], x.shape[0])
    n_t = t // tt
    first = prev_new is None
    pre_g, post_g, w_in, w_out, conv_w, pool_w, pool_scale = params
    ins, in_specs = [x], [_x_spec(tt, r)]
    if states is not None:
        ins += list(states)
        in_specs += [_stacked_spec(CONV_A - 1, r, idx, n_layers, False),
                     _stacked_spec(POOL_HIST, r, idx, n_layers, False)]
    ins += [pre_g, post_g, w_in, w_out, conv_w, pool_w, pool_scale]
    in_specs += [_layer_spec(pre_g, layer), _layer_spec(post_g, layer), _layer_spec(w_in, idx),
                 _layer_spec(w_out, idx), _layer_spec(conv_w, idx), _layer_spec(pool_w, idx),
                 _layer_spec(pool_scale, idx)]
    aliases = _alias_inputs(ins, in_specs, prev_new)
    body = functools.partial(_even_kernel, tt=tt, ts=ts, r=r, n_t=n_t, start_pos=start_pos, has_hist=states is not None,
                             n_alias=len(aliases), fill=(idx, n_layers) if first else None, layer=layer, idx=idx)
    out_specs = [_x_spec(tt, r), _stacked_spec(CONV_A - 1, r, idx, n_layers, first),
                 _stacked_spec(POOL_HIST, r, idx, n_layers, first)]
    out_shape = [jax.ShapeDtypeStruct(x.shape, x.dtype),
                 _stacked_shape(n_layers, bsz, CONV_A - 1, r, x.dtype),
                 _stacked_shape(n_layers, bsz, POOL_HIST, r, x.dtype)]
    scratch = [pltpu.VMEM((N_SLABS, (PAD_A + tt) * r, LANES), jnp.float32),
               pltpu.VMEM((N_SLABS, (PAD_B + tt) * r, LANES), jnp.float32)]
    return _launch("even_layer", body, (bsz // r, n_t), ins, in_specs, out_specs, out_shape, scratch, aliases, next_w)


def _odd_layer(x, state, params, prev_new, *, layer, idx, n_layers, tt, ts, r, emit_vn, next_w=None):
    bsz, t = (x.shape[0], x.shape[1]) if r == 1 else (x.shape[1], x.shape[0])
    n_t = t // tt
    first = prev_new is None
    ins, in_specs = [x], [_x_spec(tt, r)]
    if state is not None:
        ins.append(state)
        in_specs.append(_stacked_spec(CONV_D - 1, r, idx, n_layers, False))
    for k, arr in enumerate(params):
        ins.append(arr)
        in_specs.append(_layer_spec(arr, layer if k < 2 else idx))
    out_specs = [_x_spec(tt, r), _stacked_spec(CONV_D - 1, r, idx, n_layers, first)]
    out_shape = [jax.ShapeDtypeStruct(x.shape, x.dtype), _stacked_shape(n_layers, bsz, CONV_D - 1, r, x.dtype)]
    if emit_vn:
        out_specs.append(_stacked_spec(tt, r, idx, n_layers, first))
        out_shape.append(_stacked_shape(n_layers, bsz, t, r, x.dtype))
    aliases = _alias_inputs(ins, in_specs, prev_new)
    body = functools.partial(_odd_kernel, tt=tt, ts=ts, r=r, n_t=n_t, has_hist=state is not None, emit_vn=emit_vn,
                             n_alias=len(aliases), fill=(idx, n_layers) if first else None, layer=layer, idx=idx)
    scratch = [pltpu.VMEM((N_SLABS, (PAD_D + tt) * r, LANES), jnp.float32),
               pltpu.VMEM((tt * r, D_MODEL), jnp.bfloat16)]
    return _launch("odd_layer", body, (bsz // r, n_t), ins, in_specs, out_specs, out_shape, scratch, aliases, next_w)


def kernel(x_prompt, x_sample, state_conv_a, state_pool_b, state_conv_d, norm_pre, norm_post, w_in_even, w_out_even, conv_a_w, pool_w, pool_scale, w_in_odd, w_out_odd, sgu_ln_g, sgu_ln_b, sgu_w, sgu_b, conf_dw_w, conf_dw_b, conf_ln_g, conf_ln_b):
    xp = x_prompt
    dec_seq = x_sample.shape[1]
    depth = norm_pre.shape[0]
    n_even, n_odd = w_in_even.shape[0], w_in_odd.shape[0]
    bf16 = jnp.bfloat16
    time_major = lambda a: jnp.swapaxes(a, -3, -2)

    even_tail = (conv_a_w, pool_w.astype(bf16), pool_scale)
    odd_tail = (conf_dw_w, conf_dw_b, conf_ln_g, conf_ln_b)
    mix_p = (sgu_w, jnp.swapaxes(sgu_b, 1, 2))
    mix_s = (jnp.repeat(jnp.transpose(sgu_w[:, :, :dec_seq, :dec_seq], (0, 2, 3, 1)), LANES, axis=-1),
             jnp.repeat(jnp.swapaxes(sgu_b[:, :, :dec_seq], 1, 2), LANES, axis=-1))

    xs = time_major(x_sample)
    even_states_s = (time_major(state_conv_a), time_major(state_pool_b))
    odd_state_s = time_major(state_conv_d)
    sample = dict(tt=dec_seq, ts=dec_seq, r=SAMPLE_SEQS)
    new_even_p = new_even_s = new_odd_p = new_odd_s = None
    w_in_bf, w_out_bf = w_in_even[0].astype(bf16), w_out_even[0].astype(bf16)
    for l in range(depth):
        i = l // 2
        stacks = (w_in_odd, w_out_odd, i) if l % 2 == 0 else (w_in_even, w_out_even, i + 1)
        next_w = stacks if l + 1 < depth else None
        if l % 2 == 0:
            common = dict(layer=l, idx=i, n_layers=n_even)
            params = (norm_pre, norm_post, w_in_bf, w_out_bf) + even_tail
            outs = _even_layer(xp, None, params, new_even_p, start_pos=0, tt=PROMPT_TILE, ts=EVEN_SUB, r=1,
                               next_w=next_w, **common)
            xp, new_even_p = outs[0], outs[1:3]
            xs, *new_even_s = _even_layer(xs, even_states_s, params, new_even_s, start_pos=PAST_LEN,
                                          **common, **sample)
        else:
            common = dict(layer=l, idx=i, n_layers=n_odd)
            head = (norm_pre, norm_post, w_in_bf, w_out_bf, sgu_ln_g, sgu_ln_b)
            outs = _odd_layer(xp, None, head + mix_p + odd_tail, new_odd_p, emit_vn=False, tt=PROMPT_TILE, ts=ODD_SUB,
                              r=1, next_w=next_w, **common)
            xp, new_odd_p = outs[0], outs[1:2]
            xs, *new_odd_s = _odd_layer(xs, odd_state_s, head + mix_s + odd_tail, new_odd_s, emit_vn=True,
                                        **common, **sample)
        if next_w is not None:
            w_in_bf, w_out_bf = outs[-2:]
    ca_p, pb_p = new_even_p
    ca_s, pb_s = new_even_s
    (cd_p,) = new_odd_p
    cd_s, v_s = new_odd_s
    return (xp, time_major(xs), ca_p, time_major(ca_s), pb_p, time_major(pb_s),
            cd_p, time_major(cd_s), time_major(v_s))
```

```python
import functools

import jax
import jax.numpy as jnp
from jax import lax
from jax.experimental import pallas as pl
from jax.experimental.pallas import tpu as pltpu

D_MODEL = 1024
BRANCH = 512
LANES = 128
N_SLABS = BRANCH // LANES
IN_WIDTH = 6 * BRANCH
CONV_A = 3
POOL_WINDOWS = (2, 4, 8, 16)
POOL_HIST = max(POOL_WINDOWS) - 1
CHUNK = 128
CONV_D = 31
PAST_LEN = 16384
EPS = 1e-6

PAD_A = 8
PAD_B = 16
PAD_D = 32

VMEM_LIMIT_BYTES = 56 * 1024 * 1024

PROMPT_TILE = 1024
EVEN_SUB = 256
ODD_SUB = 512
SAMPLE_SEQS = 64


def _rms_norm(x, g):
    ms = jnp.mean(x * x, axis=-1, keepdims=True)
    return x * lax.rsqrt(ms + EPS) * g


def _layer_norm(x, g, b):
    mu = jnp.mean(x, axis=-1, keepdims=True)
    xc = x - mu
    var = jnp.mean(xc * xc, axis=-1, keepdims=True)
    return xc * lax.rsqrt(var + EPS) * g + b


def _sigmoid(x):
    return 0.5 * jnp.tanh(0.5 * x) + 0.5


def _silu(x):
    hx = 0.5 * x
    return hx * jnp.tanh(hx) + hx


def _slab(c):
    return slice(c * LANES, (c + 1) * LANES)


def _row(ref, i):
    return ref[i:i + 1, :]


def _get_rows(ref, t0, ts, r):
    if r == 1:
        return ref[0, t0:t0 + ts, :]
    return ref[t0:t0 + ts].reshape(ts * r, ref.shape[-1])


def _set_rows(ref, t0, ts, r, val):
    if r == 1:
        ref[0, t0:t0 + ts, :] = val
    else:
        ref[t0:t0 + ts] = val.reshape(ts, r, ref.shape[-1])


def _carry_history(ext, hist_ref, pad, n_hist, tt, n_t, r):
    j = pl.program_id(1)
    lo, hi = (pad - n_hist) * r, pad * r

    @pl.when(j == 0)
    def _():
        for c in range(N_SLABS):
            if hist_ref is None:
                ext[c, lo:hi, :] = jnp.zeros((hi - lo, LANES), ext.dtype)
            else:
                ext[c, lo:hi, :] = hist_ref[:, :, _slab(c)].reshape(hi - lo, LANES)

    if n_t > 1:
        @pl.when(j > 0)
        def _():
            for c in range(N_SLABS):
                ext[c, lo:hi, :] = ext[c, lo + tt * r:hi + tt * r, :]


def _zero_other_layers(ref, fill):
    if fill is not None:
        for o in range(fill[1]):
            if o != fill[0]:
                ref[o] = jnp.zeros(ref.shape[1:], ref.dtype)


def _emit_history(ext, new_ref, pad, n_hist, tt, n_t, r, fill):
    @pl.when(pl.program_id(1) == n_t - 1)
    def _():
        dst = new_ref if fill is None else new_ref.at[fill[0]]
        for c in range(N_SLABS):
            last = ext[c, (pad + tt - n_hist) * r:(pad + tt) * r, :]
            if r == 1:
                dst[0, :, _slab(c)] = last
            else:
                dst[:, :, _slab(c)] = last.reshape(n_hist, r, LANES)
        _zero_other_layers(new_ref, fill)


def _conv_slab(ext, w_ref, c, pad, t0, ts, n_taps, r):
    acc = None
    for k in range(n_taps):
        lo = (pad + t0 - (n_taps - 1) + k) * r
        term = ext[c, lo:lo + ts * r, :] * w_ref[k:k + 1, _slab(c)]
        acc = term if acc is None else acc + term
    return acc


def _store_slabs(ext, val, pad, t0, ts, r):
    for c in range(N_SLABS):
        ext[c, (pad + t0) * r:(pad + t0 + ts) * r, :] = val[:, _slab(c)]


def _half_out(wout_ref, y, half):
    return jnp.dot(y.astype(wout_ref.dtype), wout_ref[half * BRANCH:(half + 1) * BRANCH, :],
                   preferred_element_type=jnp.float32)


def _even_kernel(*refs, tt, ts, r, n_t, start_pos, has_hist, n_alias, fill, layer, idx):
    refs = list(refs)
    x_ref = refs.pop(0)
    ha_ref, hb_ref = (refs.pop(0), refs.pop(0)) if has_hist else (None, None)
    pre_ref, post_ref, win_ref, wout_ref, cw_ref, pw_ref, ps_ref = refs[:7]
    y_ref, na_ref, nb_ref, ext_a, ext_b = refs[7 + n_alias:]
    _carry_history(ext_a, ha_ref, PAD_A, CONV_A - 1, tt, n_t, r)
    _carry_history(ext_b, hb_ref, PAD_B, POOL_HIST, tt, n_t, r)
    subs = list(range(0, tt, ts))
    ms = ts * r

    z = {}
    for t0 in subs:
        h = _rms_norm(_get_rows(x_ref, t0, ts, r), _row(pre_ref, layer)).astype(win_ref.dtype)
        z[t0] = jnp.dot(h, win_ref[...], preferred_element_type=jnp.float32)
    part = lambda t0, i: z[t0][:, i * BRANCH:(i + 1) * BRANCH]

    ya, yb = {}, {}
    for t0 in subs:
        _store_slabs(ext_a, part(t0, 1) * part(t0, 2), PAD_A, t0, ts, r)
        conv = jnp.concatenate([_conv_slab(ext_a, cw_ref, c, PAD_A, t0, ts, CONV_A, r) for c in range(N_SLABS)],
                               axis=-1)
        ya[t0] = part(t0, 0) * conv * _silu(part(t0, 3))
    for t0 in subs:
        p = part(t0, 4)
        _store_slabs(ext_b, p, PAD_B, t0, ts, r)
        step = lax.broadcasted_iota(jnp.int32, (ms, LANES), 0) // r
        pos = start_pos + pl.program_id(1) * tt + t0 + step
        ds = []
        for g, w in enumerate(POOL_WINDOWS):
            acc = ext_b[g, (PAD_B + t0) * r:(PAD_B + t0 + ts) * r, :]
            for i in range(1, w):
                acc = acc + ext_b[g, (PAD_B + t0 - i) * r:(PAD_B + t0 - i + ts) * r, :]
            cnt = jnp.minimum(pos + 1, w).astype(jnp.float32)
            ds.append((acc / cnt - p[:, _slab(g)]).astype(jnp.bfloat16))
        mixed = []
        zero = jnp.zeros((LANES, LANES), jnp.bfloat16)
        for g in range(0, N_SLABS, 2):
            w2 = jnp.concatenate([jnp.concatenate([pw_ref[g], zero], axis=1),
                                  jnp.concatenate([zero, pw_ref[g + 1]], axis=1)], axis=0)
            mixed.append(jnp.dot(jnp.concatenate(ds[g:g + 2], axis=1), w2, preferred_element_type=jnp.float32))
        yb[t0] = jnp.concatenate(mixed, axis=-1) * _row(ps_ref, idx) * _silu(part(t0, 5))
    for t0 in subs:
        o = _half_out(wout_ref, ya[t0], 0) + _half_out(wout_ref, yb[t0], 1)
        _set_rows(y_ref, t0, ts, r, _get_rows(x_ref, t0, ts, r) + _rms_norm(o, _row(post_ref, layer)))

    _emit_history(ext_a, na_ref, PAD_A, CONV_A - 1, tt, n_t, r, fill)
    _emit_history(ext_b, nb_ref, PAD_B, POOL_HIST, tt, n_t, r, fill)


def _mix_chunks(vn, ws_ref, bs_ref):
    row = lax.broadcasted_iota(jnp.int32, (CHUNK, CHUNK), 0)
    col = lax.broadcasted_iota(jnp.int32, (CHUNK, CHUNK), 1)
    wm = [jnp.where(col <= row, ws_ref[g], 0.0).astype(jnp.bfloat16) for g in range(N_SLABS)]
    vnb = vn.astype(jnp.bfloat16)
    n_ch = vn.shape[0] // CHUNK
    outs = []
    for g in range(N_SLABS):
        rhs = jnp.concatenate([vnb[ch * CHUNK:(ch + 1) * CHUNK, _slab(g)] for ch in range(n_ch)], axis=1)
        outs.append(jnp.dot(wm[g], rhs, preferred_element_type=jnp.float32) + bs_ref[:, g:g + 1])
    return jnp.concatenate(
        [jnp.concatenate([outs[g][:, _slab(ch)] for g in range(N_SLABS)], axis=1) for ch in range(n_ch)], axis=0)


def _mix_steps(vn, ws_ref, bs_ref, n_steps, r):
    out = []
    for i in range(n_steps):
        acc = None
        for j in range(i + 1):
            term = vn[j * r:(j + 1) * r, :] * ws_ref[i, j:j + 1, :]
            acc = term if acc is None else acc + term
        out.append(acc + bs_ref[i:i + 1, :])
    return jnp.concatenate(out, axis=0)


def _odd_kernel(*refs, tt, ts, r, n_t, has_hist, emit_vn, n_alias, fill, layer, idx):
    refs = list(refs)
    x_ref = refs.pop(0)
    hd_ref = refs.pop(0) if has_hist else None
    (pre_ref, post_ref, win_ref, wout_ref, lng_ref, lnb_ref, ws_ref, bs_ref,
     dw_ref, db_ref, cg_ref, cb_ref) = refs[:12]
    outs = refs[12 + n_alias:]
    if emit_vn:
        y_ref, nd_ref, vn_ref, ext_d, h_ref = outs
    else:
        y_ref, nd_ref, ext_d, h_ref = outs
    _carry_history(ext_d, hd_ref, PAD_D, CONV_D - 1, tt, n_t, r)
    subs = list(range(0, tt, ts))
    hrows = lambda t0: slice(t0 * r, (t0 + ts) * r)

    for t0 in subs:
        h_ref[hrows(t0), :] = _rms_norm(_get_rows(x_ref, t0, ts, r), _row(pre_ref, layer)).astype(jnp.bfloat16)

    conv_out = {t0: [] for t0 in subs}
    for c in range(N_SLABS):
        w_c = jnp.concatenate([win_ref[:, 3 * BRANCH + c * LANES:3 * BRANCH + (c + 1) * LANES],
                               win_ref[:, 4 * BRANCH + c * LANES:4 * BRANCH + (c + 1) * LANES]], axis=1)
        for t0 in subs:
            zc = jnp.dot(h_ref[hrows(t0), :], w_c, preferred_element_type=jnp.float32)
            ext_d[c, (PAD_D + t0) * r:(PAD_D + t0 + ts) * r, :] = zc[:, :LANES] * _sigmoid(zc[:, LANES:])
            conv_out[t0].append(_conv_slab(ext_d, dw_ref, c, PAD_D, t0, ts, CONV_D, r))

    z3 = {t0: jnp.dot(h_ref[hrows(t0), :], win_ref[:, :3 * BRANCH], preferred_element_type=jnp.float32)
          for t0 in subs}
    gd = {t0: jnp.dot(h_ref[hrows(t0), :], win_ref[:, 5 * BRANCH:], preferred_element_type=jnp.float32)
          for t0 in subs}

    o1, yd = {}, {}
    for t0 in subs:
        u, v, gc = (z3[t0][:, i * BRANCH:(i + 1) * BRANCH] for i in range(3))
        vn = _layer_norm(v, _row(lng_ref, idx), _row(lnb_ref, idx))
        if emit_vn:
            _set_rows(vn_ref if fill is None else vn_ref.at[fill[0]], t0, ts, r, vn)
        mixed = _mix_chunks(vn, ws_ref, bs_ref) if r == 1 else _mix_steps(vn, ws_ref, bs_ref, ts, r)
        o1[t0] = _half_out(wout_ref, u * mixed * _silu(gc), 0)
    for t0 in subs:
        y = jnp.concatenate(conv_out[t0], axis=-1) + _row(db_ref, idx)
        yd[t0] = _silu(_layer_norm(y, _row(cg_ref, idx), _row(cb_ref, idx))) * _silu(gd[t0])
    for t0 in subs:
        o = o1[t0] + _half_out(wout_ref, yd[t0], 1)
        _set_rows(y_ref, t0, ts, r, _get_rows(x_ref, t0, ts, r) + _rms_norm(o, _row(post_ref, layer)))

    if emit_vn:
        _zero_other_layers(vn_ref, fill)
    _emit_history(ext_d, nd_ref, PAD_D, CONV_D - 1, tt, n_t, r, fill)


def _chunk_copies(w, bf, stage, obuf, sems, kind, idx, k):
    rows = stage.shape[1]
    sl = pl.ds(pl.multiple_of(k * rows, rows), rows)
    slot = k % 2
    fetch = pltpu.make_async_copy(w.at[idx, sl, :], stage.at[slot], sems.at[2 * kind, slot])
    send = pltpu.make_async_copy(obuf.at[slot], bf.at[sl, :], sems.at[2 * kind + 1, slot])
    return fetch, send


def _cast_next_weights(nw_in, nw_out, bf_in, bf_out, stage_in, stage_out, obuf_in, obuf_out, sems, *, idx, n_steps):
    s = pl.program_id(0) * pl.num_programs(1) + pl.program_id(1)
    pairs = ((nw_in, bf_in, stage_in, obuf_in, 0), (nw_out, bf_out, stage_out, obuf_out, 1))
    copies = lambda k: [_chunk_copies(w, bf, st, ob, sems, kind, idx, k) for w, bf, st, ob, kind in pairs]

    @pl.when(s == 0)
    def _():
        for fetch, _ in copies(s):
            fetch.start()

    for fetch, _ in copies(s):
        fetch.wait()

    @pl.when(s >= 2)
    def _():
        for _, send in copies(s - 2):
            send.wait()

    slot = s % 2
    obuf_in[slot] = stage_in[slot].astype(jnp.bfloat16)
    obuf_out[slot] = stage_out[slot].astype(jnp.bfloat16)
    for _, send in copies(s):
        send.start()

    @pl.when(s + 1 < n_steps)
    def _():
        for fetch, _ in copies(s + 1):
            fetch.start()

    @pl.when(s == n_steps - 1)
    def _():
        for k in (s - 1, s):
            for _, send in copies(k):
                send.wait()


def _with_next_weights(body, n_in, n_out, n_scratch, idx, n_steps, *refs):
    ins, rest = refs[:n_in], refs[n_in:]
    nw, rest = rest[:2], rest[2:]
    outs, rest = rest[:n_out], rest[n_out:]
    bf, rest = rest[:2], rest[2:]
    scratch, staging = rest[:n_scratch], rest[n_scratch:]
    _cast_next_weights(*nw, *bf, *staging, idx=idx, n_steps=n_steps)
    body(*ins, *outs, *scratch)


def _launch(name, body, grid, ins, in_specs, out_specs, out_shape, scratch, aliases, next_w):
    if next_w is not None:
        nw_in, nw_out, nidx = next_w
        n_steps = grid[0] * grid[1]
        rows = nw_in.shape[1] // n_steps
        assert nw_in.shape[1] == nw_out.shape[1] == rows * n_steps and rows % 16 == 0
        any_spec = pl.BlockSpec(memory_space=pl.ANY)
        body = functools.partial(_with_next_weights, body, len(ins), len(out_specs), len(scratch), nidx, n_steps)
        ins = ins + [nw_in, nw_out]
        in_specs = in_specs + [any_spec, any_spec]
        out_specs = out_specs + [any_spec, any_spec]
        out_shape = out_shape + [jax.ShapeDtypeStruct(w.shape[1:], jnp.bfloat16) for w in (nw_in, nw_out)]
        scratch = scratch + [pltpu.VMEM((2, rows, nw_in.shape[2]), jnp.float32),
                             pltpu.VMEM((2, rows, nw_out.shape[2]), jnp.float32),
                             pltpu.VMEM((2, rows, nw_in.shape[2]), jnp.bfloat16),
                             pltpu.VMEM((2, rows, nw_out.shape[2]), jnp.bfloat16),
                             pltpu.SemaphoreType.DMA((4, 2))]
    return pl.pallas_call(body, grid=grid, in_specs=in_specs, out_specs=out_specs, out_shape=out_shape,
                          scratch_shapes=scratch, input_output_aliases=aliases,
                          compiler_params=_compiler_params(), name=name)(*ins)


def _layer_spec(arr, idx):
    if arr.ndim == 2:
        return pl.BlockSpec(arr.shape, lambda b, j: (0, 0), pipeline_mode=pl.Buffered(1))
    tail = arr.shape[1:]
    return pl.BlockSpec((None,) + tail, lambda b, j: (idx,) + (0,) * len(tail), pipeline_mode=pl.Buffered(1))


def _stacked_spec(rows, r, idx, n_layers, whole):
    lead = n_layers if whole else None
    l0 = 0 if whole else idx
    if r == 1:
        return pl.BlockSpec((lead, 1, rows, BRANCH), lambda b, j: (l0, b, 0, 0))
    return pl.BlockSpec((lead, rows, r, BRANCH), lambda b, j: (l0, 0, b, 0))


def _stacked_shape(n_layers, bsz, rows, r, dtype):
    shape = (n_layers, bsz, rows, BRANCH) if r == 1 else (n_layers, rows, bsz, BRANCH)
    return jax.ShapeDtypeStruct(shape, dtype)


def _x_spec(tt, r):
    if r == 1:
        return pl.BlockSpec((1, tt, D_MODEL), lambda b, j: (b, j, 0))
    return pl.BlockSpec((tt, r, D_MODEL), lambda b, j: (0, b, 0))


def _compiler_params():
    return pltpu.CompilerParams(dimension_semantics=("arbitrary", "arbitrary"),
                                vmem_limit_bytes=VMEM_LIMIT_BYTES)


def _alias_inputs(ins, in_specs, prev_new):
    aliases = {}
    for k, arr in enumerate(prev_new or ()):
        aliases[len(ins)] = 1 + k
        ins.append(arr)
        in_specs.append(pl.BlockSpec(memory_space=pl.ANY))
    return aliases


def _even_layer(x, states, params, prev_new, *, layer, idx, n_layers, tt, ts, r, start_pos, next_w=None):
    bsz, t = (x.shape[0], x.shape[1]) if r == 1 else (x.shape[1], x.shape[0])
    n_t = t // tt
    first = prev_new is None
    pre_g, post_g, w_in, w_out, conv_w, pool_w, pool_scale = params
    ins, in_specs = [x], [_x_spec(tt, r)]
    if states is not None:
        ins += list(states)
        in_specs += [_stacked_spec(CONV_A - 1, r, idx, n_layers, False),
                     _stacked_spec(POOL_HIST, r, idx, n_layers, False)]
    ins += [pre_g, post_g, w_in, w_out, conv_w, pool_w, pool_scale]
    in_specs += [_layer_spec(pre_g, layer), _layer_spec(post_g, layer), _layer_spec(w_in, idx),
                 _layer_spec(w_out, idx), _layer_spec(conv_w, idx), _layer_spec(pool_w, idx),
                 _layer_spec(pool_scale, idx)]
    aliases = _alias_inputs(ins, in_specs, prev_new)
    body = functools.partial(_even_kernel, tt=tt, ts=ts, r=r, n_t=n_t, start_pos=start_pos, has_hist=states is not None,
                             n_alias=len(aliases), fill=(idx, n_layers) if first else None, layer=layer, idx=idx)
    out_specs = [_x_spec(tt, r), _stacked_spec(CONV_A - 1, r, idx, n_layers, first),
                 _stacked_spec(POOL_HIST, r, idx, n_layers, first)]
    out_shape = [jax.ShapeDtypeStruct(x.shape, x.dtype),
                 _stacked_shape(n_layers, bsz, CONV_A - 1, r, x.dtype),
                 _stacked_shape(n_layers, bsz, POOL_HIST, r, x.dtype)]
    scratch = [pltpu.VMEM((N_SLABS, (PAD_A + tt) * r, LANES), jnp.float32),
               pltpu.VMEM((N_SLABS, (PAD_B + tt) * r, LANES), jnp.float32)]
    return _launch("even_layer", body, (bsz // r, n_t), ins, in_specs, out_specs, out_shape, scratch, aliases, next_w)


def _odd_layer(x, state, params, prev_new, *, layer, idx, n_layers, tt, ts, r, emit_vn, next_w=None):
    bsz, t = (x.shape[0], x.shape[1]) if r == 1 else (x.shape[1], x.shape[0])
    n_t = t // tt
    first = prev_new is None
    ins, in_specs = [x], [_x_spec(tt, r)]
    if state is not None:
        ins.append(state)
        in_specs.append(_stacked_spec(CONV_D - 1, r, idx, n_layers, False))
    for k, arr in enumerate(params):
        ins.append(arr)
        in_specs.append(_layer_spec(arr, layer if k < 2 else idx))
    out_specs = [_x_spec(tt, r), _stacked_spec(CONV_D - 1, r, idx, n_layers, first)]
    out_shape = [jax.ShapeDtypeStruct(x.shape, x.dtype), _stacked_shape(n_layers, bsz, CONV_D - 1, r, x.dtype)]
    if emit_vn:
        out_specs.append(_stacked_spec(tt, r, idx, n_layers, first))
        out_shape.append(_stacked_shape(n_layers, bsz, t, r, x.dtype))
    aliases = _alias_inputs(ins, in_specs, prev_new)
    body = functools.partial(_odd_kernel, tt=tt, ts=ts, r=r, n_t=n_t, has_hist=state is not None, emit_vn=emit_vn,
                             n_alias=len(aliases), fill=(idx, n_layers) if first else None, layer=layer, idx=idx)
    scratch = [pltpu.VMEM((N_SLABS, (PAD_D + tt) * r, LANES), jnp.float32),
               pltpu.VMEM((tt * r, D_MODEL), jnp.bfloat16)]
    return _launch("odd_layer", body, (bsz // r, n_t), ins, in_specs, out_specs, out_shape, scratch, aliases, next_w)


def kernel(x_prompt, x_sample, state_conv_a, state_pool_b, state_conv_d, norm_pre, norm_post, w_in_even, w_out_even, conv_a_w, pool_w, pool_scale, w_in_odd, w_out_odd, sgu_ln_g, sgu_ln_b, sgu_w, sgu_b, conf_dw_w, conf_dw_b, conf_ln_g, conf_ln_b):
    xp = x_prompt
    dec_seq = x_sample.shape[1]
    depth = norm_pre.shape[0]
    n_even, n_odd = w_in_even.shape[0], w_in_odd.shape[0]
    bf16 = jnp.bfloat16
    time_major = lambda a: jnp.swapaxes(a, -3, -2)

    even_tail = (conv_a_w, pool_w.astype(bf16), pool_scale)
    odd_tail = (conf_dw_w, conf_dw_b, conf_ln_g, conf_ln_b)
    mix_p = (sgu_w, jnp.swapaxes(sgu_b, 1, 2))
    mix_s = (jnp.repeat(jnp.transpose(sgu_w[:, :, :dec_seq, :dec_seq], (0, 2, 3, 1)), LANES, axis=-1),
             jnp.repeat(jnp.swapaxes(sgu_b[:, :, :dec_seq], 1, 2), LANES, axis=-1))

    xs = time_major(x_sample)
    even_states_s = (time_major(state_conv_a), time_major(state_pool_b))
    odd_state_s = time_major(state_conv_d)
    sample = dict(tt=dec_seq, ts=dec_seq, r=SAMPLE_SEQS)
    new_even_p = new_even_s = new_odd_p = new_odd_s = None
    w_in_cur, w_out_cur = w_in_even, w_out_even
    for l in range(depth):
        i = l // 2
        stacks = (w_in_odd, w_out_odd, i) if l % 2 == 0 else (w_in_even, w_out_even, i + 1)
        next_w = stacks if l + 1 < depth else None
        if l % 2 == 0:
            common = dict(layer=l, idx=i, n_layers=n_even)
            params = (norm_pre, norm_post, w_in_cur, w_out_cur) + even_tail
            outs = _even_layer(xp, None, params, new_even_p, start_pos=0, tt=PROMPT_TILE, ts=EVEN_SUB, r=1,
                               next_w=next_w, **common)
            xp, new_even_p = outs[0], outs[1:3]
            xs, *new_even_s = _even_layer(xs, even_states_s, params, new_even_s, start_pos=PAST_LEN,
                                          **common, **sample)
        else:
            common = dict(layer=l, idx=i, n_layers=n_odd)
            head = (norm_pre, norm_post, w_in_cur, w_out_cur, sgu_ln_g, sgu_ln_b)
            outs = _odd_layer(xp, None, head + mix_p + odd_tail, new_odd_p, emit_vn=False, tt=PROMPT_TILE, ts=ODD_SUB,
                              r=1, next_w=next_w, **common)
            xp, new_odd_p = outs[0], outs[1:2]
            xs, *new_odd_s = _odd_layer(xs, odd_state_s, head + mix_s + odd_tail, new_odd_s, emit_vn=True,
                                        **common, **sample)
        if next_w is not None:
            w_in_cur, w_out_cur = outs[-2:]
    ca_p, pb_p = new_even_p
    ca_s, pb_s = new_even_s
    (cd_p,) = new_odd_p
    cd_s, v_s = new_odd_s
    return (xp, time_major(xs), ca_p, time_major(ca_s), pb_p, time_major(pb_s),
            cd_p, time_major(cd_s), time_major(v_s))
```

```python
import functools

import jax
import jax.numpy as jnp
from jax import lax
from jax.experimental import pallas as pl
from jax.experimental.pallas import tpu as pltpu

D_MODEL = 1024
BRANCH = 512
LANES = 128
N_SLABS = BRANCH // LANES
IN_WIDTH = 6 * BRANCH
CONV_A = 3
POOL_WINDOWS = (2, 4, 8, 16)
POOL_HIST = max(POOL_WINDOWS) - 1
CHUNK = 128
CONV_D = 31
PAST_LEN = 16384
EPS = 1e-6

PAD_A = 8
PAD_B = 16
PAD_D = 32

VMEM_LIMIT_BYTES = 56 * 1024 * 1024

PROMPT_TILE = 1024
EVEN_SUB = 256
ODD_SUB = 512
SAMPLE_SEQS = 64


def _rms_norm(x, g):
    ms = jnp.mean(x * x, axis=-1, keepdims=True)
    return x * lax.rsqrt(ms + EPS) * g


def _layer_norm(x, g, b):
    mu = jnp.mean(x, axis=-1, keepdims=True)
    xc = x - mu
    var = jnp.mean(xc * xc, axis=-1, keepdims=True)
    return xc * lax.rsqrt(var + EPS) * g + b


def _sigmoid(x):
    return 0.5 * jnp.tanh(0.5 * x) + 0.5


def _silu(x):
    hx = 0.5 * x
    return hx * jnp.tanh(hx) + hx


def _slab(c):
    return slice(c * LANES, (c + 1) * LANES)


def _row(ref, i):
    return ref[i:i + 1, :]


def _get_rows(ref, t0, ts, r):
    if r == 1:
        return ref[0, t0:t0 + ts, :]
    return ref[t0:t0 + ts].reshape(ts * r, ref.shape[-1])


def _set_rows(ref, t0, ts, r, val):
    if r == 1:
        ref[0, t0:t0 + ts, :] = val
    else:
        ref[t0:t0 + ts] = val.reshape(ts, r, ref.shape[-1])


def _carry_history(ext, hist_ref, pad, n_hist, tt, n_t, r):
    j = pl.program_id(1)
    lo, hi = (pad - n_hist) * r, pad * r

    @pl.when(j == 0)
    def _():
        for c in range(N_SLABS):
            if hist_ref is None:
                ext[c, lo:hi, :] = jnp.zeros((hi - lo, LANES), ext.dtype)
            else:
                ext[c, lo:hi, :] = hist_ref[:, :, _slab(c)].reshape(hi - lo, LANES)

    if n_t > 1:
        @pl.when(j > 0)
        def _():
            for c in range(N_SLABS):
                ext[c, lo:hi, :] = ext[c, lo + tt * r:hi + tt * r, :]


def _zero_other_layers(ref, fill):
    if fill is not None:
        for o in range(fill[1]):
            if o != fill[0]:
                ref[o] = jnp.zeros(ref.shape[1:], ref.dtype)


def _emit_history(ext, new_ref, pad, n_hist, tt, n_t, r, fill):
    @pl.when(pl.program_id(1) == n_t - 1)
    def _():
        dst = new_ref if fill is None else new_ref.at[fill[0]]
        for c in range(N_SLABS):
            last = ext[c, (pad + tt - n_hist) * r:(pad + tt) * r, :]
            if r == 1:
                dst[0, :, _slab(c)] = last
            else:
                dst[:, :, _slab(c)] = last.reshape(n_hist, r, LANES)
        _zero_other_layers(new_ref, fill)


def _conv_slab(ext, w_ref, c, pad, t0, ts, n_taps, r):
    acc = None
    for k in range(n_taps):
        lo = (pad + t0 - (n_taps - 1) + k) * r
        term = ext[c, lo:lo + ts * r, :] * w_ref[k:k + 1, _slab(c)]
        acc = term if acc is None else acc + term
    return acc


def _store_slabs(ext, val, pad, t0, ts, r):
    for c in range(N_SLABS):
        ext[c, (pad + t0) * r:(pad + t0 + ts) * r, :] = val[:, _slab(c)]


def _half_out(wout_ref, y, half):
    return jnp.dot(y.astype(jnp.bfloat16), wout_ref[half * BRANCH:(half + 1) * BRANCH, :],
                   preferred_element_type=jnp.float32)


def _even_kernel(*refs, tt, ts, r, n_t, start_pos, has_hist, n_alias, fill, layer, idx):
    refs = list(refs)
    x_ref = refs.pop(0)
    ha_ref, hb_ref = (refs.pop(0), refs.pop(0)) if has_hist else (None, None)
    pre_ref, post_ref, win_ref, wout_ref, cw_ref, pw_ref, ps_ref = refs[:7]
    y_ref, na_ref, nb_ref, ext_a, ext_b = refs[7 + n_alias:]
    _carry_history(ext_a, ha_ref, PAD_A, CONV_A - 1, tt, n_t, r)
    _carry_history(ext_b, hb_ref, PAD_B, POOL_HIST, tt, n_t, r)
    subs = list(range(0, tt, ts))
    ms = ts * r

    z = {}
    for t0 in subs:
        h = _rms_norm(_get_rows(x_ref, t0, ts, r), _row(pre_ref, layer)).astype(jnp.bfloat16)
        z[t0] = jnp.dot(h, win_ref[...], preferred_element_type=jnp.float32)
    part = lambda t0, i: z[t0][:, i * BRANCH:(i + 1) * BRANCH]

    ya, yb = {}, {}
    for t0 in subs:
        _store_slabs(ext_a, part(t0, 1) * part(t0, 2), PAD_A, t0, ts, r)
        conv = jnp.concatenate([_conv_slab(ext_a, cw_ref, c, PAD_A, t0, ts, CONV_A, r) for c in range(N_SLABS)],
                               axis=-1)
        ya[t0] = part(t0, 0) * conv * _silu(part(t0, 3))
    for t0 in subs:
        p = part(t0, 4)
        _store_slabs(ext_b, p, PAD_B, t0, ts, r)
        step = lax.broadcasted_iota(jnp.int32, (ms, LANES), 0) // r
        pos = start_pos + pl.program_id(1) * tt + t0 + step
        ds = []
        for g, w in enumerate(POOL_WINDOWS):
            acc = ext_b[g, (PAD_B + t0) * r:(PAD_B + t0 + ts) * r, :]
            for i in range(1, w):
                acc = acc + ext_b[g, (PAD_B + t0 - i) * r:(PAD_B + t0 - i + ts) * r, :]
            cnt = jnp.minimum(pos + 1, w).astype(jnp.float32)
            ds.append((acc / cnt - p[:, _slab(g)]).astype(jnp.bfloat16))
        mixed = []
        zero = jnp.zeros((LANES, LANES), jnp.bfloat16)
        for g in range(0, N_SLABS, 2):
            w2 = jnp.concatenate([jnp.concatenate([pw_ref[g], zero], axis=1),
                                  jnp.concatenate([zero, pw_ref[g + 1]], axis=1)], axis=0)
            mixed.append(jnp.dot(jnp.concatenate(ds[g:g + 2], axis=1), w2, preferred_element_type=jnp.float32))
        yb[t0] = jnp.concatenate(mixed, axis=-1) * _row(ps_ref, idx) * _silu(part(t0, 5))
    for t0 in subs:
        o = _half_out(wout_ref, ya[t0], 0) + _half_out(wout_ref, yb[t0], 1)
        _set_rows(y_ref, t0, ts, r, _get_rows(x_ref, t0, ts, r) + _rms_norm(o, _row(post_ref, layer)))

    _emit_history(ext_a, na_ref, PAD_A, CONV_A - 1, tt, n_t, r, fill)
    _emit_history(ext_b, nb_ref, PAD_B, POOL_HIST, tt, n_t, r, fill)


def _mix_chunks(vn, ws_ref, bs_ref):
    row = lax.broadcasted_iota(jnp.int32, (CHUNK, CHUNK), 0)
    col = lax.broadcasted_iota(jnp.int32, (CHUNK, CHUNK), 1)
    wm = [jnp.where(col <= row, ws_ref[g], 0.0).astype(jnp.bfloat16) for g in range(N_SLABS)]
    vnb = vn.astype(jnp.bfloat16)
    n_ch = vn.shape[0] // CHUNK
    outs = []
    for g in range(N_SLABS):
        rhs = jnp.concatenate([vnb[ch * CHUNK:(ch + 1) * CHUNK, _slab(g)] for ch in range(n_ch)], axis=1)
        outs.append(jnp.dot(wm[g], rhs, preferred_element_type=jnp.float32) + bs_ref[:, g:g + 1])
    return jnp.concatenate(
        [jnp.concatenate([outs[g][:, _slab(ch)] for g in range(N_SLABS)], axis=1) for ch in range(n_ch)], axis=0)


def _mix_steps(vn, ws_ref, bs_ref, n_steps, r):
    out = []
    for i in range(n_steps):
        acc = None
        for j in range(i + 1):
            term = vn[j * r:(j + 1) * r, :] * ws_ref[i, j:j + 1, :]
            acc = term if acc is None else acc + term
        out.append(acc + bs_ref[i:i + 1, :])
    return jnp.concatenate(out, axis=0)


def _odd_kernel(*refs, tt, ts, r, n_t, has_hist, emit_vn, n_alias, fill, layer, idx, y_native):
    refs = list(refs)
    x_ref = refs.pop(0)
    hd_ref = refs.pop(0) if has_hist else None
    (pre_ref, post_ref, win_ref, wout_ref, lng_ref, lnb_ref, ws_ref, bs_ref,
     dw_ref, db_ref, cg_ref, cb_ref) = refs[:12]
    outs = refs[12 + n_alias:]
    if y_native:
        y_hbm, y_sems = outs[0], outs.pop()
        outs[0] = outs.pop()
    if emit_vn:
        y_ref, nd_ref, vn_ref, ext_d, h_ref = outs
    else:
        y_ref, nd_ref, ext_d, h_ref = outs
    _carry_history(ext_d, hd_ref, PAD_D, CONV_D - 1, tt, n_t, r)
    subs = list(range(0, tt, ts))
    hrows = lambda t0: slice(t0 * r, (t0 + ts) * r)

    for t0 in subs:
        h_ref[hrows(t0), :] = _rms_norm(_get_rows(x_ref, t0, ts, r), _row(pre_ref, layer)).astype(jnp.bfloat16)

    conv_out = {t0: [] for t0 in subs}
    for c in range(N_SLABS):
        w_c = jnp.concatenate([win_ref[:, 3 * BRANCH + c * LANES:3 * BRANCH + (c + 1) * LANES],
                               win_ref[:, 4 * BRANCH + c * LANES:4 * BRANCH + (c + 1) * LANES]], axis=1)
        for t0 in subs:
            zc = jnp.dot(h_ref[hrows(t0), :], w_c, preferred_element_type=jnp.float32)
            ext_d[c, (PAD_D + t0) * r:(PAD_D + t0 + ts) * r, :] = zc[:, :LANES] * _sigmoid(zc[:, LANES:])
            conv_out[t0].append(_conv_slab(ext_d, dw_ref, c, PAD_D, t0, ts, CONV_D, r))

    z3 = {t0: jnp.dot(h_ref[hrows(t0), :], win_ref[:, :3 * BRANCH], preferred_element_type=jnp.float32)
          for t0 in subs}
    gd = {t0: jnp.dot(h_ref[hrows(t0), :], win_ref[:, 5 * BRANCH:], preferred_element_type=jnp.float32)
          for t0 in subs}

    o1, yd = {}, {}
    for t0 in subs:
        u, v, gc = (z3[t0][:, i * BRANCH:(i + 1) * BRANCH] for i in range(3))
        vn = _layer_norm(v, _row(lng_ref, idx), _row(lnb_ref, idx))
        if emit_vn:
            _set_rows(vn_ref if fill is None else vn_ref.at[fill[0]], t0, ts, r, vn)
        mixed = _mix_chunks(vn, ws_ref, bs_ref) if r == 1 else _mix_steps(vn, ws_ref, bs_ref, ts, r)
        o1[t0] = _half_out(wout_ref, u * mixed * _silu(gc), 0)
    for t0 in subs:
        y = jnp.concatenate(conv_out[t0], axis=-1) + _row(db_ref, idx)
        yd[t0] = _silu(_layer_norm(y, _row(cg_ref, idx), _row(cb_ref, idx))) * _silu(gd[t0])
    for t0 in subs:
        o = o1[t0] + _half_out(wout_ref, yd[t0], 1)
        _set_rows(y_ref, t0, ts, r, _get_rows(x_ref, t0, ts, r) + _rms_norm(o, _row(post_ref, layer)))

    if y_native:
        seqs = pl.ds(pl.multiple_of(pl.program_id(0) * r, r), r)
        sends = [pltpu.make_async_copy(y_ref.at[t], y_hbm.at[seqs, t, :], y_sems.at[t]) for t in range(tt)]
        for c in sends:
            c.start()
        for c in sends:
            c.wait()
    if emit_vn:
        _zero_other_layers(vn_ref, fill)
    _emit_history(ext_d, nd_ref, PAD_D, CONV_D - 1, tt, n_t, r, fill)


def _chunk_copies(w, bf, stage, obuf, sems, kind, idx, k):
    rows = stage.shape[1]
    sl = pl.ds(pl.multiple_of(k * rows, rows), rows)
    slot = k % 2
    fetch = pltpu.make_async_copy(w.at[idx, sl, :], stage.at[slot], sems.at[2 * kind, slot])
    send = pltpu.make_async_copy(obuf.at[slot], bf.at[sl, :], sems.at[2 * kind + 1, slot])
    return fetch, send


def _cast_next_weights(nw_in, nw_out, bf_in, bf_out, stage_in, stage_out, obuf_in, obuf_out, sems, *, idx, n_steps):
    s = pl.program_id(0) * pl.num_programs(1) + pl.program_id(1)
    pairs = ((nw_in, bf_in, stage_in, obuf_in, 0), (nw_out, bf_out, stage_out, obuf_out, 1))
    copies = lambda k: [_chunk_copies(w, bf, st, ob, sems, kind, idx, k) for w, bf, st, ob, kind in pairs]

    @pl.when(s == 0)
    def _():
        for fetch, _ in copies(s):
            fetch.start()

    for fetch, _ in copies(s):
        fetch.wait()

    @pl.when(s >= 2)
    def _():
        for _, send in copies(s - 2):
            send.wait()

    slot = s % 2
    obuf_in[slot] = stage_in[slot].astype(jnp.bfloat16)
    obuf_out[slot] = stage_out[slot].astype(jnp.bfloat16)
    for _, send in copies(s):
        send.start()

    @pl.when(s + 1 < n_steps)
    def _():
        for fetch, _ in copies(s + 1):
            fetch.start()

    @pl.when(s == n_steps - 1)
    def _():
        for k in (s - 1, s):
            for _, send in copies(k):
                send.wait()


def _with_next_weights(body, n_in, n_out, n_scratch, idx, n_steps, *refs):
    ins, rest = refs[:n_in], refs[n_in:]
    nw, rest = rest[:2], rest[2:]
    outs, rest = rest[:n_out], rest[n_out:]
    bf, rest = rest[:2], rest[2:]
    scratch, staging = rest[:n_scratch], rest[n_scratch:]
    _cast_next_weights(*nw, *bf, *staging, idx=idx, n_steps=n_steps)
    body(*ins, *outs, *scratch)


def _launch(name, body, grid, ins, in_specs, out_specs, out_shape, scratch, aliases, next_w):
    if next_w is not None:
        nw_in, nw_out, nidx = next_w
        n_steps = grid[0] * grid[1]
        rows = nw_in.shape[1] // n_steps
        assert nw_in.shape[1] == nw_out.shape[1] == rows * n_steps and rows % 16 == 0
        any_spec = pl.BlockSpec(memory_space=pl.ANY)
        body = functools.partial(_with_next_weights, body, len(ins), len(out_specs), len(scratch), nidx, n_steps)
        ins = ins + [nw_in, nw_out]
        in_specs = in_specs + [any_spec, any_spec]
        out_specs = out_specs + [any_spec, any_spec]
        out_shape = out_shape + [jax.ShapeDtypeStruct(w.shape[1:], jnp.bfloat16) for w in (nw_in, nw_out)]
        scratch = scratch + [pltpu.VMEM((2, rows, nw_in.shape[2]), jnp.float32),
                             pltpu.VMEM((2, rows, nw_out.shape[2]), jnp.float32),
                             pltpu.VMEM((2, rows, nw_in.shape[2]), jnp.bfloat16),
                             pltpu.VMEM((2, rows, nw_out.shape[2]), jnp.bfloat16),
                             pltpu.SemaphoreType.DMA((4, 2))]
    return pl.pallas_call(body, grid=grid, in_specs=in_specs, out_specs=out_specs, out_shape=out_shape,
                          scratch_shapes=scratch, input_output_aliases=aliases,
                          compiler_params=_compiler_params(), name=name)(*ins)


def _layer_spec(arr, idx):
    if arr.ndim == 2:
        return pl.BlockSpec(arr.shape, lambda b, j: (0, 0), pipeline_mode=pl.Buffered(1))
    tail = arr.shape[1:]
    return pl.BlockSpec((None,) + tail, lambda b, j: (idx,) + (0,) * len(tail), pipeline_mode=pl.Buffered(1))


def _stacked_spec(rows, r, idx, n_layers, whole):
    lead = n_layers if whole else None
    l0 = 0 if whole else idx
    if r == 1:
        return pl.BlockSpec((lead, 1, rows, BRANCH), lambda b, j: (l0, b, 0, 0))
    return pl.BlockSpec((lead, rows, r, BRANCH), lambda b, j: (l0, 0, b, 0))


def _stacked_shape(n_layers, bsz, rows, r, dtype):
    shape = (n_layers, bsz, rows, BRANCH) if r == 1 else (n_layers, rows, bsz, BRANCH)
    return jax.ShapeDtypeStruct(shape, dtype)


def _x_spec(tt, r):
    if r == 1:
        return pl.BlockSpec((1, tt, D_MODEL), lambda b, j: (b, j, 0))
    return pl.BlockSpec((tt, r, D_MODEL), lambda b, j: (0, b, 0))


def _compiler_params():
    return pltpu.CompilerParams(dimension_semantics=("arbitrary", "arbitrary"),
                                vmem_limit_bytes=VMEM_LIMIT_BYTES)


def _alias_inputs(ins, in_specs, prev_new):
    aliases = {}
    for k, arr in enumerate(prev_new or ()):
        aliases[len(ins)] = 1 + k
        ins.append(arr)
        in_specs.append(pl.BlockSpec(memory_space=pl.ANY))
    return aliases


def _even_layer(x, states, params, prev_new, *, layer, idx, n_layers, tt, ts, r, start_pos, next_w=None):
    bsz, t = (x.shape[0], x.shape[1]) if r == 1 else (x.shape[1], x.shape[0])
    n_t = t // tt
    first = prev_new is None
    pre_g, post_g, w_in, w_out, conv_w, pool_w, pool_scale = params
    ins, in_specs = [x], [_x_spec(tt, r)]
    if states is not None:
        ins += list(states)
        in_specs += [_stacked_spec(CONV_A - 1, r, idx, n_layers, False),
                     _stacked_spec(POOL_HIST, r, idx, n_layers, False)]
    ins += [pre_g, post_g, w_in, w_out, conv_w, pool_w, pool_scale]
    in_specs += [_layer_spec(pre_g, layer), _layer_spec(post_g, layer), _layer_spec(w_in, idx),
                 _layer_spec(w_out, idx), _layer_spec(conv_w, idx), _layer_spec(pool_w, idx),
                 _layer_spec(pool_scale, idx)]
    aliases = _alias_inputs(ins, in_specs, prev_new)
    body = functools.partial(_even_kernel, tt=tt, ts=ts, r=r, n_t=n_t, start_pos=start_pos, has_hist=states is not None,
                             n_alias=len(aliases), fill=(idx, n_layers) if first else None, layer=layer, idx=idx)
    out_specs = [_x_spec(tt, r), _stacked_spec(CONV_A - 1, r, idx, n_layers, first),
                 _stacked_spec(POOL_HIST, r, idx, n_layers, first)]
    out_shape = [jax.ShapeDtypeStruct(x.shape, x.dtype),
                 _stacked_shape(n_layers, bsz, CONV_A - 1, r, x.dtype),
                 _stacked_shape(n_layers, bsz, POOL_HIST, r, x.dtype)]
    scratch = [pltpu.VMEM((N_SLABS, (PAD_A + tt) * r, LANES), jnp.float32),
               pltpu.VMEM((N_SLABS, (PAD_B + tt) * r, LANES), jnp.float32)]
    return _launch("even_layer", body, (bsz // r, n_t), ins, in_specs, out_specs, out_shape, scratch, aliases, next_w)


def _odd_layer(x, state, params, prev_new, *, layer, idx, n_layers, tt, ts, r, emit_vn, next_w=None, y_native=False):
    bsz, t = (x.shape[0], x.shape[1]) if r == 1 else (x.shape[1], x.shape[0])
    n_t = t // tt
    first = prev_new is None
    ins, in_specs = [x], [_x_spec(tt, r)]
    if state is not None:
        ins.append(state)
        in_specs.append(_stacked_spec(CONV_D - 1, r, idx, n_layers, False))
    for k, arr in enumerate(params):
        ins.append(arr)
        in_specs.append(_layer_spec(arr, layer if k < 2 else idx))
    out_specs = [_x_spec(tt, r), _stacked_spec(CONV_D - 1, r, idx, n_layers, first)]
    out_shape = [jax.ShapeDtypeStruct(x.shape, x.dtype), _stacked_shape(n_layers, bsz, CONV_D - 1, r, x.dtype)]
    if emit_vn:
        out_specs.append(_stacked_spec(tt, r, idx, n_layers, first))
        out_shape.append(_stacked_shape(n_layers, bsz, t, r, x.dtype))
    aliases = _alias_inputs(ins, in_specs, prev_new)
    body = functools.partial(_odd_kernel, tt=tt, ts=ts, r=r, n_t=n_t, has_hist=state is not None, emit_vn=emit_vn,
                             n_alias=len(aliases), fill=(idx, n_layers) if first else None, layer=layer, idx=idx,
                             y_native=y_native)
    scratch = [pltpu.VMEM((N_SLABS, (PAD_D + tt) * r, LANES), jnp.float32),
               pltpu.VMEM((tt * r, D_MODEL), jnp.bfloat16)]
    if y_native:
        assert r > 1 and n_t == 1
        out_specs[0] = pl.BlockSpec(memory_space=pl.ANY)
        out_shape[0] = jax.ShapeDtypeStruct((bsz, t, D_MODEL), x.dtype)
        scratch += [pltpu.VMEM((tt, r, D_MODEL), jnp.float32), pltpu.SemaphoreType.DMA((tt,))]
    return _launch("odd_layer", body, (bsz // r, n_t), ins, in_specs, out_specs, out_shape, scratch, aliases, next_w)


def kernel(x_prompt, x_sample, state_conv_a, state_pool_b, state_conv_d, norm_pre, norm_post, w_in_even, w_out_even, conv_a_w, pool_w, pool_scale, w_in_odd, w_out_odd, sgu_ln_g, sgu_ln_b, sgu_w, sgu_b, conf_dw_w, conf_dw_b, conf_ln_g, conf_ln_b):
    xp = x_prompt
    dec_seq = x_sample.shape[1]
    depth = norm_pre.shape[0]
    n_even, n_odd = w_in_even.shape[0], w_in_odd.shape[0]
    bf16 = jnp.bfloat16
    time_major = lambda a: jnp.swapaxes(a, -3, -2)

    even_tail = (conv_a_w, pool_w.astype(bf16), pool_scale)
    odd_tail = (conf_dw_w, conf_dw_b, conf_ln_g, conf_ln_b)
    mix_p = (sgu_w, jnp.swapaxes(sgu_b, 1, 2))
    mix_s = (jnp.repeat(jnp.transpose(sgu_w[:, :, :dec_seq, :dec_seq], (0, 2, 3, 1)), LANES, axis=-1),
             jnp.repeat(jnp.swapaxes(sgu_b[:, :, :dec_seq], 1, 2), LANES, axis=-1))

    xs = time_major(x_sample)
    even_states_s = (time_major(state_conv_a), time_major(state_pool_b))
    odd_state_s = time_major(state_conv_d)
    sample = dict(tt=dec_seq, ts=dec_seq, r=SAMPLE_SEQS)
    new_even_p = new_even_s = new_odd_p = new_odd_s = None
    w_in_bf, w_out_bf = w_in_even[0].astype(bf16), w_out_even[0].astype(bf16)
    for l in range(depth):
        i = l // 2
        stacks = (w_in_odd, w_out_odd, i) if l % 2 == 0 else (w_in_even, w_out_even, i + 1)
        next_w = stacks if l + 1 < depth else None
        if l % 2 == 0:
            common = dict(layer=l, idx=i, n_layers=n_even)
            params = (norm_pre, norm_post, w_in_bf, w_out_bf) + even_tail
            outs = _even_layer(xp, None, params, new_even_p, start_pos=0, tt=PROMPT_TILE, ts=EVEN_SUB, r=1,
                               next_w=next_w, **common)
            xp, new_even_p = outs[0], outs[1:3]
            xs, *new_even_s = _even_layer(xs, even_states_s, params, new_even_s, start_pos=PAST_LEN,
                                          **common, **sample)
        else:
            common = dict(layer=l, idx=i, n_layers=n_odd)
            head = (norm_pre, norm_post, w_in_bf, w_out_bf, sgu_ln_g, sgu_ln_b)
            outs = _odd_layer(xp, None, head + mix_p + odd_tail, new_odd_p, emit_vn=False, tt=PROMPT_TILE, ts=ODD_SUB,
                              r=1, next_w=next_w, **common)
            xp, new_odd_p = outs[0], outs[1:2]
            xs, *new_odd_s = _odd_layer(xs, odd_state_s, head + mix_s + odd_tail, new_odd_s, emit_vn=True,
                                        y_native=l + 1 == depth, **common, **sample)
        if next_w is not None:
            w_in_bf, w_out_bf = outs[-2:]
    ca_p, pb_p = new_even_p
    ca_s, pb_s = new_even_s
    (cd_p,) = new_odd_p
    cd_s, v_s = new_odd_s
    return (xp, xs, ca_p, time_major(ca_s), pb_p, time_major(pb_s),
            cd_p, time_major(cd_s), time_major(v_s))
```

```python
import functools

import jax
import jax.numpy as jnp
from jax import lax
from jax.experimental import pallas as pl
from jax.experimental.pallas import tpu as pltpu

D_MODEL = 1024
BRANCH = 512
LANES = 128
N_SLABS = BRANCH // LANES
IN_WIDTH = 6 * BRANCH
CONV_A = 3
POOL_WINDOWS = (2, 4, 8, 16)
POOL_HIST = max(POOL_WINDOWS) - 1
CHUNK = 128
CONV_D = 31
PAST_LEN = 16384
EPS = 1e-6

PAD_A = 8
PAD_B = 16
PAD_D = 32

VMEM_LIMIT_BYTES = 56 * 1024 * 1024

PROMPT_TILE = 1024
EVEN_SUB = 256
ODD_SUB = 512
SAMPLE_SEQS = 64


def _rms_norm(x, g):
    ms = jnp.mean(x * x, axis=-1, keepdims=True)
    return x * lax.rsqrt(ms + EPS) * g


def _layer_norm(x, g, b):
    mu = jnp.mean(x, axis=-1, keepdims=True)
    xc = x - mu
    var = jnp.mean(xc * xc, axis=-1, keepdims=True)
    return xc * lax.rsqrt(var + EPS) * g + b


def _sigmoid(x):
    return 0.5 * jnp.tanh(0.5 * x) + 0.5


def _silu(x):
    hx = 0.5 * x
    return hx * jnp.tanh(hx) + hx


def _slab(c):
    return slice(c * LANES, (c + 1) * LANES)


def _row(ref, i):
    return ref[i:i + 1, :]


def _get_rows(ref, t0, ts, r):
    if r == 1:
        return ref[0, t0:t0 + ts, :]
    return ref[t0:t0 + ts].reshape(ts * r, ref.shape[-1])


def _set_rows(ref, t0, ts, r, val):
    if r == 1:
        ref[0, t0:t0 + ts, :] = val
    else:
        ref[t0:t0 + ts] = val.reshape(ts, r, ref.shape[-1])


def _carry_history(ext, hist_ref, pad, n_hist, tt, n_t, r):
    j = pl.program_id(1)
    lo, hi = (pad - n_hist) * r, pad * r

    @pl.when(j == 0)
    def _():
        for c in range(N_SLABS):
            if hist_ref is None:
                ext[c, lo:hi, :] = jnp.zeros((hi - lo, LANES), ext.dtype)
            else:
                ext[c, lo:hi, :] = hist_ref[:, :, _slab(c)].reshape(hi - lo, LANES)

    if n_t > 1:
        @pl.when(j > 0)
        def _():
            for c in range(N_SLABS):
                ext[c, lo:hi, :] = ext[c, lo + tt * r:hi + tt * r, :]


def _zero_other_layers(ref, fill):
    if fill is not None:
        for o in range(fill[1]):
            if o != fill[0]:
                ref[o] = jnp.zeros(ref.shape[1:], ref.dtype)


def _emit_history(ext, new_ref, pad, n_hist, tt, n_t, r, fill):
    @pl.when(pl.program_id(1) == n_t - 1)
    def _():
        dst = new_ref if fill is None else new_ref.at[fill[0]]
        for c in range(N_SLABS):
            last = ext[c, (pad + tt - n_hist) * r:(pad + tt) * r, :]
            if r == 1:
                dst[0, :, _slab(c)] = last
            else:
                dst[:, :, _slab(c)] = last.reshape(n_hist, r, LANES)
        _zero_other_layers(new_ref, fill)


def _conv_slab(ext, w_ref, c, pad, t0, ts, n_taps, r):
    acc = None
    for k in range(n_taps):
        lo = (pad + t0 - (n_taps - 1) + k) * r
        term = ext[c, lo:lo + ts * r, :] * w_ref[k:k + 1, _slab(c)]
        acc = term if acc is None else acc + term
    return acc


def _store_slabs(ext, val, pad, t0, ts, r):
    for c in range(N_SLABS):
        ext[c, (pad + t0) * r:(pad + t0 + ts) * r, :] = val[:, _slab(c)]


def _half_out(wout_ref, y, half):
    return jnp.dot(y.astype(jnp.bfloat16), wout_ref[half * BRANCH:(half + 1) * BRANCH, :],
                   preferred_element_type=jnp.float32)


def _fetch_native_x(x_hbm, xbuf, sems, tt, r):
    b = pl.program_id(0)
    copies = lambda blk: [pltpu.make_async_copy(x_hbm.at[pl.ds(blk * r, r), t, :], xbuf.at[blk, t], sems.at[blk, t])
                          for t in range(tt)]

    @pl.when(b == 0)
    def _():
        for blk in range(xbuf.shape[0]):
            for c in copies(blk):
                c.start()

    for blk in range(xbuf.shape[0]):
        @pl.when(b == blk)
        def _():
            for c in copies(blk):
                c.wait()

    return xbuf.at[b]


def _even_kernel(*refs, tt, ts, r, n_t, start_pos, has_hist, n_alias, fill, layer, idx, x_native):
    refs = list(refs)
    x_ref = refs.pop(0)
    if x_native:
        x_sems, xbuf = refs.pop(), refs.pop()
        x_ref = _fetch_native_x(x_ref, xbuf, x_sems, tt, r)
    ha_ref, hb_ref = (refs.pop(0), refs.pop(0)) if has_hist else (None, None)
    pre_ref, post_ref, win_ref, wout_ref, cw_ref, pw_ref, ps_ref = refs[:7]
    y_ref, na_ref, nb_ref, ext_a, ext_b = refs[7 + n_alias:]
    _carry_history(ext_a, ha_ref, PAD_A, CONV_A - 1, tt, n_t, r)
    _carry_history(ext_b, hb_ref, PAD_B, POOL_HIST, tt, n_t, r)
    subs = list(range(0, tt, ts))
    ms = ts * r

    z = {}
    for t0 in subs:
        h = _rms_norm(_get_rows(x_ref, t0, ts, r), _row(pre_ref, layer)).astype(jnp.bfloat16)
        z[t0] = jnp.dot(h, win_ref[...], preferred_element_type=jnp.float32)
    part = lambda t0, i: z[t0][:, i * BRANCH:(i + 1) * BRANCH]

    ya, yb = {}, {}
    for t0 in subs:
        _store_slabs(ext_a, part(t0, 1) * part(t0, 2), PAD_A, t0, ts, r)
        conv = jnp.concatenate([_conv_slab(ext_a, cw_ref, c, PAD_A, t0, ts, CONV_A, r) for c in range(N_SLABS)],
                               axis=-1)
        ya[t0] = part(t0, 0) * conv * _silu(part(t0, 3))
    for t0 in subs:
        p = part(t0, 4)
        _store_slabs(ext_b, p, PAD_B, t0, ts, r)
        step = lax.broadcasted_iota(jnp.int32, (ms, LANES), 0) // r
        pos = start_pos + pl.program_id(1) * tt + t0 + step
        ds = []
        for g, w in enumerate(POOL_WINDOWS):
            acc = ext_b[g, (PAD_B + t0) * r:(PAD_B + t0 + ts) * r, :]
            for i in range(1, w):
                acc = acc + ext_b[g, (PAD_B + t0 - i) * r:(PAD_B + t0 - i + ts) * r, :]
            cnt = jnp.minimum(pos + 1, w).astype(jnp.float32)
            ds.append((acc / cnt - p[:, _slab(g)]).astype(jnp.bfloat16))
        mixed = []
        zero = jnp.zeros((LANES, LANES), jnp.bfloat16)
        for g in range(0, N_SLABS, 2):
            w2 = jnp.concatenate([jnp.concatenate([pw_ref[g], zero], axis=1),
                                  jnp.concatenate([zero, pw_ref[g + 1]], axis=1)], axis=0)
            mixed.append(jnp.dot(jnp.concatenate(ds[g:g + 2], axis=1), w2, preferred_element_type=jnp.float32))
        yb[t0] = jnp.concatenate(mixed, axis=-1) * _row(ps_ref, idx) * _silu(part(t0, 5))
    for t0 in subs:
        o = _half_out(wout_ref, ya[t0], 0) + _half_out(wout_ref, yb[t0], 1)
        _set_rows(y_ref, t0, ts, r, _get_rows(x_ref, t0, ts, r) + _rms_norm(o, _row(post_ref, layer)))

    _emit_history(ext_a, na_ref, PAD_A, CONV_A - 1, tt, n_t, r, fill)
    _emit_history(ext_b, nb_ref, PAD_B, POOL_HIST, tt, n_t, r, fill)


def _mix_chunks(vn, ws_ref, bs_ref):
    row = lax.broadcasted_iota(jnp.int32, (CHUNK, CHUNK), 0)
    col = lax.broadcasted_iota(jnp.int32, (CHUNK, CHUNK), 1)
    wm = [jnp.where(col <= row, ws_ref[g], 0.0).astype(jnp.bfloat16) for g in range(N_SLABS)]
    vnb = vn.astype(jnp.bfloat16)
    n_ch = vn.shape[0] // CHUNK
    outs = []
    for g in range(N_SLABS):
        rhs = jnp.concatenate([vnb[ch * CHUNK:(ch + 1) * CHUNK, _slab(g)] for ch in range(n_ch)], axis=1)
        outs.append(jnp.dot(wm[g], rhs, preferred_element_type=jnp.float32) + bs_ref[:, g:g + 1])
    return jnp.concatenate(
        [jnp.concatenate([outs[g][:, _slab(ch)] for g in range(N_SLABS)], axis=1) for ch in range(n_ch)], axis=0)


def _mix_steps(vn, ws_ref, bs_ref, n_steps, r):
    out = []
    for i in range(n_steps):
        acc = None
        for j in range(i + 1):
            term = vn[j * r:(j + 1) * r, :] * ws_ref[i, j:j + 1, :]
            acc = term if acc is None else acc + term
        out.append(acc + bs_ref[i:i + 1, :])
    return jnp.concatenate(out, axis=0)


def _odd_kernel(*refs, tt, ts, r, n_t, has_hist, emit_vn, n_alias, fill, layer, idx, y_native):
    refs = list(refs)
    x_ref = refs.pop(0)
    hd_ref = refs.pop(0) if has_hist else None
    (pre_ref, post_ref, win_ref, wout_ref, lng_ref, lnb_ref, ws_ref, bs_ref,
     dw_ref, db_ref, cg_ref, cb_ref) = refs[:12]
    outs = refs[12 + n_alias:]
    if y_native:
        y_hbm, y_sems = outs[0], outs.pop()
        outs[0] = outs.pop()
    if emit_vn:
        y_ref, nd_ref, vn_ref, ext_d, h_ref = outs
    else:
        y_ref, nd_ref, ext_d, h_ref = outs
    _carry_history(ext_d, hd_ref, PAD_D, CONV_D - 1, tt, n_t, r)
    subs = list(range(0, tt, ts))
    hrows = lambda t0: slice(t0 * r, (t0 + ts) * r)

    for t0 in subs:
        h_ref[hrows(t0), :] = _rms_norm(_get_rows(x_ref, t0, ts, r), _row(pre_ref, layer)).astype(jnp.bfloat16)

    conv_out = {t0: [] for t0 in subs}
    for c in range(N_SLABS):
        w_c = jnp.concatenate([win_ref[:, 3 * BRANCH + c * LANES:3 * BRANCH + (c + 1) * LANES],
                               win_ref[:, 4 * BRANCH + c * LANES:4 * BRANCH + (c + 1) * LANES]], axis=1)
        for t0 in subs:
            zc = jnp.dot(h_ref[hrows(t0), :], w_c, preferred_element_type=jnp.float32)
            ext_d[c, (PAD_D + t0) * r:(PAD_D + t0 + ts) * r, :] = zc[:, :LANES] * _sigmoid(zc[:, LANES:])
            conv_out[t0].append(_conv_slab(ext_d, dw_ref, c, PAD_D, t0, ts, CONV_D, r))

    z3 = {t0: jnp.dot(h_ref[hrows(t0), :], win_ref[:, :3 * BRANCH], preferred_element_type=jnp.float32)
          for t0 in subs}
    gd = {t0: jnp.dot(h_ref[hrows(t0), :], win_ref[:, 5 * BRANCH:], preferred_element_type=jnp.float32)
          for t0 in subs}

    o1, yd = {}, {}
    for t0 in subs:
        u, v, gc = (z3[t0][:, i * BRANCH:(i + 1) * BRANCH] for i in range(3))
        vn = _layer_norm(v, _row(lng_ref, idx), _row(lnb_ref, idx))
        if emit_vn:
            _set_rows(vn_ref if fill is None else vn_ref.at[fill[0]], t0, ts, r, vn)
        mixed = _mix_chunks(vn, ws_ref, bs_ref) if r == 1 else _mix_steps(vn, ws_ref, bs_ref, ts, r)
        o1[t0] = _half_out(wout_ref, u * mixed * _silu(gc), 0)
    for t0 in subs:
        y = jnp.concatenate(conv_out[t0], axis=-1) + _row(db_ref, idx)
        yd[t0] = _silu(_layer_norm(y, _row(cg_ref, idx), _row(cb_ref, idx))) * _silu(gd[t0])
    for t0 in subs:
        o = o1[t0] + _half_out(wout_ref, yd[t0], 1)
        _set_rows(y_ref, t0, ts, r, _get_rows(x_ref, t0, ts, r) + _rms_norm(o, _row(post_ref, layer)))

    if y_native:
        seqs = pl.ds(pl.multiple_of(pl.program_id(0) * r, r), r)
        sends = [pltpu.make_async_copy(y_ref.at[t], y_hbm.at[seqs, t, :], y_sems.at[t]) for t in range(tt)]
        for c in sends:
            c.start()
        for c in sends:
            c.wait()
    if emit_vn:
        _zero_other_layers(vn_ref, fill)
    _emit_history(ext_d, nd_ref, PAD_D, CONV_D - 1, tt, n_t, r, fill)


def _chunk_copies(w, bf, stage, obuf, sems, kind, idx, k):
    rows = stage.shape[1]
    sl = pl.ds(pl.multiple_of(k * rows, rows), rows)
    slot = k % 2
    fetch = pltpu.make_async_copy(w.at[idx, sl, :], stage.at[slot], sems.at[2 * kind, slot])
    send = pltpu.make_async_copy(obuf.at[slot], bf.at[sl, :], sems.at[2 * kind + 1, slot])
    return fetch, send


def _cast_next_weights(nw_in, nw_out, bf_in, bf_out, stage_in, stage_out, obuf_in, obuf_out, sems, *, idx, n_steps):
    s = pl.program_id(0) * pl.num_programs(1) + pl.program_id(1)
    pairs = ((nw_in, bf_in, stage_in, obuf_in, 0), (nw_out, bf_out, stage_out, obuf_out, 1))
    copies = lambda k: [_chunk_copies(w, bf, st, ob, sems, kind, idx, k) for w, bf, st, ob, kind in pairs]

    @pl.when(s == 0)
    def _():
        for fetch, _ in copies(s):
            fetch.start()

    for fetch, _ in copies(s):
        fetch.wait()

    @pl.when(s >= 2)
    def _():
        for _, send in copies(s - 2):
            send.wait()

    slot = s % 2
    obuf_in[slot] = stage_in[slot].astype(jnp.bfloat16)
    obuf_out[slot] = stage_out[slot].astype(jnp.bfloat16)
    for _, send in copies(s):
        send.start()

    @pl.when(s + 1 < n_steps)
    def _():
        for fetch, _ in copies(s + 1):
            fetch.start()

    @pl.when(s == n_steps - 1)
    def _():
        for k in (s - 1, s):
            for _, send in copies(k):
                send.wait()


def _with_next_weights(body, n_in, n_out, n_scratch, idx, n_steps, *refs):
    ins, rest = refs[:n_in], refs[n_in:]
    nw, rest = rest[:2], rest[2:]
    outs, rest = rest[:n_out], rest[n_out:]
    bf, rest = rest[:2], rest[2:]
    scratch, staging = rest[:n_scratch], rest[n_scratch:]
    _cast_next_weights(*nw, *bf, *staging, idx=idx, n_steps=n_steps)
    body(*ins, *outs, *scratch)


def _launch(name, body, grid, ins, in_specs, out_specs, out_shape, scratch, aliases, next_w):
    if next_w is not None:
        nw_in, nw_out, nidx = next_w
        n_steps = grid[0] * grid[1]
        rows = nw_in.shape[1] // n_steps
        assert nw_in.shape[1] == nw_out.shape[1] == rows * n_steps and rows % 16 == 0
        any_spec = pl.BlockSpec(memory_space=pl.ANY)
        body = functools.partial(_with_next_weights, body, len(ins), len(out_specs), len(scratch), nidx, n_steps)
        ins = ins + [nw_in, nw_out]
        in_specs = in_specs + [any_spec, any_spec]
        out_specs = out_specs + [any_spec, any_spec]
        out_shape = out_shape + [jax.ShapeDtypeStruct(w.shape[1:], jnp.bfloat16) for w in (nw_in, nw_out)]
        scratch = scratch + [pltpu.VMEM((2, rows, nw_in.shape[2]), jnp.float32),
                             pltpu.VMEM((2, rows, nw_out.shape[2]), jnp.float32),
                             pltpu.VMEM((2, rows, nw_in.shape[2]), jnp.bfloat16),
                             pltpu.VMEM((2, rows, nw_out.shape[2]), jnp.bfloat16),
                             pltpu.SemaphoreType.DMA((4, 2))]
    return pl.pallas_call(body, grid=grid, in_specs=in_specs, out_specs=out_specs, out_shape=out_shape,
                          scratch_shapes=scratch, input_output_aliases=aliases,
                          compiler_params=_compiler_params(), name=name)(*ins)


def _layer_spec(arr, idx):
    if arr.ndim == 2:
        return pl.BlockSpec(arr.shape, lambda b, j: (0, 0), pipeline_mode=pl.Buffered(1))
    tail = arr.shape[1:]
    return pl.BlockSpec((None,) + tail, lambda b, j: (idx,) + (0,) * len(tail), pipeline_mode=pl.Buffered(1))


def _stacked_spec(rows, r, idx, n_layers, whole):
    lead = n_layers if whole else None
    l0 = 0 if whole else idx
    if r == 1:
        return pl.BlockSpec((lead, 1, rows, BRANCH), lambda b, j: (l0, b, 0, 0))
    return pl.BlockSpec((lead, rows, r, BRANCH), lambda b, j: (l0, 0, b, 0))


def _stacked_shape(n_layers, bsz, rows, r, dtype):
    shape = (n_layers, bsz, rows, BRANCH) if r == 1 else (n_layers, rows, bsz, BRANCH)
    return jax.ShapeDtypeStruct(shape, dtype)


def _x_spec(tt, r):
    if r == 1:
        return pl.BlockSpec((1, tt, D_MODEL), lambda b, j: (b, j, 0))
    return pl.BlockSpec((tt, r, D_MODEL), lambda b, j: (0, b, 0))


def _compiler_params():
    return pltpu.CompilerParams(dimension_semantics=("arbitrary", "arbitrary"),
                                vmem_limit_bytes=VMEM_LIMIT_BYTES)


def _alias_inputs(ins, in_specs, prev_new):
    aliases = {}
    for k, arr in enumerate(prev_new or ()):
        aliases[len(ins)] = 1 + k
        ins.append(arr)
        in_specs.append(pl.BlockSpec(memory_space=pl.ANY))
    return aliases


def _even_layer(x, states, params, prev_new, *, layer, idx, n_layers, tt, ts, r, start_pos, next_w=None,
                x_native=False):
    bsz, t = (x.shape[0], x.shape[1]) if r == 1 or x_native else (x.shape[1], x.shape[0])
    n_t = t // tt
    first = prev_new is None
    pre_g, post_g, w_in, w_out, conv_w, pool_w, pool_scale = params
    ins, in_specs = [x], [pl.BlockSpec(memory_space=pl.ANY) if x_native else _x_spec(tt, r)]
    if states is not None:
        ins += list(states)
        in_specs += [_stacked_spec(CONV_A - 1, r, idx, n_layers, False),
                     _stacked_spec(POOL_HIST, r, idx, n_layers, False)]
    ins += [pre_g, post_g, w_in, w_out, conv_w, pool_w, pool_scale]
    in_specs += [_layer_spec(pre_g, layer), _layer_spec(post_g, layer), _layer_spec(w_in, idx),
                 _layer_spec(w_out, idx), _layer_spec(conv_w, idx), _layer_spec(pool_w, idx),
                 _layer_spec(pool_scale, idx)]
    aliases = _alias_inputs(ins, in_specs, prev_new)
    body = functools.partial(_even_kernel, tt=tt, ts=ts, r=r, n_t=n_t, start_pos=start_pos, has_hist=states is not None,
                             n_alias=len(aliases), fill=(idx, n_layers) if first else None, layer=layer, idx=idx,
                             x_native=x_native)
    out_specs = [_x_spec(tt, r), _stacked_spec(CONV_A - 1, r, idx, n_layers, first),
                 _stacked_spec(POOL_HIST, r, idx, n_layers, first)]
    out_shape = [jax.ShapeDtypeStruct((bsz, t, D_MODEL) if r == 1 else (t, bsz, D_MODEL), x.dtype),
                 _stacked_shape(n_layers, bsz, CONV_A - 1, r, x.dtype),
                 _stacked_shape(n_layers, bsz, POOL_HIST, r, x.dtype)]
    scratch = [pltpu.VMEM((N_SLABS, (PAD_A + tt) * r, LANES), jnp.float32),
               pltpu.VMEM((N_SLABS, (PAD_B + tt) * r, LANES), jnp.float32)]
    if x_native:
        assert r > 1 and n_t == 1
        scratch += [pltpu.VMEM((bsz // r, tt, r, D_MODEL), jnp.float32), pltpu.SemaphoreType.DMA((bsz // r, tt))]
    return _launch("even_layer", body, (bsz // r, n_t), ins, in_specs, out_specs, out_shape, scratch, aliases, next_w)


def _odd_layer(x, state, params, prev_new, *, layer, idx, n_layers, tt, ts, r, emit_vn, next_w=None, y_native=False):
    bsz, t = (x.shape[0], x.shape[1]) if r == 1 else (x.shape[1], x.shape[0])
    n_t = t // tt
    first = prev_new is None
    ins, in_specs = [x], [_x_spec(tt, r)]
    if state is not None:
        ins.append(state)
        in_specs.append(_stacked_spec(CONV_D - 1, r, idx, n_layers, False))
    for k, arr in enumerate(params):
        ins.append(arr)
        in_specs.append(_layer_spec(arr, layer if k < 2 else idx))
    out_specs = [_x_spec(tt, r), _stacked_spec(CONV_D - 1, r, idx, n_layers, first)]
    out_shape = [jax.ShapeDtypeStruct(x.shape, x.dtype), _stacked_shape(n_layers, bsz, CONV_D - 1, r, x.dtype)]
    if emit_vn:
        out_specs.append(_stacked_spec(tt, r, idx, n_layers, first))
        out_shape.append(_stacked_shape(n_layers, bsz, t, r, x.dtype))
    aliases = _alias_inputs(ins, in_specs, prev_new)
    body = functools.partial(_odd_kernel, tt=tt, ts=ts, r=r, n_t=n_t, has_hist=state is not None, emit_vn=emit_vn,
                             n_alias=len(aliases), fill=(idx, n_layers) if first else None, layer=layer, idx=idx,
                             y_native=y_native)
    scratch = [pltpu.VMEM((N_SLABS, (PAD_D + tt) * r, LANES), jnp.float32),
               pltpu.VMEM((tt * r, D_MODEL), jnp.bfloat16)]
    if y_native:
        assert r > 1 and n_t == 1
        out_specs[0] = pl.BlockSpec(memory_space=pl.ANY)
        out_shape[0] = jax.ShapeDtypeStruct((bsz, t, D_MODEL), x.dtype)
        scratch += [pltpu.VMEM((tt, r, D_MODEL), jnp.float32), pltpu.SemaphoreType.DMA((tt,))]
    return _launch("odd_layer", body, (bsz // r, n_t), ins, in_specs, out_specs, out_shape, scratch, aliases, next_w)


def kernel(x_prompt, x_sample, state_conv_a, state_pool_b, state_conv_d, norm_pre, norm_post, w_in_even, w_out_even, conv_a_w, pool_w, pool_scale, w_in_odd, w_out_odd, sgu_ln_g, sgu_ln_b, sgu_w, sgu_b, conf_dw_w, conf_dw_b, conf_ln_g, conf_ln_b):
    xp = x_prompt
    dec_seq = x_sample.shape[1]
    depth = norm_pre.shape[0]
    n_even, n_odd = w_in_even.shape[0], w_in_odd.shape[0]
    bf16 = jnp.bfloat16
    time_major = lambda a: jnp.swapaxes(a, -3, -2)

    even_tail = (conv_a_w, pool_w.astype(bf16), pool_scale)
    odd_tail = (conf_dw_w, conf_dw_b, conf_ln_g, conf_ln_b)
    mix_p = (sgu_w, jnp.swapaxes(sgu_b, 1, 2))
    mix_s = (jnp.repeat(jnp.transpose(sgu_w[:, :, :dec_seq, :dec_seq], (0, 2, 3, 1)), LANES, axis=-1),
             jnp.repeat(jnp.swapaxes(sgu_b[:, :, :dec_seq], 1, 2), LANES, axis=-1))

    xs = x_sample
    even_states_s = (time_major(state_conv_a), time_major(state_pool_b))
    odd_state_s = time_major(state_conv_d)
    sample = dict(tt=dec_seq, ts=dec_seq, r=SAMPLE_SEQS)
    new_even_p = new_even_s = new_odd_p = new_odd_s = None
    w_in_bf, w_out_bf = w_in_even[0].astype(bf16), w_out_even[0].astype(bf16)
    for l in range(depth):
        i = l // 2
        stacks = (w_in_odd, w_out_odd, i) if l % 2 == 0 else (w_in_even, w_out_even, i + 1)
        next_w = stacks if l + 1 < depth else None
        if l % 2 == 0:
            common = dict(layer=l, idx=i, n_layers=n_even)
            params = (norm_pre, norm_post, w_in_bf, w_out_bf) + even_tail
            outs = _even_layer(xp, None, params, new_even_p, start_pos=0, tt=PROMPT_TILE, ts=EVEN_SUB, r=1,
                               next_w=next_w, **common)
            xp, new_even_p = outs[0], outs[1:3]
            xs, *new_even_s = _even_layer(xs, even_states_s, params, new_even_s, start_pos=PAST_LEN,
                                          x_native=l == 0, **common, **sample)
        else:
            common = dict(layer=l, idx=i, n_layers=n_odd)
            head = (norm_pre, norm_post, w_in_bf, w_out_bf, sgu_ln_g, sgu_ln_b)
            outs = _odd_layer(xp, None, head + mix_p + odd_tail, new_odd_p, emit_vn=False, tt=PROMPT_TILE, ts=ODD_SUB,
                              r=1, next_w=next_w, **common)
            xp, new_odd_p = outs[0], outs[1:2]
            xs, *new_odd_s = _odd_layer(xs, odd_state_s, head + mix_s + odd_tail, new_odd_s, emit_vn=True,
                                        y_native=l + 1 == depth, **common, **sample)
        if next_w is not None:
            w_in_bf, w_out_bf = outs[-2:]
    ca_p, pb_p = new_even_p
    ca_s, pb_s = new_even_s
    (cd_p,) = new_odd_p
    cd_s, v_s = new_odd_s
    return (xp, xs, ca_p, time_major(ca_s), pb_p, time_major(pb_s),
            cd_p, time_major(cd_s), time_major(v_s))
```

```python
import functools

import jax
import jax.numpy as jnp
from jax import lax
from jax.experimental import pallas as pl
from jax.experimental.pallas import tpu as pltpu

D_MODEL = 1024
BRANCH = 512
LANES = 128
N_SLABS = BRANCH // LANES
IN_WIDTH = 6 * BRANCH
CONV_A = 3
POOL_WINDOWS = (2, 4, 8, 16)
POOL_HIST = max(POOL_WINDOWS) - 1
CHUNK = 128
CONV_D = 31
PAST_LEN = 16384
EPS = 1e-6

PAD_A = 8
PAD_B = 16
PAD_D = 32

VMEM_LIMIT_BYTES = 56 * 1024 * 1024

PROMPT_TILE = 1024
EVEN_SUB = 256
ODD_SUB = 512
SAMPLE_SEQS = 64


def _rms_norm(x, g):
    ms = jnp.mean(x * x, axis=-1, keepdims=True)
    return x * lax.rsqrt(ms + EPS) * g


def _layer_norm(x, g, b):
    mu = jnp.mean(x, axis=-1, keepdims=True)
    xc = x - mu
    var = jnp.mean(xc * xc, axis=-1, keepdims=True)
    return xc * lax.rsqrt(var + EPS) * g + b


def _sigmoid(x):
    return 0.5 * jnp.tanh(0.5 * x) + 0.5


def _silu(x):
    hx = 0.5 * x
    return hx * jnp.tanh(hx) + hx


def _slab(c):
    return slice(c * LANES, (c + 1) * LANES)


def _row(ref, i):
    return ref[i:i + 1, :]


def _get_rows(ref, t0, ts, r):
    if r == 1:
        return ref[0, t0:t0 + ts, :]
    return ref[t0:t0 + ts].reshape(ts * r, ref.shape[-1])


def _set_rows(ref, t0, ts, r, val):
    if r == 1:
        ref[0, t0:t0 + ts, :] = val
    else:
        ref[t0:t0 + ts] = val.reshape(ts, r, ref.shape[-1])


def _carry_history(ext, hist_ref, pad, n_hist, tt, n_t, r):
    j = pl.program_id(1)
    lo, hi = (pad - n_hist) * r, pad * r

    @pl.when(j == 0)
    def _():
        for c in range(N_SLABS):
            if hist_ref is None:
                ext[c, lo:hi, :] = jnp.zeros((hi - lo, LANES), ext.dtype)
            else:
                ext[c, lo:hi, :] = hist_ref[:, :, _slab(c)].reshape(hi - lo, LANES)

    if n_t > 1:
        @pl.when(j > 0)
        def _():
            for c in range(N_SLABS):
                ext[c, lo:hi, :] = ext[c, lo + tt * r:hi + tt * r, :]


def _zero_other_layers(ref, fill):
    if fill is not None:
        for o in range(fill[1]):
            if o != fill[0]:
                ref[o] = jnp.zeros(ref.shape[1:], ref.dtype)


def _emit_history(ext, new_ref, pad, n_hist, tt, n_t, r, fill):
    @pl.when(pl.program_id(1) == n_t - 1)
    def _():
        dst = new_ref if fill is None else new_ref.at[fill[0]]
        for c in range(N_SLABS):
            last = ext[c, (pad + tt - n_hist) * r:(pad + tt) * r, :]
            if r == 1:
                dst[0, :, _slab(c)] = last
            else:
                dst[:, :, _slab(c)] = last.reshape(n_hist, r, LANES)
        _zero_other_layers(new_ref, fill)


def _conv_slab(ext, w_ref, c, pad, t0, ts, n_taps, r):
    acc = None
    for k in range(n_taps):
        lo = (pad + t0 - (n_taps - 1) + k) * r
        term = ext[c, lo:lo + ts * r, :] * w_ref[k:k + 1, _slab(c)]
        acc = term if acc is None else acc + term
    return acc


def _store_slabs(ext, val, pad, t0, ts, r):
    for c in range(N_SLABS):
        ext[c, (pad + t0) * r:(pad + t0 + ts) * r, :] = val[:, _slab(c)]


def _half_out(wout_ref, y, half):
    return jnp.dot(y.astype(jnp.bfloat16), wout_ref[half * BRANCH:(half + 1) * BRANCH, :],
                   preferred_element_type=jnp.float32)


def _fetch_native_x(x_hbm, xbuf, sems, tt, r):
    b = pl.program_id(0)
    copies = lambda blk: [pltpu.make_async_copy(x_hbm.at[pl.ds(blk * r, r), t, :], xbuf.at[blk, t], sems.at[blk, t])
                          for t in range(tt)]

    @pl.when(b == 0)
    def _():
        for blk in range(xbuf.shape[0]):
            for c in copies(blk):
                c.start()

    for blk in range(xbuf.shape[0]):
        @pl.when(b == blk)
        def _():
            for c in copies(blk):
                c.wait()

    return xbuf.at[b]


def _even_kernel(*refs, tt, ts, r, n_t, start_pos, has_hist, n_alias, fill, layer, idx, x_native):
    refs = list(refs)
    x_ref = refs.pop(0)
    if x_native:
        x_sems, xbuf = refs.pop(), refs.pop()
        x_ref = _fetch_native_x(x_ref, xbuf, x_sems, tt, r)
    ha_ref, hb_ref = (refs.pop(0), refs.pop(0)) if has_hist else (None, None)
    pre_ref, post_ref, win_ref, wout_ref, cw_ref, pw_ref, ps_ref = refs[:7]
    y_ref, na_ref, nb_ref, ext_a, ext_b = refs[7 + n_alias:]
    _carry_history(ext_a, ha_ref, PAD_A, CONV_A - 1, tt, n_t, r)
    _carry_history(ext_b, hb_ref, PAD_B, POOL_HIST, tt, n_t, r)
    subs = list(range(0, tt, ts))
    ms = ts * r

    z = {}
    for t0 in subs:
        h = _rms_norm(_get_rows(x_ref, t0, ts, r), _row(pre_ref, layer)).astype(jnp.bfloat16)
        z[t0] = jnp.dot(h, win_ref[...], preferred_element_type=jnp.float32)
    part = lambda t0, i: z[t0][:, i * BRANCH:(i + 1) * BRANCH]

    ya, yb = {}, {}
    for t0 in subs:
        _store_slabs(ext_a, part(t0, 1) * part(t0, 2), PAD_A, t0, ts, r)
        conv = jnp.concatenate([_conv_slab(ext_a, cw_ref, c, PAD_A, t0, ts, CONV_A, r) for c in range(N_SLABS)],
                               axis=-1)
        ya[t0] = part(t0, 0) * conv * _silu(part(t0, 3))
    for t0 in subs:
        p = part(t0, 4)
        _store_slabs(ext_b, p, PAD_B, t0, ts, r)
        step = lax.broadcasted_iota(jnp.int32, (ms, LANES), 0) // r
        pos = start_pos + pl.program_id(1) * tt + t0 + step
        ds = []
        for g, w in enumerate(POOL_WINDOWS):
            acc = ext_b[g, (PAD_B + t0) * r:(PAD_B + t0 + ts) * r, :]
            for i in range(1, w):
                acc = acc + ext_b[g, (PAD_B + t0 - i) * r:(PAD_B + t0 - i + ts) * r, :]
            cnt = jnp.minimum(pos + 1, w).astype(jnp.float32)
            ds.append((acc / cnt - p[:, _slab(g)]).astype(jnp.bfloat16))
        mixed = []
        zero = jnp.zeros((LANES, LANES), jnp.bfloat16)
        for g in range(0, N_SLABS, 2):
            w2 = jnp.concatenate([jnp.concatenate([pw_ref[g], zero], axis=1),
                                  jnp.concatenate([zero, pw_ref[g + 1]], axis=1)], axis=0)
            mixed.append(jnp.dot(jnp.concatenate(ds[g:g + 2], axis=1), w2, preferred_element_type=jnp.float32))
        yb[t0] = jnp.concatenate(mixed, axis=-1) * _row(ps_ref, idx) * _silu(part(t0, 5))
    for t0 in subs:
        o = _half_out(wout_ref, ya[t0], 0) + _half_out(wout_ref, yb[t0], 1)
        _set_rows(y_ref, t0, ts, r, _get_rows(x_ref, t0, ts, r) + _rms_norm(o, _row(post_ref, layer)))

    _emit_history(ext_a, na_ref, PAD_A, CONV_A - 1, tt, n_t, r, fill)
    _emit_history(ext_b, nb_ref, PAD_B, POOL_HIST, tt, n_t, r, fill)


def _mix_chunks(vn, ws_ref, bs_ref):
    row = lax.broadcasted_iota(jnp.int32, (CHUNK, CHUNK), 0)
    col = lax.broadcasted_iota(jnp.int32, (CHUNK, CHUNK), 1)
    wm = [jnp.where(col <= row, ws_ref[g], 0.0).astype(jnp.bfloat16) for g in range(N_SLABS)]
    vnb = vn.astype(jnp.bfloat16)
    n_ch = vn.shape[0] // CHUNK
    outs = []
    for g in range(N_SLABS):
        rhs = jnp.concatenate([vnb[ch * CHUNK:(ch + 1) * CHUNK, _slab(g)] for ch in range(n_ch)], axis=1)
        outs.append(jnp.dot(wm[g], rhs, preferred_element_type=jnp.float32) + bs_ref[:, g:g + 1])
    return jnp.concatenate(
        [jnp.concatenate([outs[g][:, _slab(ch)] for g in range(N_SLABS)], axis=1) for ch in range(n_ch)], axis=0)


def _mix_steps(vn, ws_ref, bs_ref, n_steps, r):
    out = []
    for i in range(n_steps):
        acc = None
        for j in range(i + 1):
            term = vn[j * r:(j + 1) * r, :] * ws_ref[i, j:j + 1, :]
            acc = term if acc is None else acc + term
        out.append(acc + bs_ref[i:i + 1, :])
    return jnp.concatenate(out, axis=0)


def _odd_kernel(*refs, tt, ts, r, n_t, has_hist, emit_vn, n_alias, fill, layer, idx, y_native):
    refs = list(refs)
    x_ref = refs.pop(0)
    hd_ref = refs.pop(0) if has_hist else None
    (pre_ref, post_ref, win_ref, wout_ref, lng_ref, lnb_ref, ws_ref, bs_ref,
     dw_ref, db_ref, cg_ref, cb_ref) = refs[:12]
    outs = refs[12 + n_alias:]
    if y_native:
        y_hbm, y_sems = outs[0], outs.pop()
        outs[0] = outs.pop()
    if emit_vn:
        vn_hbm, vn_sems = outs[2], outs.pop()
        outs[2] = outs.pop()
        y_ref, nd_ref, vn_ref, ext_d, h_ref = outs
    else:
        y_ref, nd_ref, ext_d, h_ref = outs
    _carry_history(ext_d, hd_ref, PAD_D, CONV_D - 1, tt, n_t, r)
    subs = list(range(0, tt, ts))
    hrows = lambda t0: slice(t0 * r, (t0 + ts) * r)

    for t0 in subs:
        h_ref[hrows(t0), :] = _rms_norm(_get_rows(x_ref, t0, ts, r), _row(pre_ref, layer)).astype(jnp.bfloat16)

    conv_out = {t0: [] for t0 in subs}
    for c in range(N_SLABS):
        w_c = jnp.concatenate([win_ref[:, 3 * BRANCH + c * LANES:3 * BRANCH + (c + 1) * LANES],
                               win_ref[:, 4 * BRANCH + c * LANES:4 * BRANCH + (c + 1) * LANES]], axis=1)
        for t0 in subs:
            zc = jnp.dot(h_ref[hrows(t0), :], w_c, preferred_element_type=jnp.float32)
            ext_d[c, (PAD_D + t0) * r:(PAD_D + t0 + ts) * r, :] = zc[:, :LANES] * _sigmoid(zc[:, LANES:])
            conv_out[t0].append(_conv_slab(ext_d, dw_ref, c, PAD_D, t0, ts, CONV_D, r))

    z3 = {t0: jnp.dot(h_ref[hrows(t0), :], win_ref[:, :3 * BRANCH], preferred_element_type=jnp.float32)
          for t0 in subs}
    gd = {t0: jnp.dot(h_ref[hrows(t0), :], win_ref[:, 5 * BRANCH:], preferred_element_type=jnp.float32)
          for t0 in subs}

    o1, yd = {}, {}
    for t0 in subs:
        u, v, gc = (z3[t0][:, i * BRANCH:(i + 1) * BRANCH] for i in range(3))
        vn = _layer_norm(v, _row(lng_ref, idx), _row(lnb_ref, idx))
        if emit_vn:
            _set_rows(vn_ref, t0, ts, r, vn)
        mixed = _mix_chunks(vn, ws_ref, bs_ref) if r == 1 else _mix_steps(vn, ws_ref, bs_ref, ts, r)
        o1[t0] = _half_out(wout_ref, u * mixed * _silu(gc), 0)
    for t0 in subs:
        y = jnp.concatenate(conv_out[t0], axis=-1) + _row(db_ref, idx)
        yd[t0] = _silu(_layer_norm(y, _row(cg_ref, idx), _row(cb_ref, idx))) * _silu(gd[t0])
    for t0 in subs:
        o = o1[t0] + _half_out(wout_ref, yd[t0], 1)
        _set_rows(y_ref, t0, ts, r, _get_rows(x_ref, t0, ts, r) + _rms_norm(o, _row(post_ref, layer)))

    if y_native:
        seqs = pl.ds(pl.multiple_of(pl.program_id(0) * r, r), r)
        sends = [pltpu.make_async_copy(y_ref.at[t], y_hbm.at[seqs, t, :], y_sems.at[t]) for t in range(tt)]
        for c in sends:
            c.start()
        for c in sends:
            c.wait()
    if emit_vn:
        seqs = pl.ds(pl.multiple_of(pl.program_id(0) * r, r), r)

        def send_all(layer_slot):
            sends = [pltpu.make_async_copy(vn_ref.at[t], vn_hbm.at[layer_slot, seqs, t, :], vn_sems.at[t])
                     for t in range(tt)]
            for c in sends:
                c.start()
            for c in sends:
                c.wait()

        send_all(idx)
        if fill is not None:
            vn_ref[...] = jnp.zeros(vn_ref.shape, vn_ref.dtype)
            for o in range(fill[1]):
                if o != fill[0]:
                    send_all(o)
    _emit_history(ext_d, nd_ref, PAD_D, CONV_D - 1, tt, n_t, r, fill)


def _chunk_copies(w, bf, stage, obuf, sems, kind, idx, k):
    rows = stage.shape[1]
    sl = pl.ds(pl.multiple_of(k * rows, rows), rows)
    slot = k % 2
    fetch = pltpu.make_async_copy(w.at[idx, sl, :], stage.at[slot], sems.at[2 * kind, slot])
    send = pltpu.make_async_copy(obuf.at[slot], bf.at[sl, :], sems.at[2 * kind + 1, slot])
    return fetch, send


def _cast_next_weights(nw_in, nw_out, bf_in, bf_out, stage_in, stage_out, obuf_in, obuf_out, sems, *, idx, n_steps):
    s = pl.program_id(0) * pl.num_programs(1) + pl.program_id(1)
    pairs = ((nw_in, bf_in, stage_in, obuf_in, 0), (nw_out, bf_out, stage_out, obuf_out, 1))
    copies = lambda k: [_chunk_copies(w, bf, st, ob, sems, kind, idx, k) for w, bf, st, ob, kind in pairs]

    @pl.when(s == 0)
    def _():
        for fetch, _ in copies(s):
            fetch.start()

    for fetch, _ in copies(s):
        fetch.wait()

    @pl.when(s >= 2)
    def _():
        for _, send in copies(s - 2):
            send.wait()

    slot = s % 2
    obuf_in[slot] = stage_in[slot].astype(jnp.bfloat16)
    obuf_out[slot] = stage_out[slot].astype(jnp.bfloat16)
    for _, send in copies(s):
        send.start()

    @pl.when(s + 1 < n_steps)
    def _():
        for fetch, _ in copies(s + 1):
            fetch.start()

    @pl.when(s == n_steps - 1)
    def _():
        for k in (s - 1, s):
            for _, send in copies(k):
                send.wait()


def _with_next_weights(body, n_in, n_out, n_scratch, idx, n_steps, *refs):
    ins, rest = refs[:n_in], refs[n_in:]
    nw, rest = rest[:2], rest[2:]
    outs, rest = rest[:n_out], rest[n_out:]
    bf, rest = rest[:2], rest[2:]
    scratch, staging = rest[:n_scratch], rest[n_scratch:]
    _cast_next_weights(*nw, *bf, *staging, idx=idx, n_steps=n_steps)
    body(*ins, *outs, *scratch)


def _launch(name, body, grid, ins, in_specs, out_specs, out_shape, scratch, aliases, next_w):
    if next_w is not None:
        nw_in, nw_out, nidx = next_w
        n_steps = grid[0] * grid[1]
        rows = nw_in.shape[1] // n_steps
        assert nw_in.shape[1] == nw_out.shape[1] == rows * n_steps and rows % 16 == 0
        any_spec = pl.BlockSpec(memory_space=pl.ANY)
        body = functools.partial(_with_next_weights, body, len(ins), len(out_specs), len(scratch), nidx, n_steps)
        ins = ins + [nw_in, nw_out]
        in_specs = in_specs + [any_spec, any_spec]
        out_specs = out_specs + [any_spec, any_spec]
        out_shape = out_shape + [jax.ShapeDtypeStruct(w.shape[1:], jnp.bfloat16) for w in (nw_in, nw_out)]
        scratch = scratch + [pltpu.VMEM((2, rows, nw_in.shape[2]), jnp.float32),
                             pltpu.VMEM((2, rows, nw_out.shape[2]), jnp.float32),
                             pltpu.VMEM((2, rows, nw_in.shape[2]), jnp.bfloat16),
                             pltpu.VMEM((2, rows, nw_out.shape[2]), jnp.bfloat16),
                             pltpu.SemaphoreType.DMA((4, 2))]
    return pl.pallas_call(body, grid=grid, in_specs=in_specs, out_specs=out_specs, out_shape=out_shape,
                          scratch_shapes=scratch, input_output_aliases=aliases,
                          compiler_params=_compiler_params(), name=name)(*ins)


def _layer_spec(arr, idx):
    if arr.ndim == 2:
        return pl.BlockSpec(arr.shape, lambda b, j: (0, 0), pipeline_mode=pl.Buffered(1))
    tail = arr.shape[1:]
    return pl.BlockSpec((None,) + tail, lambda b, j: (idx,) + (0,) * len(tail), pipeline_mode=pl.Buffered(1))


def _stacked_spec(rows, r, idx, n_layers, whole):
    lead = n_layers if whole else None
    l0 = 0 if whole else idx
    if r == 1:
        return pl.BlockSpec((lead, 1, rows, BRANCH), lambda b, j: (l0, b, 0, 0))
    return pl.BlockSpec((lead, rows, r, BRANCH), lambda b, j: (l0, 0, b, 0))


def _stacked_shape(n_layers, bsz, rows, r, dtype):
    shape = (n_layers, bsz, rows, BRANCH) if r == 1 else (n_layers, rows, bsz, BRANCH)
    return jax.ShapeDtypeStruct(shape, dtype)


def _x_spec(tt, r):
    if r == 1:
        return pl.BlockSpec((1, tt, D_MODEL), lambda b, j: (b, j, 0))
    return pl.BlockSpec((tt, r, D_MODEL), lambda b, j: (0, b, 0))


def _compiler_params():
    return pltpu.CompilerParams(dimension_semantics=("arbitrary", "arbitrary"),
                                vmem_limit_bytes=VMEM_LIMIT_BYTES)


def _alias_inputs(ins, in_specs, prev_new):
    aliases = {}
    for k, arr in enumerate(prev_new or ()):
        aliases[len(ins)] = 1 + k
        ins.append(arr)
        in_specs.append(pl.BlockSpec(memory_space=pl.ANY))
    return aliases


def _even_layer(x, states, params, prev_new, *, layer, idx, n_layers, tt, ts, r, start_pos, next_w=None,
                x_native=False):
    bsz, t = (x.shape[0], x.shape[1]) if r == 1 or x_native else (x.shape[1], x.shape[0])
    n_t = t // tt
    first = prev_new is None
    pre_g, post_g, w_in, w_out, conv_w, pool_w, pool_scale = params
    ins, in_specs = [x], [pl.BlockSpec(memory_space=pl.ANY) if x_native else _x_spec(tt, r)]
    if states is not None:
        ins += list(states)
        in_specs += [_stacked_spec(CONV_A - 1, r, idx, n_layers, False),
                     _stacked_spec(POOL_HIST, r, idx, n_layers, False)]
    ins += [pre_g, post_g, w_in, w_out, conv_w, pool_w, pool_scale]
    in_specs += [_layer_spec(pre_g, layer), _layer_spec(post_g, layer), _layer_spec(w_in, idx),
                 _layer_spec(w_out, idx), _layer_spec(conv_w, idx), _layer_spec(pool_w, idx),
                 _layer_spec(pool_scale, idx)]
    aliases = _alias_inputs(ins, in_specs, prev_new)
    body = functools.partial(_even_kernel, tt=tt, ts=ts, r=r, n_t=n_t, start_pos=start_pos, has_hist=states is not None,
                             n_alias=len(aliases), fill=(idx, n_layers) if first else None, layer=layer, idx=idx,
                             x_native=x_native)
    out_specs = [_x_spec(tt, r), _stacked_spec(CONV_A - 1, r, idx, n_layers, first),
                 _stacked_spec(POOL_HIST, r, idx, n_layers, first)]
    out_shape = [jax.ShapeDtypeStruct((bsz, t, D_MODEL) if r == 1 else (t, bsz, D_MODEL), x.dtype),
                 _stacked_shape(n_layers, bsz, CONV_A - 1, r, x.dtype),
                 _stacked_shape(n_layers, bsz, POOL_HIST, r, x.dtype)]
    scratch = [pltpu.VMEM((N_SLABS, (PAD_A + tt) * r, LANES), jnp.float32),
               pltpu.VMEM((N_SLABS, (PAD_B + tt) * r, LANES), jnp.float32)]
    if x_native:
        assert r > 1 and n_t == 1
        scratch += [pltpu.VMEM((bsz // r, tt, r, D_MODEL), jnp.float32), pltpu.SemaphoreType.DMA((bsz // r, tt))]
    return _launch("even_layer", body, (bsz // r, n_t), ins, in_specs, out_specs, out_shape, scratch, aliases, next_w)


def _odd_layer(x, state, params, prev_new, *, layer, idx, n_layers, tt, ts, r, emit_vn, next_w=None, y_native=False):
    bsz, t = (x.shape[0], x.shape[1]) if r == 1 else (x.shape[1], x.shape[0])
    n_t = t // tt
    first = prev_new is None
    ins, in_specs = [x], [_x_spec(tt, r)]
    if state is not None:
        ins.append(state)
        in_specs.append(_stacked_spec(CONV_D - 1, r, idx, n_layers, False))
    for k, arr in enumerate(params):
        ins.append(arr)
        in_specs.append(_layer_spec(arr, layer if k < 2 else idx))
    out_specs = [_x_spec(tt, r), _stacked_spec(CONV_D - 1, r, idx, n_layers, first)]
    out_shape = [jax.ShapeDtypeStruct(x.shape, x.dtype), _stacked_shape(n_layers, bsz, CONV_D - 1, r, x.dtype)]
    if emit_vn:
        assert r > 1 and n_t == 1
        out_specs.append(pl.BlockSpec(memory_space=pl.ANY))
        out_shape.append(jax.ShapeDtypeStruct((n_layers, bsz, t, BRANCH), x.dtype))
    aliases = _alias_inputs(ins, in_specs, prev_new)
    body = functools.partial(_odd_kernel, tt=tt, ts=ts, r=r, n_t=n_t, has_hist=state is not None, emit_vn=emit_vn,
                             n_alias=len(aliases), fill=(idx, n_layers) if first else None, layer=layer, idx=idx,
                             y_native=y_native)
    scratch = [pltpu.VMEM((N_SLABS, (PAD_D + tt) * r, LANES), jnp.float32),
               pltpu.VMEM((tt * r, D_MODEL), jnp.bfloat16)]
    if emit_vn:
        scratch += [pltpu.VMEM((tt, r, BRANCH), jnp.float32), pltpu.SemaphoreType.DMA((tt,))]
    if y_native:
        assert r > 1 and n_t == 1
        out_specs[0] = pl.BlockSpec(memory_space=pl.ANY)
        out_shape[0] = jax.ShapeDtypeStruct((bsz, t, D_MODEL), x.dtype)
        scratch += [pltpu.VMEM((tt, r, D_MODEL), jnp.float32), pltpu.SemaphoreType.DMA((tt,))]
    return _launch("odd_layer", body, (bsz // r, n_t), ins, in_specs, out_specs, out_shape, scratch, aliases, next_w)


def kernel(x_prompt, x_sample, state_conv_a, state_pool_b, state_conv_d, norm_pre, norm_post, w_in_even, w_out_even, conv_a_w, pool_w, pool_scale, w_in_odd, w_out_odd, sgu_ln_g, sgu_ln_b, sgu_w, sgu_b, conf_dw_w, conf_dw_b, conf_ln_g, conf_ln_b):
    xp = x_prompt
    dec_seq = x_sample.shape[1]
    depth = norm_pre.shape[0]
    n_even, n_odd = w_in_even.shape[0], w_in_odd.shape[0]
    bf16 = jnp.bfloat16
    time_major = lambda a: jnp.swapaxes(a, -3, -2)

    even_tail = (conv_a_w, pool_w.astype(bf16), pool_scale)
    odd_tail = (conf_dw_w, conf_dw_b, conf_ln_g, conf_ln_b)
    mix_p = (sgu_w, jnp.swapaxes(sgu_b, 1, 2))
    mix_s = (jnp.repeat(jnp.transpose(sgu_w[:, :, :dec_seq, :dec_seq], (0, 2, 3, 1)), LANES, axis=-1),
             jnp.repeat(jnp.swapaxes(sgu_b[:, :, :dec_seq], 1, 2), LANES, axis=-1))

    xs = x_sample
    even_states_s = (time_major(state_conv_a), time_major(state_pool_b))
    odd_state_s = time_major(state_conv_d)
    sample = dict(tt=dec_seq, ts=dec_seq, r=SAMPLE_SEQS)
    new_even_p = new_even_s = new_odd_p = new_odd_s = None
    w_in_bf, w_out_bf = w_in_even[0].astype(bf16), w_out_even[0].astype(bf16)
    for l in range(depth):
        i = l // 2
        stacks = (w_in_odd, w_out_odd, i) if l % 2 == 0 else (w_in_even, w_out_even, i + 1)
        next_w = stacks if l + 1 < depth else None
        if l % 2 == 0:
            common = dict(layer=l, idx=i, n_layers=n_even)
            params = (norm_pre, norm_post, w_in_bf, w_out_bf) + even_tail
            outs = _even_layer(xp, None, params, new_even_p, start_pos=0, tt=PROMPT_TILE, ts=EVEN_SUB, r=1,
                               next_w=next_w, **common)
            xp, new_even_p = outs[0], outs[1:3]
            xs, *new_even_s = _even_layer(xs, even_states_s, params, new_even_s, start_pos=PAST_LEN,
                                          x_native=l == 0, **common, **sample)
        else:
            common = dict(layer=l, idx=i, n_layers=n_odd)
            head = (norm_pre, norm_post, w_in_bf, w_out_bf, sgu_ln_g, sgu_ln_b)
            outs = _odd_layer(xp, None, head + mix_p + odd_tail, new_odd_p, emit_vn=False, tt=PROMPT_TILE, ts=ODD_SUB,
                              r=1, next_w=next_w, **common)
            xp, new_odd_p = outs[0], outs[1:2]
            xs, *new_odd_s = _odd_layer(xs, odd_state_s, head + mix_s + odd_tail, new_odd_s, emit_vn=True,
                                        y_native=l + 1 == depth, **common, **sample)
        if next_w is not None:
            w_in_bf, w_out_bf = outs[-2:]
    ca_p, pb_p = new_even_p
    ca_s, pb_s = new_even_s
    (cd_p,) = new_odd_p
    cd_s, v_s = new_odd_s
    return (xp, xs, ca_p, time_major(ca_s), pb_p, time_major(pb_s),
            cd_p, time_major(cd_s), v_s)
```

```python
import functools

import jax
import jax.numpy as jnp
from jax import lax
from jax.experimental import pallas as pl
from jax.experimental.pallas import tpu as pltpu

D_MODEL = 1024
BRANCH = 512
LANES = 128
N_SLABS = BRANCH // LANES
IN_WIDTH = 6 * BRANCH
CONV_A = 3
POOL_WINDOWS = (2, 4, 8, 16)
POOL_HIST = max(POOL_WINDOWS) - 1
CHUNK = 128
CONV_D = 31
PAST_LEN = 16384
EPS = 1e-6

PAD_A = 8
PAD_B = 16
PAD_D = 32

VMEM_LIMIT_BYTES = 56 * 1024 * 1024

PROMPT_TILE = 1024
EVEN_SUB = 256
ODD_SUB = 512
SAMPLE_SEQS = 64


def _rms_norm(x, g):
    ms = jnp.mean(x * x, axis=-1, keepdims=True)
    return x * lax.rsqrt(ms + EPS) * g


def _layer_norm(x, g, b):
    mu = jnp.mean(x, axis=-1, keepdims=True)
    xc = x - mu
    var = jnp.mean(xc * xc, axis=-1, keepdims=True)
    return xc * lax.rsqrt(var + EPS) * g + b


def _sigmoid(x):
    return 0.5 * jnp.tanh(0.5 * x) + 0.5


def _silu(x):
    hx = 0.5 * x
    return hx * jnp.tanh(hx) + hx


def _slab(c):
    return slice(c * LANES, (c + 1) * LANES)


def _row(ref, i):
    return ref[i:i + 1, :]


def _get_rows(ref, t0, ts, r):
    if r == 1:
        return ref[0, t0:t0 + ts, :]
    return ref[t0:t0 + ts].reshape(ts * r, ref.shape[-1])


def _set_rows(ref, t0, ts, r, val):
    if r == 1:
        ref[0, t0:t0 + ts, :] = val
    else:
        ref[t0:t0 + ts] = val.reshape(ts, r, ref.shape[-1])


def _carry_history(ext, hist_ref, pad, n_hist, tt, n_t, r):
    j = pl.program_id(1)
    lo, hi = (pad - n_hist) * r, pad * r

    @pl.when(j == 0)
    def _():
        for c in range(N_SLABS):
            if hist_ref is None:
                ext[c, lo:hi, :] = jnp.zeros((hi - lo, LANES), ext.dtype)
            else:
                ext[c, lo:hi, :] = hist_ref[:, :, _slab(c)].reshape(hi - lo, LANES)

    if n_t > 1:
        @pl.when(j > 0)
        def _():
            for c in range(N_SLABS):
                ext[c, lo:hi, :] = ext[c, lo + tt * r:hi + tt * r, :]


def _zero_other_layers(ref, fill):
    if fill is not None:
        for o in range(fill[1]):
            if o != fill[0]:
                ref[o] = jnp.zeros(ref.shape[1:], ref.dtype)


def _emit_history(ext, new_ref, pad, n_hist, tt, n_t, r, fill):
    @pl.when(pl.program_id(1) == n_t - 1)
    def _():
        dst = new_ref if fill is None else new_ref.at[fill[0]]
        for c in range(N_SLABS):
            last = ext[c, (pad + tt - n_hist) * r:(pad + tt) * r, :]
            if r == 1:
                dst[0, :, _slab(c)] = last
            else:
                dst[:, :, _slab(c)] = last.reshape(n_hist, r, LANES)
        _zero_other_layers(new_ref, fill)


def _conv_slab(ext, w_ref, c, pad, t0, ts, n_taps, r):
    acc = None
    for k in range(n_taps):
        lo = (pad + t0 - (n_taps - 1) + k) * r
        term = ext[c, lo:lo + ts * r, :] * w_ref[k:k + 1, _slab(c)]
        acc = term if acc is None else acc + term
    return acc


def _store_slabs(ext, val, pad, t0, ts, r):
    for c in range(N_SLABS):
        ext[c, (pad + t0) * r:(pad + t0 + ts) * r, :] = val[:, _slab(c)]


def _half_out(wout_ref, y, half):
    return jnp.dot(y.astype(jnp.bfloat16), wout_ref[half * BRANCH:(half + 1) * BRANCH, :],
                   preferred_element_type=jnp.float32)


def _fetch_native_x(x_hbm, xbuf, sems, tt, r):
    b = pl.program_id(0)
    copies = lambda blk: [pltpu.make_async_copy(x_hbm.at[pl.ds(blk * r, r), t, :], xbuf.at[blk, t], sems.at[blk, t])
                          for t in range(tt)]

    @pl.when(b == 0)
    def _():
        for blk in range(xbuf.shape[0]):
            for t, c in enumerate(copies(blk)):
                c.start(priority=t % 2)

    for blk in range(xbuf.shape[0]):
        @pl.when(b == blk)
        def _():
            for c in copies(blk):
                c.wait()

    return xbuf.at[b]


def _even_kernel(*refs, tt, ts, r, n_t, start_pos, has_hist, n_alias, fill, layer, idx, x_native):
    refs = list(refs)
    x_ref = refs.pop(0)
    if x_native:
        x_sems, xbuf = refs.pop(), refs.pop()
        x_ref = _fetch_native_x(x_ref, xbuf, x_sems, tt, r)
    ha_ref, hb_ref = (refs.pop(0), refs.pop(0)) if has_hist else (None, None)
    pre_ref, post_ref, win_ref, wout_ref, cw_ref, pw_ref, ps_ref = refs[:7]
    y_ref, na_ref, nb_ref, ext_a, ext_b = refs[7 + n_alias:]
    _carry_history(ext_a, ha_ref, PAD_A, CONV_A - 1, tt, n_t, r)
    _carry_history(ext_b, hb_ref, PAD_B, POOL_HIST, tt, n_t, r)
    subs = list(range(0, tt, ts))
    ms = ts * r

    z = {}
    for t0 in subs:
        h = _rms_norm(_get_rows(x_ref, t0, ts, r), _row(pre_ref, layer)).astype(jnp.bfloat16)
        z[t0] = jnp.dot(h, win_ref[...], preferred_element_type=jnp.float32)
    part = lambda t0, i: z[t0][:, i * BRANCH:(i + 1) * BRANCH]

    ya, yb = {}, {}
    for t0 in subs:
        _store_slabs(ext_a, part(t0, 1) * part(t0, 2), PAD_A, t0, ts, r)
        conv = jnp.concatenate([_conv_slab(ext_a, cw_ref, c, PAD_A, t0, ts, CONV_A, r) for c in range(N_SLABS)],
                               axis=-1)
        ya[t0] = part(t0, 0) * conv * _silu(part(t0, 3))
    for t0 in subs:
        p = part(t0, 4)
        _store_slabs(ext_b, p, PAD_B, t0, ts, r)
        step = lax.broadcasted_iota(jnp.int32, (ms, LANES), 0) // r
        pos = start_pos + pl.program_id(1) * tt + t0 + step
        ds = []
        for g, w in enumerate(POOL_WINDOWS):
            acc = ext_b[g, (PAD_B + t0) * r:(PAD_B + t0 + ts) * r, :]
            for i in range(1, w):
                acc = acc + ext_b[g, (PAD_B + t0 - i) * r:(PAD_B + t0 - i + ts) * r, :]
            cnt = jnp.minimum(pos + 1, w).astype(jnp.float32)
            ds.append((acc / cnt - p[:, _slab(g)]).astype(jnp.bfloat16))
        mixed = []
        zero = jnp.zeros((LANES, LANES), jnp.bfloat16)
        for g in range(0, N_SLABS, 2):
            w2 = jnp.concatenate([jnp.concatenate([pw_ref[g], zero], axis=1),
                                  jnp.concatenate([zero, pw_ref[g + 1]], axis=1)], axis=0)
            mixed.append(jnp.dot(jnp.concatenate(ds[g:g + 2], axis=1), w2, preferred_element_type=jnp.float32))
        yb[t0] = jnp.concatenate(mixed, axis=-1) * _row(ps_ref, idx) * _silu(part(t0, 5))
    for t0 in subs:
        o = _half_out(wout_ref, ya[t0], 0) + _half_out(wout_ref, yb[t0], 1)
        _set_rows(y_ref, t0, ts, r, _get_rows(x_ref, t0, ts, r) + _rms_norm(o, _row(post_ref, layer)))

    _emit_history(ext_a, na_ref, PAD_A, CONV_A - 1, tt, n_t, r, fill)
    _emit_history(ext_b, nb_ref, PAD_B, POOL_HIST, tt, n_t, r, fill)


def _mix_chunks(vn, ws_ref, bs_ref):
    row = lax.broadcasted_iota(jnp.int32, (CHUNK, CHUNK), 0)
    col = lax.broadcasted_iota(jnp.int32, (CHUNK, CHUNK), 1)
    wm = [jnp.where(col <= row, ws_ref[g], 0.0).astype(jnp.bfloat16) for g in range(N_SLABS)]
    vnb = vn.astype(jnp.bfloat16)
    n_ch = vn.shape[0] // CHUNK
    outs = []
    for g in range(N_SLABS):
        rhs = jnp.concatenate([vnb[ch * CHUNK:(ch + 1) * CHUNK, _slab(g)] for ch in range(n_ch)], axis=1)
        outs.append(jnp.dot(wm[g], rhs, preferred_element_type=jnp.float32) + bs_ref[:, g:g + 1])
    return jnp.concatenate(
        [jnp.concatenate([outs[g][:, _slab(ch)] for g in range(N_SLABS)], axis=1) for ch in range(n_ch)], axis=0)


def _mix_steps(vn, ws_ref, bs_ref, n_steps, r):
    out = []
    for i in range(n_steps):
        acc = None
        for j in range(i + 1):
            term = vn[j * r:(j + 1) * r, :] * ws_ref[i, j:j + 1, :]
            acc = term if acc is None else acc + term
        out.append(acc + bs_ref[i:i + 1, :])
    return jnp.concatenate(out, axis=0)


def _odd_kernel(*refs, tt, ts, r, n_t, has_hist, emit_vn, n_alias, fill, layer, idx, y_native):
    refs = list(refs)
    x_ref = refs.pop(0)
    hd_ref = refs.pop(0) if has_hist else None
    (pre_ref, post_ref, win_ref, wout_ref, lng_ref, lnb_ref, ws_ref, bs_ref,
     dw_ref, db_ref, cg_ref, cb_ref) = refs[:12]
    outs = refs[12 + n_alias:]
    if y_native:
        y_hbm, y_sems = outs[0], outs.pop()
        outs[0] = outs.pop()
    if emit_vn:
        y_ref, nd_ref, vn_ref, ext_d, h_ref = outs
    else:
        y_ref, nd_ref, ext_d, h_ref = outs
    _carry_history(ext_d, hd_ref, PAD_D, CONV_D - 1, tt, n_t, r)
    subs = list(range(0, tt, ts))
    hrows = lambda t0: slice(t0 * r, (t0 + ts) * r)

    for t0 in subs:
        h_ref[hrows(t0), :] = _rms_norm(_get_rows(x_ref, t0, ts, r), _row(pre_ref, layer)).astype(jnp.bfloat16)

    conv_out = {t0: [] for t0 in subs}
    for c in range(N_SLABS):
        w_c = jnp.concatenate([win_ref[:, 3 * BRANCH + c * LANES:3 * BRANCH + (c + 1) * LANES],
                               win_ref[:, 4 * BRANCH + c * LANES:4 * BRANCH + (c + 1) * LANES]], axis=1)
        for t0 in subs:
            zc = jnp.dot(h_ref[hrows(t0), :], w_c, preferred_element_type=jnp.float32)
            ext_d[c, (PAD_D + t0) * r:(PAD_D + t0 + ts) * r, :] = zc[:, :LANES] * _sigmoid(zc[:, LANES:])
            conv_out[t0].append(_conv_slab(ext_d, dw_ref, c, PAD_D, t0, ts, CONV_D, r))

    z3 = {t0: jnp.dot(h_ref[hrows(t0), :], win_ref[:, :3 * BRANCH], preferred_element_type=jnp.float32)
          for t0 in subs}
    gd = {t0: jnp.dot(h_ref[hrows(t0), :], win_ref[:, 5 * BRANCH:], preferred_element_type=jnp.float32)
          for t0 in subs}

    o1, yd = {}, {}
    for t0 in subs:
        u, v, gc = (z3[t0][:, i * BRANCH:(i + 1) * BRANCH] for i in range(3))
        vn = _layer_norm(v, _row(lng_ref, idx), _row(lnb_ref, idx))
        if emit_vn:
            _set_rows(vn_ref if fill is None else vn_ref.at[fill[0]], t0, ts, r, vn)
        mixed = _mix_chunks(vn, ws_ref, bs_ref) if r == 1 else _mix_steps(vn, ws_ref, bs_ref, ts, r)
        o1[t0] = _half_out(wout_ref, u * mixed * _silu(gc), 0)
    for t0 in subs:
        y = jnp.concatenate(conv_out[t0], axis=-1) + _row(db_ref, idx)
        yd[t0] = _silu(_layer_norm(y, _row(cg_ref, idx), _row(cb_ref, idx))) * _silu(gd[t0])
    for t0 in subs:
        o = o1[t0] + _half_out(wout_ref, yd[t0], 1)
        _set_rows(y_ref, t0, ts, r, _get_rows(x_ref, t0, ts, r) + _rms_norm(o, _row(post_ref, layer)))

    if y_native:
        seqs = pl.ds(pl.multiple_of(pl.program_id(0) * r, r), r)
        sends = [pltpu.make_async_copy(y_ref.at[t], y_hbm.at[seqs, t, :], y_sems.at[t]) for t in range(tt)]
        for t, c in enumerate(sends):
            c.start(priority=t % 2)
        for c in sends:
            c.wait()
    if emit_vn:
        _zero_other_layers(vn_ref, fill)
    _emit_history(ext_d, nd_ref, PAD_D, CONV_D - 1, tt, n_t, r, fill)


def _chunk_copies(w, bf, stage, obuf, sems, kind, idx, k):
    rows = stage.shape[1]
    sl = pl.ds(pl.multiple_of(k * rows, rows), rows)
    slot = k % 2
    fetch = pltpu.make_async_copy(w.at[idx, sl, :], stage.at[slot], sems.at[2 * kind, slot])
    send = pltpu.make_async_copy(obuf.at[slot], bf.at[sl, :], sems.at[2 * kind + 1, slot])
    return fetch, send


def _cast_next_weights(nw_in, nw_out, bf_in, bf_out, stage_in, stage_out, obuf_in, obuf_out, sems, *, idx, n_steps):
    s = pl.program_id(0) * pl.num_programs(1) + pl.program_id(1)
    pairs = ((nw_in, bf_in, stage_in, obuf_in, 0), (nw_out, bf_out, stage_out, obuf_out, 1))
    copies = lambda k: [_chunk_copies(w, bf, st, ob, sems, kind, idx, k) for w, bf, st, ob, kind in pairs]

    @pl.when(s == 0)
    def _():
        for fetch, _ in copies(s):
            fetch.start()

    for fetch, _ in copies(s):
        fetch.wait()

    @pl.when(s >= 2)
    def _():
        for _, send in copies(s - 2):
            send.wait()

    slot = s % 2
    obuf_in[slot] = stage_in[slot].astype(jnp.bfloat16)
    obuf_out[slot] = stage_out[slot].astype(jnp.bfloat16)
    for _, send in copies(s):
        send.start()

    @pl.when(s + 1 < n_steps)
    def _():
        for fetch, _ in copies(s + 1):
            fetch.start()

    @pl.when(s == n_steps - 1)
    def _():
        for k in (s - 1, s):
            for _, send in copies(k):
                send.wait()


def _with_next_weights(body, n_in, n_out, n_scratch, idx, n_steps, *refs):
    ins, rest = refs[:n_in], refs[n_in:]
    nw, rest = rest[:2], rest[2:]
    outs, rest = rest[:n_out], rest[n_out:]
    bf, rest = rest[:2], rest[2:]
    scratch, staging = rest[:n_scratch], rest[n_scratch:]
    _cast_next_weights(*nw, *bf, *staging, idx=idx, n_steps=n_steps)
    body(*ins, *outs, *scratch)


def _launch(name, body, grid, ins, in_specs, out_specs, out_shape, scratch, aliases, next_w):
    if next_w is not None:
        nw_in, nw_out, nidx = next_w
        n_steps = grid[0] * grid[1]
        rows = nw_in.shape[1] // n_steps
        assert nw_in.shape[1] == nw_out.shape[1] == rows * n_steps and rows % 16 == 0
        any_spec = pl.BlockSpec(memory_space=pl.ANY)
        body = functools.partial(_with_next_weights, body, len(ins), len(out_specs), len(scratch), nidx, n_steps)
        ins = ins + [nw_in, nw_out]
        in_specs = in_specs + [any_spec, any_spec]
        out_specs = out_specs + [any_spec, any_spec]
        out_shape = out_shape + [jax.ShapeDtypeStruct(w.shape[1:], jnp.bfloat16) for w in (nw_in, nw_out)]
        scratch = scratch + [pltpu.VMEM((2, rows, nw_in.shape[2]), jnp.float32),
                             pltpu.VMEM((2, rows, nw_out.shape[2]), jnp.float32),
                             pltpu.VMEM((2, rows, nw_in.shape[2]), jnp.bfloat16),
                             pltpu.VMEM((2, rows, nw_out.shape[2]), jnp.bfloat16),
                             pltpu.SemaphoreType.DMA((4, 2))]
    return pl.pallas_call(body, grid=grid, in_specs=in_specs, out_specs=out_specs, out_shape=out_shape,
                          scratch_shapes=scratch, input_output_aliases=aliases,
                          compiler_params=_compiler_params(), name=name)(*ins)


def _layer_spec(arr, idx):
    if arr.ndim == 2:
        return pl.BlockSpec(arr.shape, lambda b, j: (0, 0), pipeline_mode=pl.Buffered(1))
    tail = arr.shape[1:]
    return pl.BlockSpec((None,) + tail, lambda b, j: (idx,) + (0,) * len(tail), pipeline_mode=pl.Buffered(1))


def _stacked_spec(rows, r, idx, n_layers, whole):
    lead = n_layers if whole else None
    l0 = 0 if whole else idx
    if r == 1:
        return pl.BlockSpec((lead, 1, rows, BRANCH), lambda b, j: (l0, b, 0, 0))
    return pl.BlockSpec((lead, rows, r, BRANCH), lambda b, j: (l0, 0, b, 0))


def _stacked_shape(n_layers, bsz, rows, r, dtype):
    shape = (n_layers, bsz, rows, BRANCH) if r == 1 else (n_layers, rows, bsz, BRANCH)
    return jax.ShapeDtypeStruct(shape, dtype)


def _x_spec(tt, r):
    if r == 1:
        return pl.BlockSpec((1, tt, D_MODEL), lambda b, j: (b, j, 0))
    return pl.BlockSpec((tt, r, D_MODEL), lambda b, j: (0, b, 0))


def _compiler_params():
    return pltpu.CompilerParams(dimension_semantics=("arbitrary", "arbitrary"),
                                vmem_limit_bytes=VMEM_LIMIT_BYTES)


def _alias_inputs(ins, in_specs, prev_new):
    aliases = {}
    for k, arr in enumerate(prev_new or ()):
        aliases[len(ins)] = 1 + k
        ins.append(arr)
        in_specs.append(pl.BlockSpec(memory_space=pl.ANY))
    return aliases


def _even_layer(x, states, params, prev_new, *, layer, idx, n_layers, tt, ts, r, start_pos, next_w=None,
                x_native=False):
    bsz, t = (x.shape[0], x.shape[1]) if r == 1 or x_native else (x.shape[1], x.shape[0])
    n_t = t // tt
    first = prev_new is None
    pre_g, post_g, w_in, w_out, conv_w, pool_w, pool_scale = params
    ins, in_specs = [x], [pl.BlockSpec(memory_space=pl.ANY) if x_native else _x_spec(tt, r)]
    if states is not None:
        ins += list(states)
        in_specs += [_stacked_spec(CONV_A - 1, r, idx, n_layers, False),
                     _stacked_spec(POOL_HIST, r, idx, n_layers, False)]
    ins += [pre_g, post_g, w_in, w_out, conv_w, pool_w, pool_scale]
    in_specs += [_layer_spec(pre_g, layer), _layer_spec(post_g, layer), _layer_spec(w_in, idx),
                 _layer_spec(w_out, idx), _layer_spec(conv_w, idx), _layer_spec(pool_w, idx),
                 _layer_spec(pool_scale, idx)]
    aliases = _alias_inputs(ins, in_specs, prev_new)
    body = functools.partial(_even_kernel, tt=tt, ts=ts, r=r, n_t=n_t, start_pos=start_pos, has_hist=states is not None,
                             n_alias=len(aliases), fill=(idx, n_layers) if first else None, layer=layer, idx=idx,
                             x_native=x_native)
    out_specs = [_x_spec(tt, r), _stacked_spec(CONV_A - 1, r, idx, n_layers, first),
                 _stacked_spec(POOL_HIST, r, idx, n_layers, first)]
    out_shape = [jax.ShapeDtypeStruct((bsz, t, D_MODEL) if r == 1 else (t, bsz, D_MODEL), x.dtype),
                 _stacked_shape(n_layers, bsz, CONV_A - 1, r, x.dtype),
                 _stacked_shape(n_layers, bsz, POOL_HIST, r, x.dtype)]
    scratch = [pltpu.VMEM((N_SLABS, (PAD_A + tt) * r, LANES), jnp.float32),
               pltpu.VMEM((N_SLABS, (PAD_B + tt) * r, LANES), jnp.float32)]
    if x_native:
        assert r > 1 and n_t == 1
        scratch += [pltpu.VMEM((bsz // r, tt, r, D_MODEL), jnp.float32), pltpu.SemaphoreType.DMA((bsz // r, tt))]
    return _launch("even_layer", body, (bsz // r, n_t), ins, in_specs, out_specs, out_shape, scratch, aliases, next_w)


def _odd_layer(x, state, params, prev_new, *, layer, idx, n_layers, tt, ts, r, emit_vn, next_w=None, y_native=False):
    bsz, t = (x.shape[0], x.shape[1]) if r == 1 else (x.shape[1], x.shape[0])
    n_t = t // tt
    first = prev_new is None
    ins, in_specs = [x], [_x_spec(tt, r)]
    if state is not None:
        ins.append(state)
        in_specs.append(_stacked_spec(CONV_D - 1, r, idx, n_layers, False))
    for k, arr in enumerate(params):
        ins.append(arr)
        in_specs.append(_layer_spec(arr, layer if k < 2 else idx))
    out_specs = [_x_spec(tt, r), _stacked_spec(CONV_D - 1, r, idx, n_layers, first)]
    out_shape = [jax.ShapeDtypeStruct(x.shape, x.dtype), _stacked_shape(n_layers, bsz, CONV_D - 1, r, x.dtype)]
    if emit_vn:
        out_specs.append(_stacked_spec(tt, r, idx, n_layers, first))
        out_shape.append(_stacked_shape(n_layers, bsz, t, r, x.dtype))
    aliases = _alias_inputs(ins, in_specs, prev_new)
    body = functools.partial(_odd_kernel, tt=tt, ts=ts, r=r, n_t=n_t, has_hist=state is not None, emit_vn=emit_vn,
                             n_alias=len(aliases), fill=(idx, n_layers) if first else None, layer=layer, idx=idx,
                             y_native=y_native)
    scratch = [pltpu.VMEM((N_SLABS, (PAD_D + tt) * r, LANES), jnp.float32),
               pltpu.VMEM((tt * r, D_MODEL), jnp.bfloat16)]
    if y_native:
        assert r > 1 and n_t == 1
        out_specs[0] = pl.BlockSpec(memory_space=pl.ANY)
        out_shape[0] = jax.ShapeDtypeStruct((bsz, t, D_MODEL), x.dtype)
        scratch += [pltpu.VMEM((tt, r, D_MODEL), jnp.float32), pltpu.SemaphoreType.DMA((tt,))]
    return _launch("odd_layer", body, (bsz // r, n_t), ins, in_specs, out_specs, out_shape, scratch, aliases, next_w)


def kernel(x_prompt, x_sample, state_conv_a, state_pool_b, state_conv_d, norm_pre, norm_post, w_in_even, w_out_even, conv_a_w, pool_w, pool_scale, w_in_odd, w_out_odd, sgu_ln_g, sgu_ln_b, sgu_w, sgu_b, conf_dw_w, conf_dw_b, conf_ln_g, conf_ln_b):
    xp = x_prompt
    dec_seq = x_sample.shape[1]
    depth = norm_pre.shape[0]
    n_even, n_odd = w_in_even.shape[0], w_in_odd.shape[0]
    bf16 = jnp.bfloat16
    time_major = lambda a: jnp.swapaxes(a, -3, -2)

    even_tail = (conv_a_w, pool_w.astype(bf16), pool_scale)
    odd_tail = (conf_dw_w, conf_dw_b, conf_ln_g, conf_ln_b)
    mix_p = (sgu_w, jnp.swapaxes(sgu_b, 1, 2))
    mix_s = (jnp.repeat(jnp.transpose(sgu_w[:, :, :dec_seq, :dec_seq], (0, 2, 3, 1)), LANES, axis=-1),
             jnp.repeat(jnp.swapaxes(sgu_b[:, :, :dec_seq], 1, 2), LANES, axis=-1))

    xs = x_sample
    even_states_s = (time_major(state_conv_a), time_major(state_pool_b))
    odd_state_s = time_major(state_conv_d)
    sample = dict(tt=dec_seq, ts=dec_seq, r=SAMPLE_SEQS)
    new_even_p = new_even_s = new_odd_p = new_odd_s = None
    w_in_bf, w_out_bf = w_in_even[0].astype(bf16), w_out_even[0].astype(bf16)
    for l in range(depth):
        i = l // 2
        stacks = (w_in_odd, w_out_odd, i) if l % 2 == 0 else (w_in_even, w_out_even, i + 1)
        next_w = stacks if l + 1 < depth else None
        if l % 2 == 0:
            common = dict(layer=l, idx=i, n_layers=n_even)
            params = (norm_pre, norm_post, w_in_bf, w_out_bf) + even_tail
            outs = _even_layer(xp, None, params, new_even_p, start_pos=0, tt=PROMPT_TILE, ts=EVEN_SUB, r=1,
                               next_w=next_w, **common)
            xp, new_even_p = outs[0], outs[1:3]
            xs, *new_even_s = _even_layer(xs, even_states_s, params, new_even_s, start_pos=PAST_LEN,
                                          x_native=l == 0, **common, **sample)
        else:
            common = dict(layer=l, idx=i, n_layers=n_odd)
            head = (norm_pre, norm_post, w_in_bf, w_out_bf, sgu_ln_g, sgu_ln_b)
            outs = _odd_layer(xp, None, head + mix_p + odd_tail, new_odd_p, emit_vn=False, tt=PROMPT_TILE, ts=ODD_SUB,
                              r=1, next_w=next_w, **common)
            xp, new_odd_p = outs[0], outs[1:2]
            xs, *new_odd_s = _odd_layer(xs, odd_state_s, head + mix_s + odd_tail, new_odd_s, emit_vn=True,
                                        y_native=l + 1 == depth, **common, **sample)
        if next_w is not None:
            w_in_bf, w_out_bf = outs[-2:]
    ca_p, pb_p = new_even_p
    ca_s, pb_s = new_even_s
    (cd_p,) = new_odd_p
    cd_s, v_s = new_odd_s
    return (xp, xs, ca_p, time_major(ca_s), pb_p, time_major(pb_s),
            cd_p, time_major(cd_s), time_major(v_s))
```

```python
import functools

import jax
import jax.numpy as jnp
from jax import lax
from jax.experimental import pallas as pl
from jax.experimental.pallas import tpu as pltpu

D_MODEL = 1024
BRANCH = 512
LANES = 128
N_SLABS = BRANCH // LANES
IN_WIDTH = 6 * BRANCH
CONV_A = 3
POOL_WINDOWS = (2, 4, 8, 16)
POOL_HIST = max(POOL_WINDOWS) - 1
CHUNK = 128
CONV_D = 31
PAST_LEN = 16384
EPS = 1e-6

PAD_A = 8
PAD_B = 16
PAD_D = 32

VMEM_LIMIT_BYTES = 56 * 1024 * 1024

PROMPT_TILE = 1024
EVEN_SUB = 256
ODD_SUB = 512
SAMPLE_SEQS = 64


def _rms_norm(x, g):
    ms = jnp.mean(x * x, axis=-1, keepdims=True)
    return x * lax.rsqrt(ms + EPS) * g


def _layer_norm(x, g, b):
    mu = jnp.mean(x, axis=-1, keepdims=True)
    xc = x - mu
    var = jnp.mean(xc * xc, axis=-1, keepdims=True)
    return xc * lax.rsqrt(var + EPS) * g + b


def _sigmoid(x):
    return 0.5 * jnp.tanh(0.5 * x) + 0.5


def _silu(x):
    hx = 0.5 * x
    return hx * jnp.tanh(hx) + hx


def _slab(c):
    return slice(c * LANES, (c + 1) * LANES)


def _row(ref, i):
    return ref[i:i + 1, :]


def _get_rows(ref, t0, ts, r):
    if r == 1:
        return ref[0, t0:t0 + ts, :]
    return ref[t0:t0 + ts].reshape(ts * r, ref.shape[-1])


def _set_rows(ref, t0, ts, r, val):
    if r == 1:
        ref[0, t0:t0 + ts, :] = val
    else:
        ref[t0:t0 + ts] = val.reshape(ts, r, ref.shape[-1])


def _carry_history(ext, hist_ref, pad, n_hist, tt, n_t, r):
    j = pl.program_id(1)
    lo, hi = (pad - n_hist) * r, pad * r

    @pl.when(j == 0)
    def _():
        for c in range(N_SLABS):
            if hist_ref is None:
                ext[c, lo:hi, :] = jnp.zeros((hi - lo, LANES), ext.dtype)
            else:
                ext[c, lo:hi, :] = hist_ref[:, :, _slab(c)].reshape(hi - lo, LANES)

    if n_t > 1:
        @pl.when(j > 0)
        def _():
            for c in range(N_SLABS):
                ext[c, lo:hi, :] = ext[c, lo + tt * r:hi + tt * r, :]


def _zero_other_layers(ref, fill):
    if fill is not None:
        for o in range(fill[1]):
            if o != fill[0]:
                ref[o] = jnp.zeros(ref.shape[1:], ref.dtype)


def _emit_history(ext, new_ref, pad, n_hist, tt, n_t, r, fill):
    @pl.when(pl.program_id(1) == n_t - 1)
    def _():
        dst = new_ref if fill is None else new_ref.at[fill[0]]
        for c in range(N_SLABS):
            last = ext[c, (pad + tt - n_hist) * r:(pad + tt) * r, :]
            if r == 1:
                dst[0, :, _slab(c)] = last
            else:
                dst[:, :, _slab(c)] = last.reshape(n_hist, r, LANES)
        _zero_other_layers(new_ref, fill)


def _conv_slab(ext, w_ref, c, pad, t0, ts, n_taps, r):
    acc = None
    for k in range(n_taps):
        lo = (pad + t0 - (n_taps - 1) + k) * r
        term = ext[c, lo:lo + ts * r, :] * w_ref[k:k + 1, _slab(c)]
        acc = term if acc is None else acc + term
    return acc


def _store_slabs(ext, val, pad, t0, ts, r):
    for c in range(N_SLABS):
        ext[c, (pad + t0) * r:(pad + t0 + ts) * r, :] = val[:, _slab(c)]


def _half_out(wout_ref, y, half):
    return jnp.dot(y.astype(jnp.bfloat16), wout_ref[half * BRANCH:(half + 1) * BRANCH, :],
                   preferred_element_type=jnp.float32)


def _fetch_native_x(x_hbm, xbuf, sems, tt, r):
    b = pl.program_id(0)
    copies = lambda blk: [pltpu.make_async_copy(x_hbm.at[pl.ds(blk * r, r), t, :], xbuf.at[blk, t], sems.at[blk, t])
                          for t in range(tt)]

    @pl.when(b == 0)
    def _():
        for blk in range(xbuf.shape[0]):
            for t, c in enumerate(copies(blk)):
                c.start(priority=t % 2)

    for blk in range(xbuf.shape[0]):
        @pl.when(b == blk)
        def _():
            for c in copies(blk):
                c.wait()

    return xbuf.at[b]


def _even_kernel(*refs, tt, ts, r, n_t, start_pos, has_hist, n_alias, fill, layer, idx, x_native):
    refs = list(refs)
    x_ref = refs.pop(0)
    if x_native:
        x_sems, xbuf = refs.pop(), refs.pop()
        x_ref = _fetch_native_x(x_ref, xbuf, x_sems, tt, r)
    ha_ref, hb_ref = (refs.pop(0), refs.pop(0)) if has_hist else (None, None)
    pre_ref, post_ref, win_ref, wout_ref, cw_ref, pw_ref, ps_ref = refs[:7]
    y_ref, na_ref, nb_ref, ext_a, ext_b = refs[7 + n_alias:]
    _carry_history(ext_a, ha_ref, PAD_A, CONV_A - 1, tt, n_t, r)
    _carry_history(ext_b, hb_ref, PAD_B, POOL_HIST, tt, n_t, r)
    subs = list(range(0, tt, ts))
    ms = ts * r

    z = {}
    for t0 in subs:
        h = _rms_norm(_get_rows(x_ref, t0, ts, r), _row(pre_ref, layer)).astype(jnp.bfloat16)
        z[t0] = jnp.dot(h, win_ref[...], preferred_element_type=jnp.float32)
    part = lambda t0, i: z[t0][:, i * BRANCH:(i + 1) * BRANCH]

    ya, yb = {}, {}
    for t0 in subs:
        _store_slabs(ext_a, part(t0, 1) * part(t0, 2), PAD_A, t0, ts, r)
        conv = jnp.concatenate([_conv_slab(ext_a, cw_ref, c, PAD_A, t0, ts, CONV_A, r) for c in range(N_SLABS)],
                               axis=-1)
        ya[t0] = part(t0, 0) * conv * _silu(part(t0, 3))
    for t0 in subs:
        p = part(t0, 4)
        _store_slabs(ext_b, p, PAD_B, t0, ts, r)
        step = lax.broadcasted_iota(jnp.int32, (ms, LANES), 0) // r
        pos = start_pos + pl.program_id(1) * tt + t0 + step
        ds = []
        for g, w in enumerate(POOL_WINDOWS):
            acc = ext_b[g, (PAD_B + t0) * r:(PAD_B + t0 + ts) * r, :]
            for i in range(1, w):
                acc = acc + ext_b[g, (PAD_B + t0 - i) * r:(PAD_B + t0 - i + ts) * r, :]
            cnt = jnp.minimum(pos + 1, w).astype(jnp.float32)
            ds.append((acc / cnt - p[:, _slab(g)]).astype(jnp.bfloat16))
        mixed = []
        zero = jnp.zeros((LANES, LANES), jnp.bfloat16)
        for g in range(0, N_SLABS, 2):
            w2 = jnp.concatenate([jnp.concatenate([pw_ref[g], zero], axis=1),
                                  jnp.concatenate([zero, pw_ref[g + 1]], axis=1)], axis=0)
            mixed.append(jnp.dot(jnp.concatenate(ds[g:g + 2], axis=1), w2, preferred_element_type=jnp.float32))
        yb[t0] = jnp.concatenate(mixed, axis=-1) * _row(ps_ref, idx) * _silu(part(t0, 5))
    for t0 in subs:
        o = _half_out(wout_ref, ya[t0], 0) + _half_out(wout_ref, yb[t0], 1)
        _set_rows(y_ref, t0, ts, r, _get_rows(x_ref, t0, ts, r) + _rms_norm(o, _row(post_ref, layer)))

    _emit_history(ext_a, na_ref, PAD_A, CONV_A - 1, tt, n_t, r, fill)
    _emit_history(ext_b, nb_ref, PAD_B, POOL_HIST, tt, n_t, r, fill)


def _mix_chunks(vn, ws_ref, bs_ref):
    row = lax.broadcasted_iota(jnp.int32, (CHUNK, CHUNK), 0)
    col = lax.broadcasted_iota(jnp.int32, (CHUNK, CHUNK), 1)
    wm = [jnp.where(col <= row, ws_ref[g], 0.0).astype(jnp.bfloat16) for g in range(N_SLABS)]
    vnb = vn.astype(jnp.bfloat16)
    n_ch = vn.shape[0] // CHUNK
    outs = []
    for g in range(N_SLABS):
        rhs = jnp.concatenate([vnb[ch * CHUNK:(ch + 1) * CHUNK, _slab(g)] for ch in range(n_ch)], axis=1)
        outs.append(jnp.dot(wm[g], rhs, preferred_element_type=jnp.float32) + bs_ref[:, g:g + 1])
    return jnp.concatenate(
        [jnp.concatenate([outs[g][:, _slab(ch)] for g in range(N_SLABS)], axis=1) for ch in range(n_ch)], axis=0)


def _mix_steps(vn, ws_ref, bs_ref, n_steps, r):
    out = []
    for i in range(n_steps):
        acc = None
        for j in range(i + 1):
            term = vn[j * r:(j + 1) * r, :] * ws_ref[i, j:j + 1, :]
            acc = term if acc is None else acc + term
        out.append(acc + bs_ref[i:i + 1, :])
    return jnp.concatenate(out, axis=0)


def _odd_kernel(*refs, tt, ts, r, n_t, has_hist, emit_vn, n_alias, fill, layer, idx, y_native):
    refs = list(refs)
    x_ref = refs.pop(0)
    hd_ref = refs.pop(0) if has_hist else None
    (pre_ref, post_ref, win_ref, wout_ref, lng_ref, lnb_ref, ws_ref, bs_ref,
     dw_ref, db_ref, cg_ref, cb_ref) = refs[:12]
    outs = refs[12 + n_alias:]
    if y_native:
        y_hbm, y_sems = outs[0], outs.pop()
        outs[0] = outs.pop()
    if emit_vn:
        y_ref, nd_ref, vn_ref, ext_d, h_ref = outs
    else:
        y_ref, nd_ref, ext_d, h_ref = outs
    _carry_history(ext_d, hd_ref, PAD_D, CONV_D - 1, tt, n_t, r)
    subs = list(range(0, tt, ts))
    hrows = lambda t0: slice(t0 * r, (t0 + ts) * r)

    for t0 in subs:
        h_ref[hrows(t0), :] = _rms_norm(_get_rows(x_ref, t0, ts, r), _row(pre_ref, layer)).astype(jnp.bfloat16)

    conv_out = {t0: [] for t0 in subs}
    for c in range(N_SLABS):
        w_c = jnp.concatenate([win_ref[:, 3 * BRANCH + c * LANES:3 * BRANCH + (c + 1) * LANES],
                               win_ref[:, 4 * BRANCH + c * LANES:4 * BRANCH + (c + 1) * LANES]], axis=1)
        for t0 in subs:
            zc = jnp.dot(h_ref[hrows(t0), :], w_c, preferred_element_type=jnp.float32)
            ext_d[c, (PAD_D + t0) * r:(PAD_D + t0 + ts) * r, :] = zc[:, :LANES] * _sigmoid(zc[:, LANES:])
            conv_out[t0].append(_conv_slab(ext_d, dw_ref, c, PAD_D, t0, ts, CONV_D, r))

    z3 = {t0: jnp.dot(h_ref[hrows(t0), :], win_ref[:, :3 * BRANCH], preferred_element_type=jnp.float32)
          for t0 in subs}
    gd = {t0: jnp.dot(h_ref[hrows(t0), :], win_ref[:, 5 * BRANCH:], preferred_element_type=jnp.float32)
          for t0 in subs}

    o1, yd = {}, {}
    for t0 in subs:
        u, v, gc = (z3[t0][:, i * BRANCH:(i + 1) * BRANCH] for i in range(3))
        vn = _layer_norm(v, _row(lng_ref, idx), _row(lnb_ref, idx))
        if emit_vn:
            _set_rows(vn_ref if fill is None else vn_ref.at[fill[0]], t0, ts, r, vn)
        mixed = _mix_chunks(vn, ws_ref, bs_ref) if r == 1 else _mix_steps(vn, ws_ref, bs_ref, ts, r)
        o1[t0] = _half_out(wout_ref, u * mixed * _silu(gc), 0)
    for t0 in subs:
        y = jnp.concatenate(conv_out[t0], axis=-1) + _row(db_ref, idx)
        yd[t0] = _silu(_layer_norm(y, _row(cg_ref, idx), _row(cb_ref, idx))) * _silu(gd[t0])
    if y_native:
        seqs = pl.ds(pl.multiple_of(pl.program_id(0) * r, r), r)
        sends = [pltpu.make_async_copy(y_ref.at[t], y_hbm.at[seqs, t, :], y_sems.at[t]) for t in range(tt)]

        @pl.when(pl.program_id(0) > 0)
        def _():
            for c in sends:
                c.wait()

    for t0 in subs:
        o = o1[t0] + _half_out(wout_ref, yd[t0], 1)
        _set_rows(y_ref, t0, ts, r, _get_rows(x_ref, t0, ts, r) + _rms_norm(o, _row(post_ref, layer)))

    if y_native:
        for t, c in enumerate(sends):
            c.start(priority=t % 2)

        @pl.when(pl.program_id(0) == pl.num_programs(0) - 1)
        def _():
            for c in sends:
                c.wait()
    if emit_vn:
        _zero_other_layers(vn_ref, fill)
    _emit_history(ext_d, nd_ref, PAD_D, CONV_D - 1, tt, n_t, r, fill)


def _chunk_copies(w, bf, stage, obuf, sems, kind, idx, k):
    rows = stage.shape[1]
    sl = pl.ds(pl.multiple_of(k * rows, rows), rows)
    slot = k % 2
    fetch = pltpu.make_async_copy(w.at[idx, sl, :], stage.at[slot], sems.at[2 * kind, slot])
    send = pltpu.make_async_copy(obuf.at[slot], bf.at[sl, :], sems.at[2 * kind + 1, slot])
    return fetch, send


def _cast_next_weights(nw_in, nw_out, bf_in, bf_out, stage_in, stage_out, obuf_in, obuf_out, sems, *, idx, n_steps):
    s = pl.program_id(0) * pl.num_programs(1) + pl.program_id(1)
    pairs = ((nw_in, bf_in, stage_in, obuf_in, 0), (nw_out, bf_out, stage_out, obuf_out, 1))
    copies = lambda k: [_chunk_copies(w, bf, st, ob, sems, kind, idx, k) for w, bf, st, ob, kind in pairs]

    @pl.when(s == 0)
    def _():
        for fetch, _ in copies(s):
            fetch.start()

    for fetch, _ in copies(s):
        fetch.wait()

    @pl.when(s >= 2)
    def _():
        for _, send in copies(s - 2):
            send.wait()

    slot = s % 2
    obuf_in[slot] = stage_in[slot].astype(jnp.bfloat16)
    obuf_out[slot] = stage_out[slot].astype(jnp.bfloat16)
    for _, send in copies(s):
        send.start()

    @pl.when(s + 1 < n_steps)
    def _():
        for fetch, _ in copies(s + 1):
            fetch.start()

    @pl.when(s == n_steps - 1)
    def _():
        for k in (s - 1, s):
            for _, send in copies(k):
                send.wait()


def _with_next_weights(body, n_in, n_out, n_scratch, idx, n_steps, *refs):
    ins, rest = refs[:n_in], refs[n_in:]
    nw, rest = rest[:2], rest[2:]
    outs, rest = rest[:n_out], rest[n_out:]
    bf, rest = rest[:2], rest[2:]
    scratch, staging = rest[:n_scratch], rest[n_scratch:]
    _cast_next_weights(*nw, *bf, *staging, idx=idx, n_steps=n_steps)
    body(*ins, *outs, *scratch)


def _launch(name, body, grid, ins, in_specs, out_specs, out_shape, scratch, aliases, next_w):
    if next_w is not None:
        nw_in, nw_out, nidx = next_w
        n_steps = grid[0] * grid[1]
        rows = nw_in.shape[1] // n_steps
        assert nw_in.shape[1] == nw_out.shape[1] == rows * n_steps and rows % 16 == 0
        any_spec = pl.BlockSpec(memory_space=pl.ANY)
        body = functools.partial(_with_next_weights, body, len(ins), len(out_specs), len(scratch), nidx, n_steps)
        ins = ins + [nw_in, nw_out]
        in_specs = in_specs + [any_spec, any_spec]
        out_specs = out_specs + [any_spec, any_spec]
        out_shape = out_shape + [jax.ShapeDtypeStruct(w.shape[1:], jnp.bfloat16) for w in (nw_in, nw_out)]
        scratch = scratch + [pltpu.VMEM((2, rows, nw_in.shape[2]), jnp.float32),
                             pltpu.VMEM((2, rows, nw_out.shape[2]), jnp.float32),
                             pltpu.VMEM((2, rows, nw_in.shape[2]), jnp.bfloat16),
                             pltpu.VMEM((2, rows, nw_out.shape[2]), jnp.bfloat16),
                             pltpu.SemaphoreType.DMA((4, 2))]
    return pl.pallas_call(body, grid=grid, in_specs=in_specs, out_specs=out_specs, out_shape=out_shape,
                          scratch_shapes=scratch, input_output_aliases=aliases,
                          compiler_params=_compiler_params(), name=name)(*ins)


def _layer_spec(arr, idx):
    if arr.ndim == 2:
        return pl.BlockSpec(arr.shape, lambda b, j: (0, 0), pipeline_mode=pl.Buffered(1))
    tail = arr.shape[1:]
    return pl.BlockSpec((None,) + tail, lambda b, j: (idx,) + (0,) * len(tail), pipeline_mode=pl.Buffered(1))


def _stacked_spec(rows, r, idx, n_layers, whole):
    lead = n_layers if whole else None
    l0 = 0 if whole else idx
    if r == 1:
        return pl.BlockSpec((lead, 1, rows, BRANCH), lambda b, j: (l0, b, 0, 0))
    return pl.BlockSpec((lead, rows, r, BRANCH), lambda b, j: (l0, 0, b, 0))


def _stacked_shape(n_layers, bsz, rows, r, dtype):
    shape = (n_layers, bsz, rows, BRANCH) if r == 1 else (n_layers, rows, bsz, BRANCH)
    return jax.ShapeDtypeStruct(shape, dtype)


def _x_spec(tt, r):
    if r == 1:
        return pl.BlockSpec((1, tt, D_MODEL), lambda b, j: (b, j, 0))
    return pl.BlockSpec((tt, r, D_MODEL), lambda b, j: (0, b, 0))


def _compiler_params():
    return pltpu.CompilerParams(dimension_semantics=("arbitrary", "arbitrary"),
                                vmem_limit_bytes=VMEM_LIMIT_BYTES)


def _alias_inputs(ins, in_specs, prev_new):
    aliases = {}
    for k, arr in enumerate(prev_new or ()):
        aliases[len(ins)] = 1 + k
        ins.append(arr)
        in_specs.append(pl.BlockSpec(memory_space=pl.ANY))
    return aliases


def _even_layer(x, states, params, prev_new, *, layer, idx, n_layers, tt, ts, r, start_pos, next_w=None,
                x_native=False):
    bsz, t = (x.shape[0], x.shape[1]) if r == 1 or x_native else (x.shape[1], x.shape[0])
    n_t = t // tt
    first = prev_new is None
    pre_g, post_g, w_in, w_out, conv_w, pool_w, pool_scale = params
    ins, in_specs = [x], [pl.BlockSpec(memory_space=pl.ANY) if x_native else _x_spec(tt, r)]
    if states is not None:
        ins += list(states)
        in_specs += [_stacked_spec(CONV_A - 1, r, idx, n_layers, False),
                     _stacked_spec(POOL_HIST, r, idx, n_layers, False)]
    ins += [pre_g, post_g, w_in, w_out, conv_w, pool_w, pool_scale]
    in_specs += [_layer_spec(pre_g, layer), _layer_spec(post_g, layer), _layer_spec(w_in, idx),
                 _layer_spec(w_out, idx), _layer_spec(conv_w, idx), _layer_spec(pool_w, idx),
                 _layer_spec(pool_scale, idx)]
    aliases = _alias_inputs(ins, in_specs, prev_new)
    body = functools.partial(_even_kernel, tt=tt, ts=ts, r=r, n_t=n_t, start_pos=start_pos, has_hist=states is not None,
                             n_alias=len(aliases), fill=(idx, n_layers) if first else None, layer=layer, idx=idx,
                             x_native=x_native)
    out_specs = [_x_spec(tt, r), _stacked_spec(CONV_A - 1, r, idx, n_layers, first),
                 _stacked_spec(POOL_HIST, r, idx, n_layers, first)]
    out_shape = [jax.ShapeDtypeStruct((bsz, t, D_MODEL) if r == 1 else (t, bsz, D_MODEL), x.dtype),
                 _stacked_shape(n_layers, bsz, CONV_A - 1, r, x.dtype),
                 _stacked_shape(n_layers, bsz, POOL_HIST, r, x.dtype)]
    scratch = [pltpu.VMEM((N_SLABS, (PAD_A + tt) * r, LANES), jnp.float32),
               pltpu.VMEM((N_SLABS, (PAD_B + tt) * r, LANES), jnp.float32)]
    if x_native:
        assert r > 1 and n_t == 1
        scratch += [pltpu.VMEM((bsz // r, tt, r, D_MODEL), jnp.float32), pltpu.SemaphoreType.DMA((bsz // r, tt))]
    return _launch("even_layer", body, (bsz // r, n_t), ins, in_specs, out_specs, out_shape, scratch, aliases, next_w)


def _odd_layer(x, state, params, prev_new, *, layer, idx, n_layers, tt, ts, r, emit_vn, next_w=None, y_native=False):
    bsz, t = (x.shape[0], x.shape[1]) if r == 1 else (x.shape[1], x.shape[0])
    n_t = t // tt
    first = prev_new is None
    ins, in_specs = [x], [_x_spec(tt, r)]
    if state is not None:
        ins.append(state)
        in_specs.append(_stacked_spec(CONV_D - 1, r, idx, n_layers, False))
    for k, arr in enumerate(params):
        ins.append(arr)
        in_specs.append(_layer_spec(arr, layer if k < 2 else idx))
    out_specs = [_x_spec(tt, r), _stacked_spec(CONV_D - 1, r, idx, n_layers, first)]
    out_shape = [jax.ShapeDtypeStruct(x.shape, x.dtype), _stacked_shape(n_layers, bsz, CONV_D - 1, r, x.dtype)]
    if emit_vn:
        out_specs.append(_stacked_spec(tt, r, idx, n_layers, first))
        out_shape.append(_stacked_shape(n_layers, bsz, t, r, x.dtype))
    aliases = _alias_inputs(ins, in_specs, prev_new)
    body = functools.partial(_odd_kernel, tt=tt, ts=ts, r=r, n_t=n_t, has_hist=state is not None, emit_vn=emit_vn,
                             n_alias=len(aliases), fill=(idx, n_layers) if first else None, layer=layer, idx=idx,
                             y_native=y_native)
    scratch = [pltpu.VMEM((N_SLABS, (PAD_D + tt) * r, LANES), jnp.float32),
               pltpu.VMEM((tt * r, D_MODEL), jnp.bfloat16)]
    if y_native:
        assert r > 1 and n_t == 1
        out_specs[0] = pl.BlockSpec(memory_space=pl.ANY)
        out_shape[0] = jax.ShapeDtypeStruct((bsz, t, D_MODEL), x.dtype)
        scratch += [pltpu.VMEM((tt, r, D_MODEL), jnp.float32), pltpu.SemaphoreType.DMA((tt,))]
    return _launch("odd_layer", body, (bsz // r, n_t), ins, in_specs, out_specs, out_shape, scratch, aliases, next_w)


def kernel(x_prompt, x_sample, state_conv_a, state_pool_b, state_conv_d, norm_pre, norm_post, w_in_even, w_out_even, conv_a_w, pool_w, pool_scale, w_in_odd, w_out_odd, sgu_ln_g, sgu_ln_b, sgu_w, sgu_b, conf_dw_w, conf_dw_b, conf_ln_g, conf_ln_b):
    xp = x_prompt
    dec_seq = x_sample.shape[1]
    depth = norm_pre.shape[0]
    n_even, n_odd = w_in_even.shape[0], w_in_odd.shape[0]
    bf16 = jnp.bfloat16
    time_major = lambda a: jnp.swapaxes(a, -3, -2)

    even_tail = (conv_a_w, pool_w.astype(bf16), pool_scale)
    odd_tail = (conf_dw_w, conf_dw_b, conf_ln_g, conf_ln_b)
    mix_p = (sgu_w, jnp.swapaxes(sgu_b, 1, 2))
    mix_s = (jnp.repeat(jnp.transpose(sgu_w[:, :, :dec_seq, :dec_seq], (0, 2, 3, 1)), LANES, axis=-1),
             jnp.repeat(jnp.swapaxes(sgu_b[:, :, :dec_seq], 1, 2), LANES, axis=-1))

    xs = x_sample
    even_states_s = (time_major(state_conv_a), time_major(state_pool_b))
    odd_state_s = time_major(state_conv_d)
    sample = dict(tt=dec_seq, ts=dec_seq, r=SAMPLE_SEQS)
    new_even_p = new_even_s = new_odd_p = new_odd_s = None
    w_in_bf, w_out_bf = w_in_even[0].astype(bf16), w_out_even[0].astype(bf16)
    for l in range(depth):
        i = l // 2
        stacks = (w_in_odd, w_out_odd, i) if l % 2 == 0 else (w_in_even, w_out_even, i + 1)
        next_w = stacks if l + 1 < depth else None
        if l % 2 == 0:
            common = dict(layer=l, idx=i, n_layers=n_even)
            params = (norm_pre, norm_post, w_in_bf, w_out_bf) + even_tail
            outs = _even_layer(xp, None, params, new_even_p, start_pos=0, tt=PROMPT_TILE, ts=EVEN_SUB, r=1,
                               next_w=next_w, **common)
            xp, new_even_p = outs[0], outs[1:3]
            xs, *new_even_s = _even_layer(xs, even_states_s, params, new_even_s, start_pos=PAST_LEN,
                                          x_native=l == 0, **common, **sample)
        else:
            common = dict(layer=l, idx=i, n_layers=n_odd)
            head = (norm_pre, norm_post, w_in_bf, w_out_bf, sgu_ln_g, sgu_ln_b)
            outs = _odd_layer(xp, None, head + mix_p + odd_tail, new_odd_p, emit_vn=False, tt=PROMPT_TILE, ts=ODD_SUB,
                              r=1, next_w=next_w, **common)
            xp, new_odd_p = outs[0], outs[1:2]
            xs, *new_odd_s = _odd_layer(xs, odd_state_s, head + mix_s + odd_tail, new_odd_s, emit_vn=True,
                                        y_native=l + 1 == depth, **common, **sample)
        if next_w is not None:
            w_in_bf, w_out_bf = outs[-2:]
    ca_p, pb_p = new_even_p
    ca_s, pb_s = new_even_s
    (cd_p,) = new_odd_p
    cd_s, v_s = new_odd_s
    return (xp, xs, ca_p, time_major(ca_s), pb_p, time_major(pb_s),
            cd_p, time_major(cd_s), time_major(v_s))
```
